```python
import math
import jax
import jax.numpy as jnp
from jax import lax
import numpy as np

D_MODEL = 2048
BATCH = 2
SEQ = 4096
DEPTH = 2

HEAD_DIM = 128
DILATION_PATTERNS = ((128, 1), (512, 4), (2048, 16))
N_ATTN_GROUPS = len(DILATION_PATTERNS)
ATTN_WIDTH = 3 * D_MODEL // 4
N_ATTN_HEADS = ATTN_WIDTH // HEAD_DIM
HEADS_PER_GROUP = N_ATTN_HEADS // N_ATTN_GROUPS
ATTN_OUT_WIDTH = HEADS_PER_GROUP * HEAD_DIM
Q_BLOCK = 128
POOL_SIZES = (2, 4, 8, 16)
N_POOL_GROUPS = len(POOL_SIZES)
POOL_WIDTH = D_MODEL // 4
POOL_GROUP_WIDTH = POOL_WIDTH // N_POOL_GROUPS
N_BRANCHES = 2
IN_WIDTH = 3 * ATTN_WIDTH + POOL_WIDTH + N_BRANCHES * D_MODEL
N_BUCKETS = 32
MAX_DISTANCE = 2048
D_FF = 11 * D_MODEL // 4
N_EXPERTS = 8
TOP_K = 2
D_FF_EXPERT = 7 * D_MODEL // 2
N_DENSE = (DEPTH + 1) // 2
N_MOE = DEPTH // 2
RMS_EPS = 1e-6

kernel_name = "hybrid_dilated_attn_pool_moe_block"


def rms_norm(x, gain):
    xf = x.astype(jnp.float32)
    y = xf * lax.rsqrt(jnp.mean(xf * xf, axis=-1, keepdims=True) + RMS_EPS)
    return (y * gain.astype(jnp.float32)).astype(x.dtype)


def t5_causal_bucket(dist):
    max_exact = N_BUCKETS // 2
    df = jnp.maximum(dist, 1).astype(jnp.float32)
    large = max_exact + (jnp.log(df / max_exact) / math.log(MAX_DISTANCE / max_exact)
                         * (N_BUCKETS - max_exact)).astype(jnp.int32)
    large = jnp.minimum(large, N_BUCKETS - 1)
    return jnp.where(dist < max_exact, dist, large)


def from_streams(t, n_stream):
    b, d = t.shape[:2]
    rest = t.shape[4:]
    t = t.reshape((b, d, -1) + rest)[:, :, :n_stream]
    return jnp.moveaxis(t, 1, 2).reshape((b, n_stream * d) + rest)


def dilated_window_attention(q, k, v, bias_table, window, dilation):
    b, s, h, e = q.shape
    span = window // dilation
    n_stream = s // dilation
    n_blk = -(-n_stream // Q_BLOCK)
    pad = n_blk * Q_BLOCK - n_stream

    def to_streams(t):
        t = jnp.moveaxis(t.reshape(b, n_stream, dilation, h, e), 2, 1)
        t = jnp.pad(t, ((0, 0), (0, 0), (0, pad), (0, 0), (0, 0)))
        return t.reshape(b, dilation, n_blk, Q_BLOCK, h, e)

    def with_prev(t):
        prev = jnp.pad(t, ((0, 0), (0, 0), (1, 0), (0, 0), (0, 0), (0, 0)))[:, :, :-1]
        return jnp.concatenate([prev, t], axis=3)

    qb = to_streams(q)
    kw = with_prev(to_streams(k))
    vw = with_prev(to_streams(v))

    qi = jnp.arange(Q_BLOCK)[:, None]
    kj = jnp.arange(2 * Q_BLOCK)[None, :]
    step = qi + Q_BLOCK - kj
    key_idx = jnp.arange(n_blk)[:, None, None] * Q_BLOCK - Q_BLOCK + kj[None]
    valid = (step >= 0) & (step <= span) & (key_idx >= 0)
    bucket = t5_causal_bucket(jnp.clip(step, 0, span) * dilation)
    bias = jnp.moveaxis(bias_table[bucket], -1, 0).astype(jnp.float32)

    scores = jnp.einsum("brnqhe,brnkhe->brnhqk", qb, kw, preferred_element_type=jnp.float32)
    scores = scores * (HEAD_DIM ** -0.5) + bias[None, None, None]
    scores = jnp.where(valid[None, None, :, None], scores, -jnp.inf)
    m = jnp.max(scores, axis=-1, keepdims=True)
    p = jnp.exp(scores - m)
    denom = jnp.sum(p, axis=-1, keepdims=True)
    o = jnp.einsum("brnhqk,brnkhe->brnqhe", p.astype(v.dtype), vw,
                   preferred_element_type=jnp.float32)
    o = o / jnp.swapaxes(denom, 3, 4)
    lse = jnp.swapaxes((m + jnp.log(denom))[..., 0], 3, 4)
    return from_streams(o, n_stream), from_streams(lse, n_stream)


def multiscale_pool(p_in, w_pool, pool_scale):
    b, s, _ = p_in.shape
    pf = p_in.astype(jnp.float32).reshape(b, s, N_POOL_GROUPS, POOL_GROUP_WIDTH)
    csum = jnp.pad(jnp.cumsum(pf, axis=1), ((0, 0), (1, 0), (0, 0), (0, 0)))
    t = jnp.arange(s)
    sizes = jnp.array(POOL_SIZES, dtype=jnp.int32)
    starts = jnp.maximum(t[:, None] + 1 - sizes[None, :], 0)
    window_sum = csum[:, 1:] - csum[:, starts, jnp.arange(N_POOL_GROUPS)]
    count = (t[:, None] + 1 - starts).astype(jnp.float32)
    y = (window_sum / count[None, :, :, None] - pf).astype(p_in.dtype)
    y = jnp.einsum("bsgc,gcd->bsgd", y, w_pool)
    return y.reshape(b, s, POOL_WIDTH) * pool_scale


def hybrid_mixer(u, w_in, rel_bias, w_pool, pool_scale, w_attn_out, w_pool_out, w_out):
    b, s, _ = u.shape
    z = u @ w_in
    q, k, v, p_in, g_in = jnp.split(
        z, [ATTN_WIDTH, 2 * ATTN_WIDTH, 3 * ATTN_WIDTH, 3 * ATTN_WIDTH + POOL_WIDTH], axis=-1)
    head_shape = (b, s, N_ATTN_GROUPS, HEADS_PER_GROUP, HEAD_DIM)
    q = q.reshape(head_shape)
    k = k.reshape(head_shape)
    v = v.reshape(head_shape)
    outs, lses = [], []
    for gi, (window, dilation) in enumerate(DILATION_PATTERNS):
        heads = slice(gi * HEADS_PER_GROUP, (gi + 1) * HEADS_PER_GROUP)
        o, lse = dilated_window_attention(q[:, :, gi], k[:, :, gi], v[:, :, gi],
                                          rel_bias[:, heads], window, dilation)
        outs.append(o)
        lses.append(lse)
    mix_w = jax.nn.softmax(jnp.stack(lses, axis=0), axis=0)
    o_attn = jnp.sum(mix_w[..., None] * jnp.stack(outs, axis=0), axis=0)
    o_attn = o_attn.astype(u.dtype).reshape(b, s, ATTN_OUT_WIDTH)
    o_pool = multiscale_pool(p_in, w_pool, pool_scale)
    gates = jax.nn.sigmoid(g_in.astype(jnp.float32)).astype(u.dtype).reshape(b, s, N_BRANCHES, D_MODEL)
    merged = gates[:, :, 0] * (o_attn @ w_attn_out) + gates[:, :, 1] * (o_pool @ w_pool_out)
    return merged @ w_out


def dense_swiglu(u, w_gate, w_up, w_down):
    return (jax.nn.silu(u @ w_gate) * (u @ w_up)) @ w_down


def moe_swiglu(u, w_router, w_gate, w_up, w_down):
    b, s, d = u.shape
    tok = u.reshape(b * s, d)
    logits = (tok @ w_router).astype(jnp.float32)
    top_vals, top_idx = lax.top_k(logits, TOP_K)
    top_w = jax.nn.softmax(top_vals, axis=-1)
    combine = jnp.sum(jax.nn.one_hot(top_idx, N_EXPERTS, dtype=jnp.float32) * top_w[..., None], axis=1)
    y = jnp.zeros((b * s, d), jnp.float32)
    for e in range(N_EXPERTS):
        h = jax.nn.silu(tok @ w_gate[e]) * (tok @ w_up[e])
        y = y + combine[:, e:e + 1] * (h @ w_down[e]).astype(jnp.float32)
    return y.astype(u.dtype).reshape(b, s, d)


def setup_inputs(seed: int = 0) -> dict:
    key = jax.random.key(seed)
    ks = jax.random.split(key, 16)
    f32 = jnp.float32

    def normal(k, shape, fan_in):
        return jax.random.normal(k, shape, f32) * (fan_in ** -0.5)

    return {
        "x": jax.random.normal(ks[0], (BATCH, SEQ, D_MODEL), f32),
        "w_in": normal(ks[1], (DEPTH, D_MODEL, IN_WIDTH), D_MODEL),
        "rel_bias": 0.5 * jax.random.normal(ks[2], (N_BUCKETS, N_ATTN_HEADS), f32),
        "w_pool": normal(ks[3], (DEPTH, N_POOL_GROUPS, POOL_GROUP_WIDTH, POOL_GROUP_WIDTH), POOL_GROUP_WIDTH),
        "pool_scale": 1.0 + 0.02 * jax.random.normal(ks[4], (DEPTH, POOL_WIDTH), f32),
        "w_attn_out": normal(ks[5], (DEPTH, ATTN_OUT_WIDTH, D_MODEL), ATTN_OUT_WIDTH),
        "w_pool_out": normal(ks[6], (DEPTH, POOL_WIDTH, D_MODEL), POOL_WIDTH),
        "w_out": normal(ks[7], (DEPTH, D_MODEL, D_MODEL), D_MODEL),
        "norm_gains": 1.0 + 0.05 * jax.random.normal(ks[8], (DEPTH, 4, D_MODEL), f32),
        "dense_w_gate": normal(ks[9], (N_DENSE, D_MODEL, D_FF), D_MODEL),
        "dense_w_up": normal(ks[10], (N_DENSE, D_MODEL, D_FF), D_MODEL),
        "dense_w_down": normal(ks[11], (N_DENSE, D_FF, D_MODEL), D_FF),
        "moe_w_router": normal(ks[12], (N_MOE, D_MODEL, N_EXPERTS), D_MODEL),
        "moe_w_gate": normal(ks[13], (N_MOE, N_EXPERTS, D_MODEL, D_FF_EXPERT), D_MODEL),
        "moe_w_up": normal(ks[14], (N_MOE, N_EXPERTS, D_MODEL, D_FF_EXPERT), D_MODEL),
        "moe_w_down": normal(ks[15], (N_MOE, N_EXPERTS, D_FF_EXPERT, D_MODEL), D_FF_EXPERT),
    }


def reference(x, w_in, rel_bias, w_pool, pool_scale, w_attn_out, w_pool_out, w_out, norm_gains,
              dense_w_gate, dense_w_up, dense_w_down, moe_w_router, moe_w_gate, moe_w_up, moe_w_down):
    h = x
    for layer in range(DEPTH):
        gains = norm_gains[layer]
        u = rms_norm(h, gains[0])
        mix = hybrid_mixer(u, w_in[layer], rel_bias, w_pool[layer], pool_scale[layer],
                           w_attn_out[layer], w_pool_out[layer], w_out[layer])
        h = h + rms_norm(mix, gains[1])
        u = rms_norm(h, gains[2])
        j = layer // 2
        if layer % 2 == 0:
            f = dense_swiglu(u, dense_w_gate[j], dense_w_up[j], dense_w_down[j])
        else:
            f = moe_swiglu(u, moe_w_router[j], moe_w_gate[j], moe_w_up[j], moe_w_down[j])
        h = h + rms_norm(f, gains[3])
    return h
```

```python
import functools
import math

import jax
import jax.numpy as jnp
from jax import lax
from jax.experimental import pallas as pl
from jax.experimental.pallas import tpu as pltpu

F32 = jnp.float32
BF16 = jnp.bfloat16

RMS_EPS = 1e-6
HEAD_DIM = 128
Q_BLOCK = 128
DILATION_PATTERNS = ((128, 1), (512, 4), (2048, 16))
N_ATTN_GROUPS = len(DILATION_PATTERNS)
HEADS_PER_GROUP = 4
GROUP_WIDTH = HEADS_PER_GROUP * HEAD_DIM
POOL_SIZES = (2, 4, 8, 16)
POOL_GROUP_WIDTH = 128
POOL_HALO = 16
N_BUCKETS = 32
MAX_DISTANCE = 2048
N_EXPERTS = 8
TOP_K = 2

V7X_VMEM_BYTES = 64 * 1024 * 1024


def _vmem_limit(pipelined_bytes, resident_bytes, temp_bytes):
    return min(2 * pipelined_bytes + resident_bytes + temp_bytes, V7X_VMEM_BYTES)


def _nbytes(shape, dtype):
    return math.prod(shape) * jnp.dtype(dtype).itemsize


def _rms_norm_f32(x, gain):
    ms = jnp.mean(x * x, axis=-1, keepdims=True)
    return x * lax.rsqrt(ms + RMS_EPS) * gain


def _inproj_kernel(h_ref, g_ref, w_ref, z_ref, u_scr):
    @pl.when(pl.program_id(1) == 0)
    def _():
        u_scr[...] = _rms_norm_f32(h_ref[...], g_ref[...]).astype(BF16)

    z_ref[...] = jnp.dot(u_scr[...], w_ref[...].astype(BF16), preferred_element_type=F32)


def _in_proj(h, gain, w, *, tm=1024, tn=512):
    n, d = h.shape
    width = w.shape[1]
    assert n % tm == 0 and width % tn == 0
    limit = _vmem_limit(
        _nbytes((tm, d), F32) + _nbytes((d, tn), F32) + _nbytes((tm, tn), F32),
        _nbytes((tm, d), BF16),
        2 * _nbytes((tm, d), F32) + _nbytes((d, tn), BF16) + _nbytes((tm, tn), F32))
    return pl.pallas_call(
        _inproj_kernel,
        grid=(n // tm, width // tn),
        in_specs=[pl.BlockSpec((tm, d), lambda i, j: (i, 0)),
                  pl.BlockSpec((1, d), lambda i, j: (0, 0)),
                  pl.BlockSpec((d, tn), lambda i, j: (0, j))],
        out_specs=pl.BlockSpec((tm, tn), lambda i, j: (i, j)),
        out_shape=jax.ShapeDtypeStruct((n, width), F32),
        scratch_shapes=[pltpu.VMEM((tm, d), BF16)],
        compiler_params=pltpu.CompilerParams(dimension_semantics=("arbitrary", "arbitrary"),
                                             vmem_limit_bytes=limit),
        name="in_proj",
    )(h, gain, w)


def _attn_kernel(q_ref, k_ref, v_ref, kp_ref, vp_ref, bias_ref, o_ref, l_ref, *, n_sub, scale):
    first_chunk = pl.program_id(2) == 0
    key_col = lax.broadcasted_iota(jnp.int32, (Q_BLOCK, 2 * Q_BLOCK), 1)
    for n in range(n_sub):
        rows = slice(n * Q_BLOCK, (n + 1) * Q_BLOCK)
        prev_rows = slice((n - 1) * Q_BLOCK, n * Q_BLOCK)
        for hh in range(HEADS_PER_GROUP):
            cols = slice(hh * HEAD_DIM, (hh + 1) * HEAD_DIM)
            q = q_ref[0, rows, cols].astype(BF16)
            if n == 0:
                k_prev, v_prev = kp_ref[0, :, cols], vp_ref[0, :, cols]
            else:
                k_prev, v_prev = k_ref[0, prev_rows, cols], v_ref[0, prev_rows, cols]
            kc = jnp.concatenate([k_prev, k_ref[0, rows, cols]], axis=0).astype(BF16)
            vc = jnp.concatenate([v_prev, v_ref[0, rows, cols]], axis=0).astype(BF16)
            s = lax.dot_general(q, kc, (((1,), (1,)), ((), ())), preferred_element_type=F32)
            s = s * scale + bias_ref[hh]
            if n == 0:
                s = jnp.where(jnp.logical_and(first_chunk, key_col < Q_BLOCK), -jnp.inf, s)
            m = jnp.max(s, axis=-1, keepdims=True)
            p = jnp.exp(s - m)
            den = jnp.sum(p, axis=-1, keepdims=True)
            o = jnp.dot(p.astype(BF16), vc, preferred_element_type=F32) / den
            o_ref[0, rows, cols] = o
            l_ref[0, rows, cols] = jnp.broadcast_to(m + jnp.log(den), (Q_BLOCK, HEAD_DIM))


def _t5_causal_bucket(dist):
    max_exact = N_BUCKETS // 2
    df = jnp.maximum(dist, 1).astype(F32)
    large = max_exact + (jnp.log(df / max_exact) / math.log(MAX_DISTANCE / max_exact)
                         * (N_BUCKETS - max_exact)).astype(jnp.int32)
    large = jnp.minimum(large, N_BUCKETS - 1)
    return jnp.where(dist < max_exact, dist, large)


def _band_bias(bias_table, window, dilation):
    span = window // dilation
    qi = jnp.arange(Q_BLOCK)[:, None]
    kj = jnp.arange(2 * Q_BLOCK)[None, :]
    step = qi + Q_BLOCK - kj
    valid = (step >= 0) & (step <= span)
    bucket = _t5_causal_bucket(jnp.clip(step, 0, span) * dilation)
    bias = jnp.moveaxis(bias_table[bucket], -1, 0).astype(F32)
    return jnp.where(valid[None], bias, -jnp.inf)


def _group_attention(z, bias, *, batch, seq, group, dilation, in_width, attn_width):
    length = seq // dilation
    tq = min(length, 4 * Q_BLOCK)
    assert length % tq == 0 and tq % Q_BLOCK == 0
    n_sub = tq // Q_BLOCK
    cols_per_row = in_width // GROUP_WIDTH
    assert in_width % GROUP_WIDTH == 0 and attn_width % GROUP_WIDTH == 0
    k_off = attn_width // GROUP_WIDTH
    zv = z.reshape(batch, length, dilation * in_width)

    def cur(off):
        return pl.BlockSpec((1, tq, GROUP_WIDTH), lambda b, r, c: (b, c, r * cols_per_row + off + group))

    def prev(off):
        return pl.BlockSpec((1, Q_BLOCK, GROUP_WIDTH),
                            lambda b, r, c: (b, jnp.maximum(c * n_sub - 1, 0), r * cols_per_row + off + group))

    out_spec = pl.BlockSpec((1, tq, GROUP_WIDTH), lambda b, r, c: (b, c, r))
    out_sds = jax.ShapeDtypeStruct((batch, length, dilation * GROUP_WIDTH), F32)
    blk = _nbytes((tq, GROUP_WIDTH), F32)
    limit = _vmem_limit(5 * blk + 2 * _nbytes((Q_BLOCK, GROUP_WIDTH), F32) + _nbytes(bias.shape, F32),
                        0, 8 * blk)
    o, lse = pl.pallas_call(
        functools.partial(_attn_kernel, n_sub=n_sub, scale=HEAD_DIM ** -0.5),
        grid=(batch, dilation, length // tq),
        in_specs=[cur(0), cur(k_off), cur(2 * k_off), prev(k_off), prev(2 * k_off),
                  pl.BlockSpec(bias.shape, lambda b, r, c: (0, 0, 0))],
        out_specs=[out_spec, out_spec],
        out_shape=[out_sds, out_sds],
        compiler_params=pltpu.CompilerParams(dimension_semantics=("arbitrary",) * 3, vmem_limit_bytes=limit),
        name=f"dilated_attn_d{dilation}",
    )(zv, zv, zv, zv, zv, bias)
    return o.reshape(batch * seq, GROUP_WIDTH), lse.reshape(batch * seq, GROUP_WIDTH)


def _merge_kernel(o0_ref, o1_ref, o2_ref, l0_ref, l1_ref, l2_ref, p_ref, pp_ref,
                  ga0_ref, ga1_ref, gb0_ref, gb1_ref, h_ref,
                  wpool_ref, pscale_ref, wao_ref, wpo_ref, wout_ref, gain_ref, out_ref,
                  *, tm, tiles_per_seq):
    tile_in_seq = pl.program_id(0) % tiles_per_seq

    la, lb, lc = l0_ref[...], l1_ref[...], l2_ref[...]
    m = jnp.maximum(jnp.maximum(la, lb), lc)
    ea, eb, ec = jnp.exp(la - m), jnp.exp(lb - m), jnp.exp(lc - m)
    o_attn = (ea * o0_ref[...] + eb * o1_ref[...] + ec * o2_ref[...]) / (ea + eb + ec)

    halo = jnp.where(tile_in_seq == 0, 0.0, pp_ref[...])
    xe = jnp.concatenate([halo, p_ref[...]], axis=0)
    t = tile_in_seq * tm + lax.broadcasted_iota(jnp.int32, (tm, 1), 0)
    pooled = []
    for g, size in enumerate(POOL_SIZES):
        a = xe[:, g * POOL_GROUP_WIDTH:(g + 1) * POOL_GROUP_WIDTH]
        s, shift = a, 1
        while shift < size:
            s = s + pltpu.roll(s, shift, axis=0)
            shift *= 2
        count = jnp.minimum(t + 1, size).astype(F32)
        y = (s[POOL_HALO:] / count - a[POOL_HALO:]).astype(BF16)
        pooled.append(jnp.dot(y, wpool_ref[g], preferred_element_type=F32))
    o_pool = jnp.concatenate(pooled, axis=1) * pscale_ref[...]

    attn_proj = jnp.dot(o_attn.astype(BF16), wao_ref[...], preferred_element_type=F32)
    pool_proj = jnp.dot(o_pool.astype(BF16), wpo_ref[...], preferred_element_type=F32)
    gate_a = jax.nn.sigmoid(jnp.concatenate([ga0_ref[...], ga1_ref[...]], axis=1))
    gate_b = jax.nn.sigmoid(jnp.concatenate([gb0_ref[...], gb1_ref[...]], axis=1))
    merged = gate_a * attn_proj + gate_b * pool_proj
    mix = jnp.dot(merged.astype(BF16), wout_ref[...], preferred_element_type=F32)
    out_ref[...] = h_ref[...] + _rms_norm_f32(mix, gain_ref[...])


def _merge(h, z, attn_outs, attn_lses, w_pool, pool_scale, w_attn_out, w_pool_out, w_out, gain,
           *, seq, attn_width, tm=256):
    n, d = h.shape
    pool_width = w_pool.shape[0] * POOL_GROUP_WIDTH
    assert pool_width == GROUP_WIDTH and len(attn_outs) == 3
    assert n % tm == 0 and seq % tm == 0 and tm % POOL_HALO == 0
    p_col = 3 * attn_width // GROUP_WIDTH
    gate_w = d // 2
    gate_col = (3 * attn_width + pool_width) // gate_w
    assert (3 * attn_width + pool_width) % gate_w == 0

    row_blk = lambda c: pl.BlockSpec((tm, GROUP_WIDTH), lambda i: (i, c))
    gate_blk = lambda c: pl.BlockSpec((tm, gate_w), lambda i: (i, gate_col + c))
    halo_blk = pl.BlockSpec((POOL_HALO, GROUP_WIDTH),
                            lambda i: (jnp.maximum(i * (tm // POOL_HALO) - 1, 0), p_col))
    full = lambda a: pl.BlockSpec(a.shape, lambda i: (0,) * a.ndim)
    weights = (w_pool, pool_scale, w_attn_out, w_pool_out, w_out, gain)

    act = 7 * _nbytes((tm, GROUP_WIDTH), F32) + 4 * _nbytes((tm, gate_w), F32) + 2 * _nbytes((tm, d), F32)
    wbytes = sum(_nbytes(a.shape, a.dtype) for a in weights)
    limit = _vmem_limit(act + wbytes, 0, 8 * _nbytes((tm, d), F32))
    return pl.pallas_call(
        functools.partial(_merge_kernel, tm=tm, tiles_per_seq=seq // tm),
        grid=(n // tm,),
        in_specs=[row_blk(0)] * 6 + [row_blk(p_col), halo_blk] + [gate_blk(c) for c in range(4)]
                 + [pl.BlockSpec((tm, d), lambda i: (i, 0))] + [full(a) for a in weights],
        out_specs=pl.BlockSpec((tm, d), lambda i: (i, 0)),
        out_shape=jax.ShapeDtypeStruct((n, d), F32),
        compiler_params=pltpu.CompilerParams(dimension_semantics=("arbitrary",), vmem_limit_bytes=limit),
        name="mixer_merge",
    )(*attn_outs, *attn_lses, z, z, z, z, z, z, h, *weights)


def _unpack_bf16_pair(words):
    lo = lax.bitcast_convert_type(words << 16, F32)
    hi = lax.bitcast_convert_type(words & jnp.uint32(0xFFFF0000), F32)
    return jnp.concatenate([lo, hi], axis=1).astype(BF16)


def _ffn_kernel(tile_expert_ref, n_tiles_ref, x_ref, gin_ref, wg_ref, wu_ref, wd_ref, gout_ref, out_ref,
                u_scr, acc_scr, *, dense):
    del tile_expert_ref
    i, j = pl.program_id(0), pl.program_id(1)

    @pl.when(j == 0)
    def _():
        if dense:
            u_scr[...] = _rms_norm_f32(x_ref[...], gin_ref[...]).astype(BF16)
        else:
            u_scr[...] = _unpack_bf16_pair(x_ref[...])
        acc_scr[...] = jnp.zeros_like(acc_scr)

    @pl.when(i < n_tiles_ref[0])
    def _():
        u = u_scr[...]
        gate = jnp.dot(u, wg_ref[...].astype(BF16), preferred_element_type=F32)
        up = jnp.dot(u, wu_ref[...].astype(BF16), preferred_element_type=F32)
        hidden = (gate * jax.nn.sigmoid(gate) * up).astype(BF16)
        acc_scr[...] += jnp.dot(hidden, wd_ref[...].astype(BF16), preferred_element_type=F32)

    @pl.when(j == pl.num_programs(1) - 1)
    def _():
        if dense:
            out_ref[...] = x_ref[...] + _rms_norm_f32(acc_scr[...], gout_ref[...])
        else:
            out_ref[...] = acc_scr[...]


def _ffn(x, tile_expert, n_tiles, gain_in, w_gate, w_up, w_down, gain_out, *, dense, tm, tf):
    rows = x.shape[0]
    n_exp, d, ff = w_gate.shape
    assert rows % tm == 0 and ff % tf == 0 and x.shape[1] == (d if dense else d // 2)
    nj = ff // tf

    def ff_tile(i, j, nt):
        return jnp.where(i < nt[0], j, nj - 1)

    x_blk = pl.BlockSpec((tm, x.shape[1]), lambda i, j, te, nt: (i, 0))
    gain_blk = pl.BlockSpec((1, d), lambda i, j, te, nt: (0, 0))
    up_blk = pl.BlockSpec((None, d, tf), lambda i, j, te, nt: (te[i], 0, ff_tile(i, j, nt)))
    down_blk = pl.BlockSpec((None, tf, d), lambda i, j, te, nt: (te[i], ff_tile(i, j, nt), 0))
    limit = _vmem_limit(
        _nbytes((tm, x.shape[1]), x.dtype) + 3 * _nbytes((d, tf), F32) + _nbytes((tm, d), F32),
        _nbytes((tm, d), BF16) + _nbytes((tm, d), F32),
        3 * _nbytes((d, tf), BF16) + 4 * _nbytes((tm, tf), F32) + _nbytes((tm, d), F32))
    return pl.pallas_call(
        functools.partial(_ffn_kernel, dense=dense),
        grid_spec=pltpu.PrefetchScalarGridSpec(
            num_scalar_prefetch=2,
            grid=(rows // tm, nj),
            in_specs=[x_blk, gain_blk, up_blk, up_blk, down_blk, gain_blk],
            out_specs=pl.BlockSpec((tm, d), lambda i, j, te, nt: (i, 0)),
            scratch_shapes=[pltpu.VMEM((tm, d), BF16), pltpu.VMEM((tm, d), F32)]),
        out_shape=jax.ShapeDtypeStruct((rows, d), F32),
        compiler_params=pltpu.CompilerParams(dimension_semantics=("arbitrary", "arbitrary"),
                                             vmem_limit_bytes=limit),
        name="swiglu_dense" if dense else "swiglu_experts",
    )(tile_expert, n_tiles, x, gain_in, w_gate, w_up, w_down, gain_out)


def _router_kernel(h_ref, gain_ref, wrt_ref, xp_ref, idx_ref, wts_ref):
    u = _rms_norm_f32(h_ref[...], gain_ref[...])
    logits = lax.dot_general(wrt_ref[...], u, (((1,), (1,)), ((), ())),
                             precision=lax.Precision.HIGHEST, preferred_element_type=F32)
    n_exp = logits.shape[0]
    expert = lax.broadcasted_iota(jnp.int32, logits.shape, 0)
    v1 = jnp.max(logits, axis=0, keepdims=True)
    i1 = jnp.min(jnp.where(logits == v1, expert, n_exp), axis=0, keepdims=True)
    rest = jnp.where(expert == i1, -jnp.inf, logits)
    v2 = jnp.max(rest, axis=0, keepdims=True)
    i2 = jnp.min(jnp.where(rest == v2, expert, n_exp), axis=0, keepdims=True)
    e2 = jnp.exp(v2 - v1)
    idx_ref[...] = jnp.concatenate([i1, i2], axis=0)
    wts_ref[...] = jnp.concatenate([1.0 / (1.0 + e2), e2 / (1.0 + e2)], axis=0)

    bits = lax.bitcast_convert_type(u.astype(BF16).astype(F32), jnp.uint32)
    half = bits.shape[1] // 2
    xp_ref[...] = (bits[:, :half] >> 16) | bits[:, half:]


def _router(h, gain, w_router_t, *, tm=512):
    n, d = h.shape
    n_exp = w_router_t.shape[0]
    assert n % tm == 0
    limit = _vmem_limit(_nbytes((tm, d), F32) + _nbytes((tm, d // 2), jnp.uint32) + _nbytes((n_exp, d), F32),
                        0, 6 * _nbytes((tm, d), F32))
    return pl.pallas_call(
        _router_kernel,
        grid=(n // tm,),
        in_specs=[pl.BlockSpec((tm, d), lambda i: (i, 0)),
                  pl.BlockSpec((1, d), lambda i: (0, 0)),
                  pl.BlockSpec((n_exp, d), lambda i: (0, 0))],
        out_specs=[pl.BlockSpec((tm, d // 2), lambda i: (i, 0)),
                   pl.BlockSpec((TOP_K, tm), lambda i: (0, i)),
                   pl.BlockSpec((TOP_K, tm), lambda i: (0, i))],
        out_shape=[jax.ShapeDtypeStruct((n, d // 2), jnp.uint32),
                   jax.ShapeDtypeStruct((TOP_K, n), jnp.int32),
                   jax.ShapeDtypeStruct((TOP_K, n), F32)],
        compiler_params=pltpu.CompilerParams(dimension_semantics=("arbitrary",), vmem_limit_bytes=limit),
        name="moe_router",
    )(h, gain, w_router_t)


def _dispatch_kernel(src_ref, x_hbm, rows_hbm, sem, *, chunk):
    base = pl.program_id(0) * chunk

    def row_copy(r):
        return pltpu.make_async_copy(x_hbm.at[pl.ds(src_ref[base + r], 1), :],
                                     rows_hbm.at[pl.ds(base + r, 1), :], sem)

    def start(r, carry):
        row_copy(r).start()
        return carry

    def wait(r, carry):
        row_copy(r).wait()
        return carry

    lax.fori_loop(0, chunk, start, 0)
    lax.fori_loop(0, chunk, wait, 0)


def _dispatch(xp, src, *, chunk=256):
    n_rows = src.shape[0]
    assert n_rows % chunk == 0
    return pl.pallas_call(
        functools.partial(_dispatch_kernel, chunk=chunk),
        grid_spec=pltpu.PrefetchScalarGridSpec(
            num_scalar_prefetch=1,
            grid=(n_rows // chunk,),
            in_specs=[pl.BlockSpec(memory_space=pl.ANY)],
            out_specs=pl.BlockSpec(memory_space=pl.ANY),
            scratch_shapes=[pltpu.SemaphoreType.DMA]),
        out_shape=jax.ShapeDtypeStruct((n_rows, xp.shape[1]), xp.dtype),
        compiler_params=pltpu.CompilerParams(dimension_semantics=("arbitrary",)),
        name="moe_dispatch",
    )(src, xp)


def _combine_kernel(pos_ref, y_hbm, wts_ref, h_ref, gain_ref, out_ref, buf0, buf1, sems, *, tm):
    base = pl.program_id(0) * tm
    bufs = (buf0, buf1)

    def row_copy(r, k):
        return pltpu.make_async_copy(y_hbm.at[pl.ds(pos_ref[TOP_K * (base + r) + k], 1), :],
                                     bufs[k].at[pl.ds(r, 1), :], sems.at[k])

    def start(r, carry):
        for k in range(TOP_K):
            row_copy(r, k).start()
        return carry

    def wait(r, carry):
        for k in range(TOP_K):
            row_copy(r, k).wait()
        return carry

    lax.fori_loop(0, tm, start, 0)
    lax.fori_loop(0, tm, wait, 0)
    w = wts_ref[...]
    y = w[:, 0:1] * buf0[...] + w[:, 1:2] * buf1[...]
    out_ref[...] = h_ref[...] + _rms_norm_f32(y, gain_ref[...])


def _combine(y_rows, pos, wts, h, gain, *, tm=256):
    n, d = h.shape
    assert n % tm == 0
    limit = _vmem_limit(2 * _nbytes((tm, d), F32) + _nbytes((tm, 128), F32),
                        2 * _nbytes((tm, d), F32), 4 * _nbytes((tm, d), F32))
    return pl.pallas_call(
        functools.partial(_combine_kernel, tm=tm),
        grid_spec=pltpu.PrefetchScalarGridSpec(
            num_scalar_prefetch=1,
            grid=(n // tm,),
            in_specs=[pl.BlockSpec(memory_space=pl.ANY),
                      pl.BlockSpec((tm, TOP_K), lambda i, pos: (i, 0)),
                      pl.BlockSpec((tm, d), lambda i, pos: (i, 0)),
                      pl.BlockSpec((1, d), lambda i, pos: (0, 0))],
            out_specs=pl.BlockSpec((tm, d), lambda i, pos: (i, 0)),
            scratch_shapes=[pltpu.VMEM((tm, d), F32), pltpu.VMEM((tm, d), F32),
                            pltpu.SemaphoreType.DMA((TOP_K,))]),
        out_shape=jax.ShapeDtypeStruct((n, d), F32),
        compiler_params=pltpu.CompilerParams(dimension_semantics=("arbitrary",), vmem_limit_bytes=limit),
        name="moe_combine",
    )(pos, y_rows, wts, h, gain)


def _dispatch_plan(top_idx, n_exp, tm):
    n = top_idx.shape[1]
    n_assign = n * TOP_K
    n_rows = n_assign + n_exp * tm
    expert = top_idx.T.reshape(n_assign)
    onehot = (expert[:, None] == jnp.arange(n_exp, dtype=jnp.int32)[None, :]).astype(jnp.int32)
    rank = jnp.sum((jnp.cumsum(onehot, axis=0) - onehot) * onehot, axis=1)
    counts = jnp.sum(onehot, axis=0)
    padded = (counts + tm - 1) // tm * tm
    ends = jnp.cumsum(padded)
    starts = ends - padded
    pos = starts[expert] + rank
    src = jnp.zeros((n_rows,), jnp.int32).at[pos].set(jnp.arange(n_assign, dtype=jnp.int32) // TOP_K)
    n_tiles = ends[-1] // tm
    tile_start = jnp.arange(n_rows // tm, dtype=jnp.int32) * tm
    tile_expert = jnp.sum((tile_start[:, None] >= ends[None, :]).astype(jnp.int32), axis=1)
    tile_expert = jnp.where(tile_start < ends[-1], tile_expert, tile_expert[n_tiles - 1]).astype(jnp.int32)
    return pos.astype(jnp.int32), src, tile_expert, n_tiles.reshape(1).astype(jnp.int32)


def _moe(h, gain_in, gain_out, w_router, w_gate, w_up, w_down, *, tm=1024, tf=256):
    n_exp = w_router.shape[1]
    xp, top_idx, top_w = _router(h, gain_in, w_router.T)
    pos, src, tile_expert, n_tiles = _dispatch_plan(top_idx, n_exp, tm)
    rows = _dispatch(xp, src)
    y_rows = _ffn(rows, tile_expert, n_tiles, gain_in, w_gate, w_up, w_down, gain_out,
                  dense=False, tm=tm, tf=tf)
    return _combine(y_rows, pos, top_w.T, h, gain_out)


def _dense_ffn(h, gain_in, gain_out, w_gate, w_up, w_down, *, tm=1024, tf=256):
    n_tiles = h.shape[0] // tm
    return _ffn(h, jnp.zeros((n_tiles,), jnp.int32), jnp.full((1,), n_tiles, jnp.int32),
                gain_in, w_gate[None], w_up[None], w_down[None], gain_out, dense=True, tm=tm, tf=tf)


def kernel(x, w_in, rel_bias, w_pool, pool_scale, w_attn_out, w_pool_out, w_out, norm_gains,
           dense_w_gate, dense_w_up, dense_w_down, moe_w_router, moe_w_gate, moe_w_up, moe_w_down):
    batch, seq, d = x.shape
    depth, _, in_width = w_in.shape
    attn_width = N_ATTN_GROUPS * GROUP_WIDTH
    h = x.reshape(batch * seq, d)
    biases = [_band_bias(rel_bias[:, g * HEADS_PER_GROUP:(g + 1) * HEADS_PER_GROUP], window, dilation)
              for g, (window, dilation) in enumerate(DILATION_PATTERNS)]
    for layer in range(depth):
        gains = norm_gains[layer].reshape(4, 1, d)
        z = _in_proj(h, gains[0], w_in[layer])
        outs, lses = [], []
        for g, (_, dilation) in enumerate(DILATION_PATTERNS):
            o, lse = _group_attention(z, biases[g], batch=batch, seq=seq, group=g, dilation=dilation,
                                      in_width=in_width, attn_width=attn_width)
            outs.append(o)
            lses.append(lse)
        h = _merge(h, z, outs, lses, w_pool[layer].astype(BF16), pool_scale[layer].reshape(1, -1),
                   w_attn_out[layer].astype(BF16), w_pool_out[layer].astype(BF16), w_out[layer].astype(BF16),
                   gains[1], seq=seq, attn_width=attn_width)
        j = layer // 2
        if layer % 2 == 0:
            h = _dense_ffn(h, gains[2], gains[3], dense_w_gate[j], dense_w_up[j], dense_w_down[j])
        else:
            h = _moe(h, gains[2], gains[3], moe_w_router[j], moe_w_gate[j], moe_w_up[j], moe_w_down[j])
    return h.reshape(batch, seq, d)
```

```python
import functools
import math

import jax
import jax.numpy as jnp
from jax import lax
from jax.experimental import pallas as pl
from jax.experimental.pallas import tpu as pltpu

F32 = jnp.float32
BF16 = jnp.bfloat16

RMS_EPS = 1e-6
HEAD_DIM = 128
Q_BLOCK = 128
DILATION_PATTERNS = ((128, 1), (512, 4), (2048, 16))
N_ATTN_GROUPS = len(DILATION_PATTERNS)
HEADS_PER_GROUP = 4
GROUP_WIDTH = HEADS_PER_GROUP * HEAD_DIM
POOL_SIZES = (2, 4, 8, 16)
POOL_GROUP_WIDTH = 128
POOL_HALO = 16
N_BUCKETS = 32
MAX_DISTANCE = 2048
N_EXPERTS = 8
TOP_K = 2

V7X_VMEM_BYTES = 64 * 1024 * 1024


def _vmem_limit(pipelined_bytes, resident_bytes, temp_bytes):
    return min(2 * pipelined_bytes + resident_bytes + temp_bytes, V7X_VMEM_BYTES)


def _nbytes(shape, dtype):
    return math.prod(shape) * jnp.dtype(dtype).itemsize


def _rms_norm_f32(x, gain):
    ms = jnp.mean(x * x, axis=-1, keepdims=True)
    return x * lax.rsqrt(ms + RMS_EPS) * gain


def _inproj_kernel(h_ref, g_ref, w_ref, z_ref, u_scr):
    @pl.when(pl.program_id(1) == 0)
    def _():
        u_scr[...] = _rms_norm_f32(h_ref[...], g_ref[...]).astype(BF16)

    z_ref[...] = jnp.dot(u_scr[...], w_ref[...].astype(BF16), preferred_element_type=F32)


def _in_proj(h, gain, w_all, layer, *, tm=1024, tn=512):
    n, d = h.shape
    width = w_all.shape[2]
    assert n % tm == 0 and width % tn == 0
    limit = _vmem_limit(
        _nbytes((tm, d), F32) + _nbytes((d, tn), F32) + _nbytes((tm, tn), F32),
        _nbytes((tm, d), BF16),
        2 * _nbytes((tm, d), F32) + _nbytes((d, tn), BF16) + _nbytes((tm, tn), F32))
    return pl.pallas_call(
        _inproj_kernel,
        grid=(n // tm, width // tn),
        in_specs=[pl.BlockSpec((tm, d), lambda i, j: (i, 0)),
                  pl.BlockSpec((1, d), lambda i, j: (0, 0)),
                  pl.BlockSpec((None, d, tn), lambda i, j: (layer, 0, j))],
        out_specs=pl.BlockSpec((tm, tn), lambda i, j: (i, j)),
        out_shape=jax.ShapeDtypeStruct((n, width), F32),
        scratch_shapes=[pltpu.VMEM((tm, d), BF16)],
        compiler_params=pltpu.CompilerParams(dimension_semantics=("arbitrary", "arbitrary"),
                                             vmem_limit_bytes=limit),
        name="in_proj",
    )(h, gain, w_all)


def _attn_kernel(q_ref, k_ref, v_ref, kh_ref, vh_ref, bias_ref, o_ref, l_ref, *, dilation, scale):
    first_chunk = pl.program_id(1) == 0
    n_sub = q_ref.shape[1] // (dilation * Q_BLOCK)
    key_col = lax.broadcasted_iota(jnp.int32, (Q_BLOCK, 2 * Q_BLOCK), 1)
    bias = bias_ref[0]

    def stream_rows(start):
        if dilation == 1:
            return pl.ds(start, Q_BLOCK)
        return pl.ds(start, Q_BLOCK, stride=dilation)

    for r in range(dilation):
        k_prev = kh_ref[0, stream_rows(r), :].astype(BF16)
        v_prev = vh_ref[0, stream_rows(r), :].astype(BF16)
        for n in range(n_sub):
            rows = stream_rows(r + n * Q_BLOCK * dilation)
            q = q_ref[0, rows, :].astype(BF16)
            k_cur = k_ref[0, rows, :].astype(BF16)
            v_cur = v_ref[0, rows, :].astype(BF16)
            kc = jnp.concatenate([k_prev, k_cur], axis=0)
            vc = jnp.concatenate([v_prev, v_cur], axis=0)
            s = lax.dot_general(q, kc, (((1,), (1,)), ((), ())), preferred_element_type=F32)
            s = s * scale + bias
            if n == 0:
                s = jnp.where(jnp.logical_and(first_chunk, key_col < Q_BLOCK), -jnp.inf, s)
            m = jnp.max(s, axis=-1, keepdims=True)
            p = jnp.exp(s - m)
            den = jnp.sum(p, axis=-1, keepdims=True)
            o = jnp.dot(p.astype(BF16), vc, preferred_element_type=F32) / den
            o_ref[0, rows, :] = o
            l_ref[0, rows, :] = jnp.broadcast_to(m + jnp.log(den), (Q_BLOCK, HEAD_DIM))
            k_prev, v_prev = k_cur, v_cur


def _t5_causal_bucket(dist):
    max_exact = N_BUCKETS // 2
    df = jnp.maximum(dist, 1).astype(F32)
    large = max_exact + (jnp.log(df / max_exact) / math.log(MAX_DISTANCE / max_exact)
                         * (N_BUCKETS - max_exact)).astype(jnp.int32)
    large = jnp.minimum(large, N_BUCKETS - 1)
    return jnp.where(dist < max_exact, dist, large)


def _band_bias(bias_table, window, dilation):
    span = window // dilation
    qi = jnp.arange(Q_BLOCK)[:, None]
    kj = jnp.arange(2 * Q_BLOCK)[None, :]
    step = qi + Q_BLOCK - kj
    valid = (step >= 0) & (step <= span)
    bucket = _t5_causal_bucket(jnp.clip(step, 0, span) * dilation)
    onehot = (bucket[:, :, None] == jnp.arange(N_BUCKETS)[None, None, :]).astype(F32)
    bias = jnp.einsum("qkb,bh->hqk", onehot, bias_table.astype(F32), precision=lax.Precision.HIGHEST)
    return jnp.where(valid[None], bias, -jnp.inf)


def _group_attention(z, bias, *, batch, seq, group, dilation, attn_width, chunk=2048):
    hist = Q_BLOCK * dilation
    assert seq % chunk == 0 and chunk % hist == 0 and attn_width % HEAD_DIM == 0
    zv = z.reshape(batch, seq, z.shape[1])
    q_col = group * HEADS_PER_GROUP
    k_col = q_col + attn_width // HEAD_DIM
    v_col = k_col + attn_width // HEAD_DIM

    def cur(col):
        return pl.BlockSpec((1, chunk, HEAD_DIM), lambda b, c, hh: (b, c, col + hh))

    def prev(col):
        return pl.BlockSpec((1, hist, HEAD_DIM),
                            lambda b, c, hh: (b, jnp.maximum(c * (chunk // hist) - 1, 0), col + hh))

    out_spec = pl.BlockSpec((1, chunk, HEAD_DIM), lambda b, c, hh: (b, c, hh))
    out_sds = jax.ShapeDtypeStruct((batch, seq, GROUP_WIDTH), F32)
    blk = _nbytes((chunk, HEAD_DIM), F32)
    limit = _vmem_limit(5 * blk + 2 * _nbytes((hist, HEAD_DIM), F32) + _nbytes(bias.shape[1:], F32),
                        0, 8 * blk)
    o, lse = pl.pallas_call(
        functools.partial(_attn_kernel, dilation=dilation, scale=HEAD_DIM ** -0.5),
        grid=(batch, seq // chunk, HEADS_PER_GROUP),
        in_specs=[cur(q_col), cur(k_col), cur(v_col), prev(k_col), prev(v_col),
                  pl.BlockSpec((1,) + bias.shape[1:], lambda b, c, hh: (hh, 0, 0))],
        out_specs=[out_spec, out_spec],
        out_shape=[out_sds, out_sds],
        compiler_params=pltpu.CompilerParams(dimension_semantics=("arbitrary",) * 3, vmem_limit_bytes=limit),
        name=f"dilated_attn_d{dilation}",
    )(zv, zv, zv, zv, zv, bias)
    return o.reshape(batch * seq, GROUP_WIDTH), lse.reshape(batch * seq, GROUP_WIDTH)


def _merge_kernel(o0_ref, o1_ref, o2_ref, l0_ref, l1_ref, l2_ref, p_ref, pp_ref,
                  ga0_ref, ga1_ref, gb0_ref, gb1_ref, h_ref,
                  wpool_ref, pscale_ref, wao_ref, wpo_ref, wout_ref, gain_ref, out_ref,
                  *, tm, tiles_per_seq):
    tile_in_seq = pl.program_id(0) % tiles_per_seq

    la, lb, lc = l0_ref[...], l1_ref[...], l2_ref[...]
    m = jnp.maximum(jnp.maximum(la, lb), lc)
    ea, eb, ec = jnp.exp(la - m), jnp.exp(lb - m), jnp.exp(lc - m)
    o_attn = (ea * o0_ref[...] + eb * o1_ref[...] + ec * o2_ref[...]) / (ea + eb + ec)

    halo = jnp.where(tile_in_seq == 0, 0.0, pp_ref[...])
    xe = jnp.concatenate([halo, p_ref[...]], axis=0)
    t = tile_in_seq * tm + lax.broadcasted_iota(jnp.int32, (tm, 1), 0)
    pooled = []
    for g, size in enumerate(POOL_SIZES):
        a = xe[:, g * POOL_GROUP_WIDTH:(g + 1) * POOL_GROUP_WIDTH]
        s, shift = a, 1
        while shift < size:
            s = s + pltpu.roll(s, shift, axis=0)
            shift *= 2
        count = jnp.minimum(t + 1, size).astype(F32)
        y = (s[POOL_HALO:] / count - a[POOL_HALO:]).astype(BF16)
        pooled.append(jnp.dot(y, wpool_ref[g], preferred_element_type=F32))
    o_pool = jnp.concatenate(pooled, axis=1) * pscale_ref[...]

    attn_proj = jnp.dot(o_attn.astype(BF16), wao_ref[...], preferred_element_type=F32)
    pool_proj = jnp.dot(o_pool.astype(BF16), wpo_ref[...], preferred_element_type=F32)
    gate_a = jax.nn.sigmoid(jnp.concatenate([ga0_ref[...], ga1_ref[...]], axis=1))
    gate_b = jax.nn.sigmoid(jnp.concatenate([gb0_ref[...], gb1_ref[...]], axis=1))
    merged = gate_a * attn_proj + gate_b * pool_proj
    mix = jnp.dot(merged.astype(BF16), wout_ref[...], preferred_element_type=F32)
    out_ref[...] = h_ref[...] + _rms_norm_f32(mix, gain_ref[...])


def _merge(h, z, attn_outs, attn_lses, w_pool, pool_scale, w_attn_out, w_pool_out, w_out, gain,
           *, seq, attn_width, tm=256):
    n, d = h.shape
    pool_width = w_pool.shape[0] * POOL_GROUP_WIDTH
    assert pool_width == GROUP_WIDTH and len(attn_outs) == 3
    assert n % tm == 0 and seq % tm == 0 and tm % POOL_HALO == 0
    p_col = 3 * attn_width // GROUP_WIDTH
    gate_w = d // 2
    gate_col = (3 * attn_width + pool_width) // gate_w
    assert (3 * attn_width + pool_width) % gate_w == 0

    row_blk = lambda c: pl.BlockSpec((tm, GROUP_WIDTH), lambda i: (i, c))
    gate_blk = lambda c: pl.BlockSpec((tm, gate_w), lambda i: (i, gate_col + c))
    halo_blk = pl.BlockSpec((POOL_HALO, GROUP_WIDTH),
                            lambda i: (jnp.maximum(i * (tm // POOL_HALO) - 1, 0), p_col))
    full = lambda a: pl.BlockSpec(a.shape, lambda i: (0,) * a.ndim)
    weights = (w_pool, pool_scale, w_attn_out, w_pool_out, w_out, gain)

    act = 7 * _nbytes((tm, GROUP_WIDTH), F32) + 4 * _nbytes((tm, gate_w), F32) + 2 * _nbytes((tm, d), F32)
    wbytes = sum(_nbytes(a.shape, a.dtype) for a in weights)
    limit = _vmem_limit(act + wbytes, 0, 8 * _nbytes((tm, d), F32))
    return pl.pallas_call(
        functools.partial(_merge_kernel, tm=tm, tiles_per_seq=seq // tm),
        grid=(n // tm,),
        in_specs=[row_blk(0)] * 6 + [row_blk(p_col), halo_blk] + [gate_blk(c) for c in range(4)]
                 + [pl.BlockSpec((tm, d), lambda i: (i, 0))] + [full(a) for a in weights],
        out_specs=pl.BlockSpec((tm, d), lambda i: (i, 0)),
        out_shape=jax.ShapeDtypeStruct((n, d), F32),
        compiler_params=pltpu.CompilerParams(dimension_semantics=("arbitrary",), vmem_limit_bytes=limit),
        name="mixer_merge",
    )(*attn_outs, *attn_lses, z, z, z, z, z, z, h, *weights)


def _unpack_bf16_pair(words):
    lo = lax.bitcast_convert_type(words << 16, F32)
    hi = lax.bitcast_convert_type(words & jnp.uint32(0xFFFF0000), F32)
    return jnp.concatenate([lo, hi], axis=1).astype(BF16)


def _swiglu_accumulate(u_scr, wg_ref, wu_ref, wd_ref, acc_ref):
    u = u_scr[...]
    gate = jnp.dot(u, wg_ref[...].astype(BF16), preferred_element_type=F32)
    up = jnp.dot(u, wu_ref[...].astype(BF16), preferred_element_type=F32)
    hidden = (gate * jax.nn.sigmoid(gate) * up).astype(BF16)
    acc_ref[...] += jnp.dot(hidden, wd_ref[...].astype(BF16), preferred_element_type=F32)


def _ffn_vmem_limit(tm, d, tf, x_bytes, scratch_bytes):
    return _vmem_limit(
        x_bytes + 3 * _nbytes((d, tf), F32) + _nbytes((tm, d), F32),
        _nbytes((tm, d), BF16) + scratch_bytes,
        3 * _nbytes((d, tf), BF16) + 4 * _nbytes((tm, tf), F32) + 2 * _nbytes((tm, d), F32))


def _dense_ffn_kernel(h_ref, gin_ref, wg_ref, wu_ref, wd_ref, gout_ref, out_ref, u_scr):
    j = pl.program_id(1)

    @pl.when(j == 0)
    def _():
        u_scr[...] = _rms_norm_f32(h_ref[...], gin_ref[...]).astype(BF16)
        out_ref[...] = jnp.zeros_like(out_ref)

    _swiglu_accumulate(u_scr, wg_ref, wu_ref, wd_ref, out_ref)

    @pl.when(j == pl.num_programs(1) - 1)
    def _():
        out_ref[...] = h_ref[...] + _rms_norm_f32(out_ref[...], gout_ref[...])


def _dense_ffn(h, gain_in, gain_out, w_gate, w_up, w_down, *, tm=1024, tf=256):
    n, d = h.shape
    ff = w_gate.shape[1]
    assert n % tm == 0 and ff % tf == 0
    row_blk = pl.BlockSpec((tm, d), lambda i, j: (i, 0))
    gain_blk = pl.BlockSpec((1, d), lambda i, j: (0, 0))
    up_blk = pl.BlockSpec((d, tf), lambda i, j: (0, j))
    return pl.pallas_call(
        _dense_ffn_kernel,
        grid=(n // tm, ff // tf),
        in_specs=[row_blk, gain_blk, up_blk, up_blk, pl.BlockSpec((tf, d), lambda i, j: (j, 0)), gain_blk],
        out_specs=row_blk,
        out_shape=jax.ShapeDtypeStruct((n, d), F32),
        scratch_shapes=[pltpu.VMEM((tm, d), BF16)],
        compiler_params=pltpu.CompilerParams(
            dimension_semantics=("arbitrary", "arbitrary"),
            vmem_limit_bytes=_ffn_vmem_limit(tm, d, tf, _nbytes((tm, d), F32), 0)),
        name="swiglu_dense",
    )(h, gain_in, w_gate, w_up, w_down, gain_out)


def _expert_ffn_kernel(tile_expert_ref, n_tiles_ref, src_ref, x_hbm, wg_ref, wu_ref, wd_ref, out_ref,
                       u_scr, stage, sem):
    del tile_expert_ref
    i, j = pl.program_id(0), pl.program_id(1)
    tm = stage.shape[0]
    active = i < n_tiles_ref[0]

    @pl.when(j == 0)
    def _():
        out_ref[...] = jnp.zeros_like(out_ref)

    @pl.when(jnp.logical_and(j == 0, active))
    def _():
        def start(r, carry):
            pltpu.make_async_copy(x_hbm.at[pl.ds(src_ref[i * tm + r], 1), :],
                                  stage.at[pl.ds(r, 1), :], sem).start()
            return carry

        lax.fori_loop(0, tm, start, 0, unroll=8)
        pltpu.make_async_copy(x_hbm.at[pl.ds(0, tm), :], stage, sem).wait()
        u_scr[...] = _unpack_bf16_pair(stage[...])

    @pl.when(active)
    def _():
        _swiglu_accumulate(u_scr, wg_ref, wu_ref, wd_ref, out_ref)


def _expert_ffn(xp, src, tile_expert, n_tiles, w_gate, w_up, w_down, *, tm, tf):
    rows = src.shape[0]
    n_exp, d, ff = w_gate.shape
    assert rows % tm == 0 and ff % tf == 0 and xp.shape[1] == d // 2
    nj = ff // tf

    def ff_tile(i, j, nt):
        return jnp.where(i < nt[0], j, nj - 1)

    up_blk = pl.BlockSpec((None, d, tf), lambda i, j, te, nt, src: (te[i], 0, ff_tile(i, j, nt)))
    down_blk = pl.BlockSpec((None, tf, d), lambda i, j, te, nt, src: (te[i], ff_tile(i, j, nt), 0))
    return pl.pallas_call(
        _expert_ffn_kernel,
        grid_spec=pltpu.PrefetchScalarGridSpec(
            num_scalar_prefetch=3,
            grid=(rows // tm, nj),
            in_specs=[pl.BlockSpec(memory_space=pl.ANY), up_blk, up_blk, down_blk],
            out_specs=pl.BlockSpec((tm, d), lambda i, j, te, nt, src: (i, 0)),
            scratch_shapes=[pltpu.VMEM((tm, d), BF16), pltpu.VMEM((tm, d // 2), jnp.uint32),
                            pltpu.SemaphoreType.DMA]),
        out_shape=jax.ShapeDtypeStruct((rows, d), F32),
        compiler_params=pltpu.CompilerParams(
            dimension_semantics=("arbitrary", "arbitrary"),
            vmem_limit_bytes=_ffn_vmem_limit(tm, d, tf, 0, _nbytes((tm, d // 2), jnp.uint32))),
        name="swiglu_experts",
    )(tile_expert, n_tiles, src, xp, w_gate, w_up, w_down)


def _router_kernel(h_ref, gain_ref, wrt_ref, xp_ref, idx_ref, wts_ref):
    u = _rms_norm_f32(h_ref[...], gain_ref[...])
    logits = lax.dot_general(wrt_ref[...], u, (((1,), (1,)), ((), ())),
                             precision=lax.Precision.HIGHEST, preferred_element_type=F32)
    n_exp = logits.shape[0]
    expert = lax.broadcasted_iota(jnp.int32, logits.shape, 0)
    v1 = jnp.max(logits, axis=0, keepdims=True)
    i1 = jnp.min(jnp.where(logits == v1, expert, n_exp), axis=0, keepdims=True)
    rest = jnp.where(expert == i1, -jnp.inf, logits)
    v2 = jnp.max(rest, axis=0, keepdims=True)
    i2 = jnp.min(jnp.where(rest == v2, expert, n_exp), axis=0, keepdims=True)
    e2 = jnp.exp(v2 - v1)
    idx_ref[...] = jnp.concatenate([i1, i2], axis=0)
    wts_ref[...] = jnp.concatenate([1.0 / (1.0 + e2), e2 / (1.0 + e2)], axis=0)

    bits = lax.bitcast_convert_type(u.astype(BF16).astype(F32), jnp.uint32)
    half = bits.shape[1] // 2
    xp_ref[...] = (bits[:, :half] >> 16) | bits[:, half:]


def _router(h, gain, w_router_t, *, tm=512):
    n, d = h.shape
    n_exp = w_router_t.shape[0]
    assert n % tm == 0
    limit = _vmem_limit(_nbytes((tm, d), F32) + _nbytes((tm, d // 2), jnp.uint32) + _nbytes((n_exp, d), F32),
                        0, 6 * _nbytes((tm, d), F32))
    return pl.pallas_call(
        _router_kernel,
        grid=(n // tm,),
        in_specs=[pl.BlockSpec((tm, d), lambda i: (i, 0)),
                  pl.BlockSpec((1, d), lambda i: (0, 0)),
                  pl.BlockSpec((n_exp, d), lambda i: (0, 0))],
        out_specs=[pl.BlockSpec((tm, d // 2), lambda i: (i, 0)),
                   pl.BlockSpec((TOP_K, tm), lambda i: (0, i)),
                   pl.BlockSpec((TOP_K, tm), lambda i: (0, i))],
        out_shape=[jax.ShapeDtypeStruct((n, d // 2), jnp.uint32),
                   jax.ShapeDtypeStruct((TOP_K, n), jnp.int32),
                   jax.ShapeDtypeStruct((TOP_K, n), F32)],
        compiler_params=pltpu.CompilerParams(dimension_semantics=("arbitrary",), vmem_limit_bytes=limit),
        name="moe_router",
    )(h, gain, w_router_t)


def _combine_kernel(pos_ref, y_hbm, wts_ref, h_ref, gain_ref, out_ref, buf0, buf1, sems, *, tm):
    base = pl.program_id(0) * tm
    bufs = (buf0, buf1)

    def start(r, carry):
        for k in range(TOP_K):
            pltpu.make_async_copy(y_hbm.at[pl.ds(pos_ref[TOP_K * (base + r) + k], 1), :],
                                  bufs[k].at[pl.ds(r, 1), :], sems.at[k]).start()
        return carry

    lax.fori_loop(0, tm, start, 0, unroll=8)
    for k in range(TOP_K):
        pltpu.make_async_copy(y_hbm.at[pl.ds(0, tm), :], bufs[k], sems.at[k]).wait()
    w = wts_ref[...]
    y = w[:, 0:1] * buf0[...] + w[:, 1:2] * buf1[...]
    out_ref[...] = h_ref[...] + _rms_norm_f32(y, gain_ref[...])


def _combine(y_rows, pos, wts, h, gain, *, tm=256):
    n, d = h.shape
    assert n % tm == 0
    limit = _vmem_limit(2 * _nbytes((tm, d), F32) + _nbytes((tm, 128), F32),
                        2 * _nbytes((tm, d), F32), 4 * _nbytes((tm, d), F32))
    return pl.pallas_call(
        functools.partial(_combine_kernel, tm=tm),
        grid_spec=pltpu.PrefetchScalarGridSpec(
            num_scalar_prefetch=1,
            grid=(n // tm,),
            in_specs=[pl.BlockSpec(memory_space=pl.ANY),
                      pl.BlockSpec((tm, TOP_K), lambda i, pos: (i, 0)),
                      pl.BlockSpec((tm, d), lambda i, pos: (i, 0)),
                      pl.BlockSpec((1, d), lambda i, pos: (0, 0))],
            out_specs=pl.BlockSpec((tm, d), lambda i, pos: (i, 0)),
            scratch_shapes=[pltpu.VMEM((tm, d), F32), pltpu.VMEM((tm, d), F32),
                            pltpu.SemaphoreType.DMA((TOP_K,))]),
        out_shape=jax.ShapeDtypeStruct((n, d), F32),
        compiler_params=pltpu.CompilerParams(dimension_semantics=("arbitrary",), vmem_limit_bytes=limit),
        name="moe_combine",
    )(pos, y_rows, wts, h, gain)


def _dispatch_plan(top_idx, n_exp, tm):
    n = top_idx.shape[1]
    n_assign = n * TOP_K
    n_rows = n_assign + n_exp * tm
    expert = top_idx.T.reshape(n_assign)
    onehot = (expert[:, None] == jnp.arange(n_exp, dtype=jnp.int32)[None, :]).astype(jnp.int32)
    rank = jnp.sum((jnp.cumsum(onehot, axis=0) - onehot) * onehot, axis=1)
    counts = jnp.sum(onehot, axis=0)
    padded = (counts + tm - 1) // tm * tm
    ends = jnp.cumsum(padded)
    starts = ends - padded
    pos = starts[expert] + rank
    src = jnp.zeros((n_rows,), jnp.int32).at[pos].set(jnp.arange(n_assign, dtype=jnp.int32) // TOP_K)
    n_tiles = ends[-1] // tm
    tile_start = jnp.arange(n_rows // tm, dtype=jnp.int32) * tm
    tile_expert = jnp.sum((tile_start[:, None] >= ends[None, :]).astype(jnp.int32), axis=1)
    tile_expert = jnp.where(tile_start < ends[-1], tile_expert, tile_expert[n_tiles - 1]).astype(jnp.int32)
    return pos.astype(jnp.int32), src, tile_expert, n_tiles.reshape(1).astype(jnp.int32)


def _moe(h, gain_in, gain_out, w_router, w_gate, w_up, w_down, *, tm=1024, tf=256):
    n_exp = w_router.shape[1]
    xp, top_idx, top_w = _router(h, gain_in, w_router.T)
    pos, src, tile_expert, n_tiles = _dispatch_plan(top_idx, n_exp, tm)
    y_rows = _expert_ffn(xp, src, tile_expert, n_tiles, w_gate, w_up, w_down, tm=tm, tf=tf)
    return _combine(y_rows, pos, top_w.T, h, gain_out)


def kernel(x, w_in, rel_bias, w_pool, pool_scale, w_attn_out, w_pool_out, w_out, norm_gains,
           dense_w_gate, dense_w_up, dense_w_down, moe_w_router, moe_w_gate, moe_w_up, moe_w_down):
    batch, seq, d = x.shape
    depth = w_in.shape[0]
    attn_width = N_ATTN_GROUPS * GROUP_WIDTH
    h = x.reshape(batch * seq, d)
    biases = [_band_bias(rel_bias[:, g * HEADS_PER_GROUP:(g + 1) * HEADS_PER_GROUP], window, dilation)
              for g, (window, dilation) in enumerate(DILATION_PATTERNS)]
    for layer in range(depth):
        gains = norm_gains[layer].reshape(4, 1, d)
        z = _in_proj(h, gains[0], w_in, layer)
        outs, lses = [], []
        for g, (_, dilation) in enumerate(DILATION_PATTERNS):
            o, lse = _group_attention(z, biases[g], batch=batch, seq=seq, group=g, dilation=dilation,
                                      attn_width=attn_width)
            outs.append(o)
            lses.append(lse)
        h = _merge(h, z, outs, lses, w_pool[layer].astype(BF16), pool_scale[layer].reshape(1, -1),
                   w_attn_out[layer].astype(BF16), w_pool_out[layer].astype(BF16), w_out[layer].astype(BF16),
                   gains[1], seq=seq, attn_width=attn_width)
        j = layer // 2
        if layer % 2 == 0:
            h = _dense_ffn(h, gains[2], gains[3], dense_w_gate[j], dense_w_up[j], dense_w_down[j])
        else:
            h = _moe(h, gains[2], gains[3], moe_w_router[j], moe_w_gate[j], moe_w_up[j], moe_w_down[j])
    return h.reshape(batch, seq, d)
```

```python
import functools
import math

import jax
import jax.numpy as jnp
from jax import lax
from jax.experimental import pallas as pl
from jax.experimental.pallas import tpu as pltpu

F32 = jnp.float32
BF16 = jnp.bfloat16

RMS_EPS = 1e-6
HEAD_DIM = 128
Q_BLOCK = 128
DILATION_PATTERNS = ((128, 1), (512, 4), (2048, 16))
N_ATTN_GROUPS = len(DILATION_PATTERNS)
HEADS_PER_GROUP = 4
GROUP_WIDTH = HEADS_PER_GROUP * HEAD_DIM
POOL_SIZES = (2, 4, 8, 16)
POOL_GROUP_WIDTH = 128
POOL_HALO = 16
N_BUCKETS = 32
MAX_DISTANCE = 2048
N_EXPERTS = 8
TOP_K = 2

V7X_VMEM_BYTES = 64 * 1024 * 1024


def _vmem_limit(pipelined_bytes, resident_bytes, temp_bytes):
    return min(2 * pipelined_bytes + resident_bytes + temp_bytes, V7X_VMEM_BYTES)


def _nbytes(shape, dtype):
    return math.prod(shape) * jnp.dtype(dtype).itemsize


def _rms_norm_f32(x, gain):
    ms = jnp.mean(x * x, axis=-1, keepdims=True)
    return x * lax.rsqrt(ms + RMS_EPS) * gain


ROW_CHUNK = 256


def _for_row_chunks(n_rows, body):
    assert n_rows % ROW_CHUNK == 0

    def step(c, carry):
        body(pl.ds(pl.multiple_of(c * ROW_CHUNK, ROW_CHUNK), ROW_CHUNK))
        return carry

    lax.fori_loop(0, n_rows // ROW_CHUNK, step, 0)


def _inproj_kernel(h_ref, g_ref, w_ref, z_ref, u_scr):
    @pl.when(pl.program_id(1) == 0)
    def _():
        def norm_rows(rows):
            u_scr[rows, :] = _rms_norm_f32(h_ref[rows, :], g_ref[...]).astype(BF16)

        _for_row_chunks(u_scr.shape[0], norm_rows)

    z_ref[...] = jnp.dot(u_scr[...], w_ref[...].astype(BF16), preferred_element_type=F32)


def _in_proj(h, gain, w_all, layer, *, tm=2048, tn=512):
    n, d = h.shape
    width = w_all.shape[2]
    assert n % tm == 0 and width % tn == 0
    limit = _vmem_limit(
        _nbytes((d, tn), F32) + _nbytes((tm, tn), F32),
        _nbytes((tm, d), F32) + _nbytes((tm, d), BF16),
        _nbytes((tm, d), F32) + _nbytes((d, tn), BF16) + _nbytes((tm, tn), F32))
    return pl.pallas_call(
        _inproj_kernel,
        grid=(n // tm, width // tn),
        in_specs=[pl.BlockSpec((tm, d), lambda i, j: (i, 0), pipeline_mode=pl.Buffered(1)),
                  pl.BlockSpec((1, d), lambda i, j: (0, 0)),
                  pl.BlockSpec((None, d, tn), lambda i, j: (layer, 0, j))],
        out_specs=pl.BlockSpec((tm, tn), lambda i, j: (i, j)),
        out_shape=jax.ShapeDtypeStruct((n, width), F32),
        scratch_shapes=[pltpu.VMEM((tm, d), BF16)],
        compiler_params=pltpu.CompilerParams(dimension_semantics=("arbitrary", "arbitrary"),
                                             vmem_limit_bytes=limit),
        name="in_proj",
    )(h, gain, w_all)


def _attn_kernel(q_ref, k_ref, v_ref, kh_ref, vh_ref, bias_ref, o_ref, l_ref, *, dilation, scale):
    first_chunk = pl.program_id(1) == 0
    n_sub = q_ref.shape[1] // (dilation * Q_BLOCK)
    key_col = lax.broadcasted_iota(jnp.int32, (Q_BLOCK, 2 * Q_BLOCK), 1)
    bias = bias_ref[0]

    def stream_rows(start):
        if dilation == 1:
            return pl.ds(start, Q_BLOCK)
        return pl.ds(start, Q_BLOCK, stride=dilation)

    for r in range(dilation):
        k_prev = kh_ref[0, stream_rows(r), :].astype(BF16)
        v_prev = vh_ref[0, stream_rows(r), :].astype(BF16)
        for n in range(n_sub):
            rows = stream_rows(r + n * Q_BLOCK * dilation)
            q = q_ref[0, rows, :].astype(BF16)
            k_cur = k_ref[0, rows, :].astype(BF16)
            v_cur = v_ref[0, rows, :].astype(BF16)
            kc = jnp.concatenate([k_prev, k_cur], axis=0)
            vc = jnp.concatenate([v_prev, v_cur], axis=0)
            s = lax.dot_general(q, kc, (((1,), (1,)), ((), ())), preferred_element_type=F32)
            s = s * scale + bias
            if n == 0:
                s = jnp.where(jnp.logical_and(first_chunk, key_col < Q_BLOCK), -jnp.inf, s)
            m = jnp.max(s, axis=-1, keepdims=True)
            p = jnp.exp(s - m)
            den = jnp.sum(p, axis=-1, keepdims=True)
            o = jnp.dot(p.astype(BF16), vc, preferred_element_type=F32) / den
            o_ref[0, rows, :] = o
            l_ref[0, rows, :] = jnp.broadcast_to(m + jnp.log(den), (Q_BLOCK, HEAD_DIM))
            k_prev, v_prev = k_cur, v_cur


def _t5_causal_bucket(dist):
    max_exact = N_BUCKETS // 2
    df = jnp.maximum(dist, 1).astype(F32)
    large = max_exact + (jnp.log(df / max_exact) / math.log(MAX_DISTANCE / max_exact)
                         * (N_BUCKETS - max_exact)).astype(jnp.int32)
    large = jnp.minimum(large, N_BUCKETS - 1)
    return jnp.where(dist < max_exact, dist, large)


def _band_bias(bias_table, window, dilation):
    span = window // dilation
    qi = jnp.arange(Q_BLOCK)[:, None]
    kj = jnp.arange(2 * Q_BLOCK)[None, :]
    step = qi + Q_BLOCK - kj
    valid = (step >= 0) & (step <= span)
    bucket = _t5_causal_bucket(jnp.clip(step, 0, span) * dilation)
    onehot = (bucket[:, :, None] == jnp.arange(N_BUCKETS)[None, None, :]).astype(F32)
    bias = jnp.einsum("qkb,bh->hqk", onehot, bias_table.astype(F32), precision=lax.Precision.HIGHEST)
    return jnp.where(valid[None], bias, -jnp.inf)


def _group_attention(z, bias, *, batch, seq, group, dilation, attn_width, chunk=2048):
    hist = Q_BLOCK * dilation
    assert seq % chunk == 0 and chunk % hist == 0 and attn_width % HEAD_DIM == 0
    zv = z.reshape(batch, seq, z.shape[1])
    q_col = group * HEADS_PER_GROUP
    k_col = q_col + attn_width // HEAD_DIM
    v_col = k_col + attn_width // HEAD_DIM

    def cur(col):
        return pl.BlockSpec((1, chunk, HEAD_DIM), lambda b, c, hh: (b, c, col + hh))

    def prev(col):
        return pl.BlockSpec((1, hist, HEAD_DIM),
                            lambda b, c, hh: (b, jnp.maximum(c * (chunk // hist) - 1, 0), col + hh))

    out_spec = pl.BlockSpec((1, chunk, HEAD_DIM), lambda b, c, hh: (b, c, hh))
    out_sds = jax.ShapeDtypeStruct((batch, seq, GROUP_WIDTH), F32)
    blk = _nbytes((chunk, HEAD_DIM), F32)
    limit = _vmem_limit(5 * blk + 2 * _nbytes((hist, HEAD_DIM), F32) + _nbytes(bias.shape[1:], F32),
                        0, 8 * blk)
    o, lse = pl.pallas_call(
        functools.partial(_attn_kernel, dilation=dilation, scale=HEAD_DIM ** -0.5),
        grid=(batch, seq // chunk, HEADS_PER_GROUP),
        in_specs=[cur(q_col), cur(k_col), cur(v_col), prev(k_col), prev(v_col),
                  pl.BlockSpec((1,) + bias.shape[1:], lambda b, c, hh: (hh, 0, 0))],
        out_specs=[out_spec, out_spec],
        out_shape=[out_sds, out_sds],
        compiler_params=pltpu.CompilerParams(dimension_semantics=("arbitrary",) * 3, vmem_limit_bytes=limit),
        name=f"dilated_attn_d{dilation}",
    )(zv, zv, zv, zv, zv, bias)
    return o.reshape(batch * seq, GROUP_WIDTH), lse.reshape(batch * seq, GROUP_WIDTH)


def _merge_kernel(o0_ref, o1_ref, o2_ref, l0_ref, l1_ref, l2_ref, p_ref, pp_ref,
                  ga0_ref, ga1_ref, gb0_ref, gb1_ref, h_ref,
                  wpool_ref, pscale_ref, wao_ref, wpo_ref, wout_ref, gain_ref, out_ref,
                  *, tm, tiles_per_seq):
    tile_in_seq = pl.program_id(0) % tiles_per_seq

    la, lb, lc = l0_ref[...], l1_ref[...], l2_ref[...]
    m = jnp.maximum(jnp.maximum(la, lb), lc)
    ea, eb, ec = jnp.exp(la - m), jnp.exp(lb - m), jnp.exp(lc - m)
    o_attn = (ea * o0_ref[...] + eb * o1_ref[...] + ec * o2_ref[...]) / (ea + eb + ec)

    halo = jnp.where(tile_in_seq == 0, 0.0, pp_ref[...])
    xe = jnp.concatenate([halo, p_ref[...]], axis=0)
    t = tile_in_seq * tm + lax.broadcasted_iota(jnp.int32, (tm, 1), 0)
    pooled = []
    for g, size in enumerate(POOL_SIZES):
        a = xe[:, g * POOL_GROUP_WIDTH:(g + 1) * POOL_GROUP_WIDTH]
        s, shift = a, 1
        while shift < size:
            s = s + pltpu.roll(s, shift, axis=0)
            shift *= 2
        count = jnp.minimum(t + 1, size).astype(F32)
        y = (s[POOL_HALO:] / count - a[POOL_HALO:]).astype(BF16)
        pooled.append(jnp.dot(y, wpool_ref[g], preferred_element_type=F32))
    o_pool = jnp.concatenate(pooled, axis=1) * pscale_ref[...]

    attn_proj = jnp.dot(o_attn.astype(BF16), wao_ref[...], preferred_element_type=F32)
    pool_proj = jnp.dot(o_pool.astype(BF16), wpo_ref[...], preferred_element_type=F32)
    gate_a = jax.nn.sigmoid(jnp.concatenate([ga0_ref[...], ga1_ref[...]], axis=1))
    gate_b = jax.nn.sigmoid(jnp.concatenate([gb0_ref[...], gb1_ref[...]], axis=1))
    merged = gate_a * attn_proj + gate_b * pool_proj
    mix = jnp.dot(merged.astype(BF16), wout_ref[...], preferred_element_type=F32)
    out_ref[...] = h_ref[...] + _rms_norm_f32(mix, gain_ref[...])


def _merge(h, z, attn_outs, attn_lses, w_pool, pool_scale, w_attn_out, w_pool_out, w_out, gain,
           *, seq, attn_width, tm=256):
    n, d = h.shape
    pool_width = w_pool.shape[0] * POOL_GROUP_WIDTH
    assert pool_width == GROUP_WIDTH and len(attn_outs) == 3
    assert n % tm == 0 and seq % tm == 0 and tm % POOL_HALO == 0
    p_col = 3 * attn_width // GROUP_WIDTH
    gate_w = d // 2
    gate_col = (3 * attn_width + pool_width) // gate_w
    assert (3 * attn_width + pool_width) % gate_w == 0

    row_blk = lambda c: pl.BlockSpec((tm, GROUP_WIDTH), lambda i: (i, c))
    gate_blk = lambda c: pl.BlockSpec((tm, gate_w), lambda i: (i, gate_col + c))
    halo_blk = pl.BlockSpec((POOL_HALO, GROUP_WIDTH),
                            lambda i: (jnp.maximum(i * (tm // POOL_HALO) - 1, 0), p_col))
    full = lambda a: pl.BlockSpec(a.shape, lambda i: (0,) * a.ndim)
    weights = (w_pool, pool_scale, w_attn_out, w_pool_out, w_out, gain)

    act = 7 * _nbytes((tm, GROUP_WIDTH), F32) + 4 * _nbytes((tm, gate_w), F32) + 2 * _nbytes((tm, d), F32)
    wbytes = sum(_nbytes(a.shape, a.dtype) for a in weights)
    limit = _vmem_limit(act + wbytes, 0, 8 * _nbytes((tm, d), F32))
    return pl.pallas_call(
        functools.partial(_merge_kernel, tm=tm, tiles_per_seq=seq // tm),
        grid=(n // tm,),
        in_specs=[row_blk(0)] * 6 + [row_blk(p_col), halo_blk] + [gate_blk(c) for c in range(4)]
                 + [pl.BlockSpec((tm, d), lambda i: (i, 0))] + [full(a) for a in weights],
        out_specs=pl.BlockSpec((tm, d), lambda i: (i, 0)),
        out_shape=jax.ShapeDtypeStruct((n, d), F32),
        compiler_params=pltpu.CompilerParams(dimension_semantics=("arbitrary",), vmem_limit_bytes=limit),
        name="mixer_merge",
    )(*attn_outs, *attn_lses, z, z, z, z, z, z, h, *weights)


def _unpack_bf16_pair(words):
    lo = lax.bitcast_convert_type(words << 16, F32)
    hi = lax.bitcast_convert_type(words & jnp.uint32(0xFFFF0000), F32)
    return jnp.concatenate([lo, hi], axis=1).astype(BF16)


def _swiglu_accumulate(u_scr, wg_ref, wu_ref, wd_ref, acc_ref, rows=slice(None)):
    u = u_scr[rows, :]
    gate = jnp.dot(u, wg_ref[...].astype(BF16), preferred_element_type=F32)
    up = jnp.dot(u, wu_ref[...].astype(BF16), preferred_element_type=F32)
    hidden = (gate * jax.nn.sigmoid(gate) * up).astype(BF16)
    acc_ref[rows, :] += jnp.dot(hidden, wd_ref[...].astype(BF16), preferred_element_type=F32)


def _ffn_vmem_limit(tm, d, tf, x_bytes, scratch_bytes):
    return _vmem_limit(
        x_bytes + 3 * _nbytes((d, tf), F32) + _nbytes((tm, d), F32),
        _nbytes((tm, d), BF16) + scratch_bytes,
        3 * _nbytes((d, tf), BF16) + 4 * _nbytes((tm, tf), F32) + 2 * _nbytes((tm, d), F32))


def _dense_ffn_kernel(h_ref, gin_ref, wg_ref, wu_ref, wd_ref, gout_ref, out_ref, u_scr):
    j = pl.program_id(1)

    @pl.when(j == 0)
    def _():
        def norm_rows(rows):
            u_scr[rows, :] = _rms_norm_f32(h_ref[rows, :], gin_ref[...]).astype(BF16)
            out_ref[rows, :] = jnp.zeros((ROW_CHUNK, out_ref.shape[1]), F32)

        _for_row_chunks(u_scr.shape[0], norm_rows)

    _swiglu_accumulate(u_scr, wg_ref, wu_ref, wd_ref, out_ref)

    @pl.when(j == pl.num_programs(1) - 1)
    def _():
        def residual_rows(rows):
            out_ref[rows, :] = h_ref[rows, :] + _rms_norm_f32(out_ref[rows, :], gout_ref[...])

        _for_row_chunks(u_scr.shape[0], residual_rows)


def _dense_ffn(h, gain_in, gain_out, w_gate, w_up, w_down, *, tm=1024, tf=256):
    n, d = h.shape
    ff = w_gate.shape[1]
    assert n % tm == 0 and ff % tf == 0
    row_blk = pl.BlockSpec((tm, d), lambda i, j: (i, 0))
    gain_blk = pl.BlockSpec((1, d), lambda i, j: (0, 0))
    up_blk = pl.BlockSpec((d, tf), lambda i, j: (0, j))
    return pl.pallas_call(
        _dense_ffn_kernel,
        grid=(n // tm, ff // tf),
        in_specs=[row_blk, gain_blk, up_blk, up_blk, pl.BlockSpec((tf, d), lambda i, j: (j, 0)), gain_blk],
        out_specs=row_blk,
        out_shape=jax.ShapeDtypeStruct((n, d), F32),
        scratch_shapes=[pltpu.VMEM((tm, d), BF16)],
        compiler_params=pltpu.CompilerParams(
            dimension_semantics=("arbitrary", "arbitrary"),
            vmem_limit_bytes=_ffn_vmem_limit(tm, d, tf, _nbytes((tm, d), F32), 0)),
        name="swiglu_dense",
    )(h, gain_in, w_gate, w_up, w_down, gain_out)


def _expert_ffn_kernel(tile_expert_ref, tile_rows_ref, src_ref, x_hbm, wg_ref, wu_ref, wd_ref, out_ref,
                       u_scr, stage, sem):
    del tile_expert_ref
    i, j = pl.program_id(0), pl.program_id(1)
    tm = stage.shape[0]
    n_rows = tile_rows_ref[i]
    n_chunks = n_rows // ROW_CHUNK

    @pl.when(j == 0)
    def _():
        def zero_rows(rows):
            out_ref[rows, :] = jnp.zeros((ROW_CHUNK, out_ref.shape[1]), F32)

        _for_row_chunks(tm, zero_rows)

        def gather_chunk(c, carry):
            base = c * ROW_CHUNK

            def start(r, inner):
                pltpu.make_async_copy(x_hbm.at[pl.ds(src_ref[i * tm + base + r], 1), :],
                                      stage.at[pl.ds(base + r, 1), :], sem).start()
                return inner

            return lax.fori_loop(0, ROW_CHUNK, start, carry, unroll=8)

        lax.fori_loop(0, n_chunks, gather_chunk, 0)

        def wait_chunk(c, carry):
            pltpu.make_async_copy(x_hbm.at[pl.ds(0, ROW_CHUNK), :], stage.at[pl.ds(0, ROW_CHUNK), :], sem).wait()
            return carry

        lax.fori_loop(0, n_chunks, wait_chunk, 0)

        def unpack_chunk(c, carry):
            rows = pl.ds(pl.multiple_of(c * ROW_CHUNK, ROW_CHUNK), ROW_CHUNK)
            u_scr[rows, :] = _unpack_bf16_pair(stage[rows, :])
            return carry

        lax.fori_loop(0, n_chunks, unpack_chunk, 0)

    @pl.when(n_rows == tm)
    def _():
        _swiglu_accumulate(u_scr, wg_ref, wu_ref, wd_ref, out_ref)

    @pl.when(jnp.logical_and(n_rows > 0, n_rows < tm))
    def _():
        def compute_chunk(c, carry):
            rows = pl.ds(pl.multiple_of(c * ROW_CHUNK, ROW_CHUNK), ROW_CHUNK)
            _swiglu_accumulate(u_scr, wg_ref, wu_ref, wd_ref, out_ref, rows)
            return carry

        lax.fori_loop(0, n_chunks, compute_chunk, 0)


def _expert_ffn(xp, src, tile_expert, tile_rows, w_gate, w_up, w_down, *, tm, tf):
    rows = src.shape[0]
    n_exp, d, ff = w_gate.shape
    assert rows % tm == 0 and tm % ROW_CHUNK == 0 and ff % tf == 0 and xp.shape[1] == d // 2
    nj = ff // tf

    def ff_tile(i, j, tr):
        return jnp.where(tr[i] > 0, j, nj - 1)

    up_blk = pl.BlockSpec((None, d, tf), lambda i, j, te, tr, src: (te[i], 0, ff_tile(i, j, tr)))
    down_blk = pl.BlockSpec((None, tf, d), lambda i, j, te, tr, src: (te[i], ff_tile(i, j, tr), 0))
    return pl.pallas_call(
        _expert_ffn_kernel,
        grid_spec=pltpu.PrefetchScalarGridSpec(
            num_scalar_prefetch=3,
            grid=(rows // tm, nj),
            in_specs=[pl.BlockSpec(memory_space=pl.ANY), up_blk, up_blk, down_blk],
            out_specs=pl.BlockSpec((tm, d), lambda i, j, te, tr, src: (i, 0)),
            scratch_shapes=[pltpu.VMEM((tm, d), BF16), pltpu.VMEM((tm, d // 2), jnp.uint32),
                            pltpu.SemaphoreType.DMA]),
        out_shape=jax.ShapeDtypeStruct((rows, d), F32),
        compiler_params=pltpu.CompilerParams(
            dimension_semantics=("arbitrary", "arbitrary"),
            vmem_limit_bytes=_ffn_vmem_limit(tm, d, tf, 0, _nbytes((tm, d // 2), jnp.uint32))),
        name="swiglu_experts",
    )(tile_expert, tile_rows, src, xp, w_gate, w_up, w_down)


def _router_kernel(h_ref, gain_ref, wrt_ref, xp_ref, idx_ref, wts_ref):
    u = _rms_norm_f32(h_ref[...], gain_ref[...])
    logits = lax.dot_general(wrt_ref[...], u, (((1,), (1,)), ((), ())),
                             precision=lax.Precision.HIGHEST, preferred_element_type=F32)
    n_exp = logits.shape[0]
    expert = lax.broadcasted_iota(jnp.int32, logits.shape, 0)
    v1 = jnp.max(logits, axis=0, keepdims=True)
    i1 = jnp.min(jnp.where(logits == v1, expert, n_exp), axis=0, keepdims=True)
    rest = jnp.where(expert == i1, -jnp.inf, logits)
    v2 = jnp.max(rest, axis=0, keepdims=True)
    i2 = jnp.min(jnp.where(rest == v2, expert, n_exp), axis=0, keepdims=True)
    e2 = jnp.exp(v2 - v1)
    idx_ref[...] = jnp.concatenate([i1, i2], axis=0)
    wts_ref[...] = jnp.concatenate([1.0 / (1.0 + e2), e2 / (1.0 + e2)], axis=0)

    bits = lax.bitcast_convert_type(u.astype(BF16).astype(F32), jnp.uint32)
    half = bits.shape[1] // 2
    xp_ref[...] = (bits[:, :half] >> 16) | bits[:, half:]


def _router(h, gain, w_router_t, *, tm=512):
    n, d = h.shape
    n_exp = w_router_t.shape[0]
    assert n % tm == 0
    limit = _vmem_limit(_nbytes((tm, d), F32) + _nbytes((tm, d // 2), jnp.uint32) + _nbytes((n_exp, d), F32),
                        0, 6 * _nbytes((tm, d), F32))
    return pl.pallas_call(
        _router_kernel,
        grid=(n // tm,),
        in_specs=[pl.BlockSpec((tm, d), lambda i: (i, 0)),
                  pl.BlockSpec((1, d), lambda i: (0, 0)),
                  pl.BlockSpec((n_exp, d), lambda i: (0, 0))],
        out_specs=[pl.BlockSpec((tm, d // 2), lambda i: (i, 0)),
                   pl.BlockSpec((TOP_K, tm), lambda i: (0, i)),
                   pl.BlockSpec((TOP_K, tm), lambda i: (0, i))],
        out_shape=[jax.ShapeDtypeStruct((n, d // 2), jnp.uint32),
                   jax.ShapeDtypeStruct((TOP_K, n), jnp.int32),
                   jax.ShapeDtypeStruct((TOP_K, n), F32)],
        compiler_params=pltpu.CompilerParams(dimension_semantics=("arbitrary",), vmem_limit_bytes=limit),
        name="moe_router",
    )(h, gain, w_router_t)


def _combine_kernel(pos_ref, y_hbm, wts_ref, h_ref, gain_ref, out_ref, buf0, buf1, sems, *, tm):
    base = pl.program_id(0) * tm
    bufs = (buf0, buf1)

    def start(r, carry):
        for k in range(TOP_K):
            pltpu.make_async_copy(y_hbm.at[pl.ds(pos_ref[TOP_K * (base + r) + k], 1), :],
                                  bufs[k].at[pl.ds(r, 1), :], sems.at[k]).start()
        return carry

    lax.fori_loop(0, tm, start, 0, unroll=8)
    for k in range(TOP_K):
        pltpu.make_async_copy(y_hbm.at[pl.ds(0, tm), :], bufs[k], sems.at[k]).wait()
    w = wts_ref[...]
    y = w[:, 0:1] * buf0[...] + w[:, 1:2] * buf1[...]
    out_ref[...] = h_ref[...] + _rms_norm_f32(y, gain_ref[...])


def _combine(y_rows, pos, wts, h, gain, *, tm=256):
    n, d = h.shape
    assert n % tm == 0
    limit = _vmem_limit(2 * _nbytes((tm, d), F32) + _nbytes((tm, 128), F32),
                        2 * _nbytes((tm, d), F32), 4 * _nbytes((tm, d), F32))
    return pl.pallas_call(
        functools.partial(_combine_kernel, tm=tm),
        grid_spec=pltpu.PrefetchScalarGridSpec(
            num_scalar_prefetch=1,
            grid=(n // tm,),
            in_specs=[pl.BlockSpec(memory_space=pl.ANY),
                      pl.BlockSpec((tm, TOP_K), lambda i, pos: (i, 0)),
                      pl.BlockSpec((tm, d), lambda i, pos: (i, 0)),
                      pl.BlockSpec((1, d), lambda i, pos: (0, 0))],
            out_specs=pl.BlockSpec((tm, d), lambda i, pos: (i, 0)),
            scratch_shapes=[pltpu.VMEM((tm, d), F32), pltpu.VMEM((tm, d), F32),
                            pltpu.SemaphoreType.DMA((TOP_K,))]),
        out_shape=jax.ShapeDtypeStruct((n, d), F32),
        compiler_params=pltpu.CompilerParams(dimension_semantics=("arbitrary",), vmem_limit_bytes=limit),
        name="moe_combine",
    )(pos, y_rows, wts, h, gain)


def _dispatch_plan(top_idx, n_exp, tm):
    n = top_idx.shape[1]
    n_assign = n * TOP_K
    n_tiles = n_assign // tm + n_exp
    expert = top_idx.T.reshape(n_assign)
    onehot = (expert[:, None] == jnp.arange(n_exp, dtype=jnp.int32)[None, :]).astype(jnp.int32)
    rank = jnp.sum((jnp.cumsum(onehot, axis=0) - onehot) * onehot, axis=1)
    counts = jnp.sum(onehot, axis=0)
    tiles_per_expert = (counts + tm - 1) // tm
    tile_ends = jnp.cumsum(tiles_per_expert)
    tile_starts = tile_ends - tiles_per_expert
    pos = tile_starts[expert] * tm + rank
    src = jnp.zeros((n_tiles * tm,), jnp.int32).at[pos].set(jnp.arange(n_assign, dtype=jnp.int32) // TOP_K)
    tile = jnp.arange(n_tiles, dtype=jnp.int32)
    tile_expert = jnp.sum((tile[:, None] >= tile_ends[None, :]).astype(jnp.int32), axis=1)
    used = tile < tile_ends[-1]
    tile_expert = jnp.where(used, tile_expert, tile_expert[tile_ends[-1] - 1])
    rows_left = (counts[tile_expert] + ROW_CHUNK - 1) // ROW_CHUNK * ROW_CHUNK - (tile - tile_starts[tile_expert]) * tm
    tile_rows = jnp.where(used, jnp.clip(rows_left, 0, tm), 0)
    return pos.astype(jnp.int32), src, tile_expert.astype(jnp.int32), tile_rows.astype(jnp.int32)


def _moe(h, gain_in, gain_out, w_router, w_gate, w_up, w_down, *, tm=1280, tf=256):
    n_exp = w_router.shape[1]
    xp, top_idx, top_w = _router(h, gain_in, w_router.T)
    pos, src, tile_expert, tile_rows = _dispatch_plan(top_idx, n_exp, tm)
    y_rows = _expert_ffn(xp, src, tile_expert, tile_rows, w_gate, w_up, w_down, tm=tm, tf=tf)
    return _combine(y_rows, pos, top_w.T, h, gain_out)


def kernel(x, w_in, rel_bias, w_pool, pool_scale, w_attn_out, w_pool_out, w_out, norm_gains,
           dense_w_gate, dense_w_up, dense_w_down, moe_w_router, moe_w_gate, moe_w_up, moe_w_down):
    batch, seq, d = x.shape
    depth = w_in.shape[0]
    attn_width = N_ATTN_GROUPS * GROUP_WIDTH
    h = x.reshape(batch * seq, d)
    biases = [_band_bias(rel_bias[:, g * HEADS_PER_GROUP:(g + 1) * HEADS_PER_GROUP], window, dilation)
              for g, (window, dilation) in enumerate(DILATION_PATTERNS)]
    for layer in range(depth):
        gains = norm_gains[layer].reshape(4, 1, d)
        z = _in_proj(h, gains[0], w_in, layer)
        outs, lses = [], []
        for g, (_, dilation) in enumerate(DILATION_PATTERNS):
            o, lse = _group_attention(z, biases[g], batch=batch, seq=seq, group=g, dilation=dilation,
                                      attn_width=attn_width)
            outs.append(o)
            lses.append(lse)
        h = _merge(h, z, outs, lses, w_pool[layer].astype(BF16), pool_scale[layer].reshape(1, -1),
                   w_attn_out[layer].astype(BF16), w_pool_out[layer].astype(BF16), w_out[layer].astype(BF16),
                   gains[1], seq=seq, attn_width=attn_width)
        j = layer // 2
        if layer % 2 == 0:
            h = _dense_ffn(h, gains[2], gains[3], dense_w_gate[j], dense_w_up[j], dense_w_down[j])
        else:
            h = _moe(h, gains[2], gains[3], moe_w_router[j], moe_w_gate[j], moe_w_up[j], moe_w_down[j])
    return h.reshape(batch, seq, d)
```

```python
import functools
import math

import jax
import jax.numpy as jnp
from jax import lax
from jax.experimental import pallas as pl
from jax.experimental.pallas import tpu as pltpu

F32 = jnp.float32
BF16 = jnp.bfloat16

RMS_EPS = 1e-6
HEAD_DIM = 128
Q_BLOCK = 128
DILATION_PATTERNS = ((128, 1), (512, 4), (2048, 16))
N_ATTN_GROUPS = len(DILATION_PATTERNS)
HEADS_PER_GROUP = 4
GROUP_WIDTH = HEADS_PER_GROUP * HEAD_DIM
POOL_SIZES = (2, 4, 8, 16)
POOL_GROUP_WIDTH = 128
POOL_HALO = 16
N_BUCKETS = 32
MAX_DISTANCE = 2048
N_EXPERTS = 8
TOP_K = 2

V7X_VMEM_BYTES = 64 * 1024 * 1024


def _vmem_limit(pipelined_bytes, resident_bytes, temp_bytes):
    return min(2 * pipelined_bytes + resident_bytes + temp_bytes, V7X_VMEM_BYTES)


def _nbytes(shape, dtype):
    return math.prod(shape) * jnp.dtype(dtype).itemsize


def _rms_norm_f32(x, gain):
    ms = jnp.mean(x * x, axis=-1, keepdims=True)
    return x * lax.rsqrt(ms + RMS_EPS) * gain


ROW_CHUNK = 256


def _for_row_chunks(n_rows, body):
    assert n_rows % ROW_CHUNK == 0

    def step(c, carry):
        body(pl.ds(pl.multiple_of(c * ROW_CHUNK, ROW_CHUNK), ROW_CHUNK))
        return carry

    lax.fori_loop(0, n_rows // ROW_CHUNK, step, 0)


def _inproj_kernel(h_ref, g_ref, w_ref, z_ref, u_scr):
    @pl.when(pl.program_id(1) == 0)
    def _():
        def norm_rows(rows):
            u_scr[rows, :] = _rms_norm_f32(h_ref[rows, :], g_ref[...]).astype(BF16)

        _for_row_chunks(u_scr.shape[0], norm_rows)

    z_ref[...] = jnp.dot(u_scr[...], w_ref[...].astype(BF16), preferred_element_type=F32)


def _in_proj(h, gain, w_all, layer, *, tm=2048, tn=512):
    n, d = h.shape
    width = w_all.shape[2]
    assert n % tm == 0 and width % tn == 0
    limit = _vmem_limit(
        _nbytes((d, tn), F32) + _nbytes((tm, tn), F32),
        _nbytes((tm, d), F32) + _nbytes((tm, d), BF16),
        _nbytes((tm, d), F32) + _nbytes((d, tn), BF16) + _nbytes((tm, tn), F32))
    return pl.pallas_call(
        _inproj_kernel,
        grid=(n // tm, width // tn),
        in_specs=[pl.BlockSpec((tm, d), lambda i, j: (i, 0), pipeline_mode=pl.Buffered(1)),
                  pl.BlockSpec((1, d), lambda i, j: (0, 0)),
                  pl.BlockSpec((None, d, tn), lambda i, j: (layer, 0, j))],
        out_specs=pl.BlockSpec((tm, tn), lambda i, j: (i, j)),
        out_shape=jax.ShapeDtypeStruct((n, width), F32),
        scratch_shapes=[pltpu.VMEM((tm, d), BF16)],
        compiler_params=pltpu.CompilerParams(dimension_semantics=("arbitrary", "arbitrary"),
                                             vmem_limit_bytes=limit),
        name="in_proj",
    )(h, gain, w_all)


def _group_attention_into(q_ref, k_ref, v_ref, kh_ref, vh_ref, bias_ref, o_scr, l_scr, *, dilation, scale):
    first_chunk = pl.program_id(1) == 0
    n_sub = q_ref.shape[1] // (dilation * Q_BLOCK)
    key_col = lax.broadcasted_iota(jnp.int32, (Q_BLOCK, 2 * Q_BLOCK), 1)
    bias = bias_ref[0]

    def stream_rows(start):
        if dilation == 1:
            return pl.ds(start, Q_BLOCK)
        return pl.ds(start, Q_BLOCK, stride=dilation)

    for r in range(dilation):
        k_prev = kh_ref[0, stream_rows(r), :].astype(BF16)
        v_prev = vh_ref[0, stream_rows(r), :].astype(BF16)
        for n in range(n_sub):
            rows = stream_rows(r + n * Q_BLOCK * dilation)
            q = q_ref[0, rows, :].astype(BF16)
            k_cur = k_ref[0, rows, :].astype(BF16)
            v_cur = v_ref[0, rows, :].astype(BF16)
            kc = jnp.concatenate([k_prev, k_cur], axis=0)
            vc = jnp.concatenate([v_prev, v_cur], axis=0)
            s = lax.dot_general(q, kc, (((1,), (1,)), ((), ())), preferred_element_type=F32)
            s = s * scale + bias
            if n == 0:
                s = jnp.where(jnp.logical_and(first_chunk, key_col < Q_BLOCK), -jnp.inf, s)
            m = jnp.max(s, axis=-1, keepdims=True)
            p = jnp.exp(s - m)
            den = jnp.sum(p, axis=-1, keepdims=True)
            o_scr[rows, :] = jnp.dot(p.astype(BF16), vc, preferred_element_type=F32) / den
            l_scr[rows, :] = jnp.broadcast_to(m + jnp.log(den), (Q_BLOCK, HEAD_DIM))
            k_prev, v_prev = k_cur, v_cur


def _attn_kernel(*refs, dilations, scale):
    n_groups = len(dilations)
    group_refs = [refs[6 * g:6 * (g + 1)] for g in range(n_groups)]
    out_ref = refs[6 * n_groups]
    scratch = refs[6 * n_groups + 1:]
    o_scrs, l_scrs = scratch[:n_groups], scratch[n_groups:]
    for g, dilation in enumerate(dilations):
        _group_attention_into(*group_refs[g], o_scrs[g], l_scrs[g], dilation=dilation, scale=scale)

    def mix_rows(rows):
        lses = [l[rows, :] for l in l_scrs]
        m = functools.reduce(jnp.maximum, lses)
        es = [jnp.exp(l - m) for l in lses]
        num = sum(e * o[rows, :] for e, o in zip(es, o_scrs))
        out_ref[0, rows, :] = num / sum(es)

    _for_row_chunks(out_ref.shape[1], mix_rows)


def _t5_causal_bucket(dist):
    max_exact = N_BUCKETS // 2
    df = jnp.maximum(dist, 1).astype(F32)
    large = max_exact + (jnp.log(df / max_exact) / math.log(MAX_DISTANCE / max_exact)
                         * (N_BUCKETS - max_exact)).astype(jnp.int32)
    large = jnp.minimum(large, N_BUCKETS - 1)
    return jnp.where(dist < max_exact, dist, large)


def _band_bias(bias_table, window, dilation):
    span = window // dilation
    qi = jnp.arange(Q_BLOCK)[:, None]
    kj = jnp.arange(2 * Q_BLOCK)[None, :]
    step = qi + Q_BLOCK - kj
    valid = (step >= 0) & (step <= span)
    bucket = _t5_causal_bucket(jnp.clip(step, 0, span) * dilation)
    onehot = (bucket[:, :, None] == jnp.arange(N_BUCKETS)[None, None, :]).astype(F32)
    bias = jnp.einsum("qkb,bh->hqk", onehot, bias_table.astype(F32), precision=lax.Precision.HIGHEST)
    return jnp.where(valid[None], bias, -jnp.inf)


def _attention(z, biases, dilations, *, batch, seq, attn_width, chunk=2048):
    assert seq % chunk == 0 and attn_width % HEAD_DIM == 0
    zv = z.reshape(batch, seq, z.shape[1])
    blk = _nbytes((chunk, HEAD_DIM), F32)
    in_specs, operands, pipelined = [], [], blk
    for group, (dilation, bias) in enumerate(zip(dilations, biases)):
        hist = Q_BLOCK * dilation
        assert chunk % hist == 0
        q_col = group * HEADS_PER_GROUP
        k_col = q_col + attn_width // HEAD_DIM
        v_col = k_col + attn_width // HEAD_DIM

        def cur(col):
            return pl.BlockSpec((1, chunk, HEAD_DIM), lambda b, c, hh, col=col: (b, c, col + hh))

        def prev(col, hist=hist):
            return pl.BlockSpec(
                (1, hist, HEAD_DIM),
                lambda b, c, hh, col=col, hist=hist: (b, jnp.maximum(c * (chunk // hist) - 1, 0), col + hh))

        in_specs += [cur(q_col), cur(k_col), cur(v_col), prev(k_col), prev(v_col),
                     pl.BlockSpec((1,) + bias.shape[1:], lambda b, c, hh: (hh, 0, 0))]
        operands += [zv, zv, zv, zv, zv, bias]
        pipelined += 3 * blk + 2 * _nbytes((hist, HEAD_DIM), F32) + _nbytes(bias.shape[1:], F32)
    n_groups = len(dilations)
    o = pl.pallas_call(
        functools.partial(_attn_kernel, dilations=tuple(dilations), scale=HEAD_DIM ** -0.5),
        grid=(batch, seq // chunk, HEADS_PER_GROUP),
        in_specs=in_specs,
        out_specs=pl.BlockSpec((1, chunk, HEAD_DIM), lambda b, c, hh: (b, c, hh)),
        out_shape=jax.ShapeDtypeStruct((batch, seq, GROUP_WIDTH), F32),
        scratch_shapes=[pltpu.VMEM((chunk, HEAD_DIM), F32)] * (2 * n_groups),
        compiler_params=pltpu.CompilerParams(
            dimension_semantics=("arbitrary",) * 3,
            vmem_limit_bytes=_vmem_limit(pipelined, 2 * n_groups * blk, 4 * blk)),
        name="dilated_attention",
    )(*operands)
    return o.reshape(batch * seq, GROUP_WIDTH)


def _merge_kernel(oattn_ref, p_ref, pp_ref, ga0_ref, ga1_ref, gb0_ref, gb1_ref, h_ref,
                  wpool_ref, pscale_ref, wao_ref, wpo_ref, wout_ref, gain_ref, out_ref,
                  *, tm, tiles_per_seq):
    tile_in_seq = pl.program_id(0) % tiles_per_seq

    halo = jnp.where(tile_in_seq == 0, 0.0, pp_ref[...])
    xe = jnp.concatenate([halo, p_ref[...]], axis=0)
    t = tile_in_seq * tm + lax.broadcasted_iota(jnp.int32, (tm, 1), 0)
    pooled = []
    for g, size in enumerate(POOL_SIZES):
        a = xe[:, g * POOL_GROUP_WIDTH:(g + 1) * POOL_GROUP_WIDTH]
        s, shift = a, 1
        while shift < size:
            s = s + pltpu.roll(s, shift, axis=0)
            shift *= 2
        count = jnp.minimum(t + 1, size).astype(F32)
        y = (s[POOL_HALO:] / count - a[POOL_HALO:]).astype(BF16)
        pooled.append(jnp.dot(y, wpool_ref[g], preferred_element_type=F32))
    o_pool = jnp.concatenate(pooled, axis=1) * pscale_ref[...]

    attn_proj = jnp.dot(oattn_ref[...].astype(BF16), wao_ref[...], preferred_element_type=F32)
    pool_proj = jnp.dot(o_pool.astype(BF16), wpo_ref[...], preferred_element_type=F32)
    gate_a = jax.nn.sigmoid(jnp.concatenate([ga0_ref[...], ga1_ref[...]], axis=1))
    gate_b = jax.nn.sigmoid(jnp.concatenate([gb0_ref[...], gb1_ref[...]], axis=1))
    merged = gate_a * attn_proj + gate_b * pool_proj
    mix = jnp.dot(merged.astype(BF16), wout_ref[...], preferred_element_type=F32)
    out_ref[...] = h_ref[...] + _rms_norm_f32(mix, gain_ref[...])


def _merge(h, z, o_attn, w_pool, pool_scale, w_attn_out, w_pool_out, w_out, gain, *, seq, attn_width, tm=256):
    n, d = h.shape
    pool_width = w_pool.shape[0] * POOL_GROUP_WIDTH
    assert pool_width == GROUP_WIDTH
    assert n % tm == 0 and seq % tm == 0 and tm % POOL_HALO == 0
    p_col = 3 * attn_width // GROUP_WIDTH
    gate_w = d // 2
    gate_col = (3 * attn_width + pool_width) // gate_w
    assert (3 * attn_width + pool_width) % gate_w == 0

    row_blk = lambda c: pl.BlockSpec((tm, GROUP_WIDTH), lambda i: (i, c))
    gate_blk = lambda c: pl.BlockSpec((tm, gate_w), lambda i: (i, gate_col + c))
    halo_blk = pl.BlockSpec((POOL_HALO, GROUP_WIDTH),
                            lambda i: (jnp.maximum(i * (tm // POOL_HALO) - 1, 0), p_col))
    full = lambda a: pl.BlockSpec(a.shape, lambda i: (0,) * a.ndim)
    weights = (w_pool, pool_scale, w_attn_out, w_pool_out, w_out, gain)

    act = 2 * _nbytes((tm, GROUP_WIDTH), F32) + 4 * _nbytes((tm, gate_w), F32) + 2 * _nbytes((tm, d), F32)
    wbytes = sum(_nbytes(a.shape, a.dtype) for a in weights)
    limit = _vmem_limit(act + wbytes, 0, 8 * _nbytes((tm, d), F32))
    return pl.pallas_call(
        functools.partial(_merge_kernel, tm=tm, tiles_per_seq=seq // tm),
        grid=(n // tm,),
        in_specs=[row_blk(0), row_blk(p_col), halo_blk] + [gate_blk(c) for c in range(4)]
                 + [pl.BlockSpec((tm, d), lambda i: (i, 0))] + [full(a) for a in weights],
        out_specs=pl.BlockSpec((tm, d), lambda i: (i, 0)),
        out_shape=jax.ShapeDtypeStruct((n, d), F32),
        compiler_params=pltpu.CompilerParams(dimension_semantics=("arbitrary",), vmem_limit_bytes=limit),
        name="mixer_merge",
    )(o_attn, z, z, z, z, z, z, h, *weights)


def _unpack_bf16_pair(words):
    lo = lax.bitcast_convert_type(words << 16, F32)
    hi = lax.bitcast_convert_type(words & jnp.uint32(0xFFFF0000), F32)
    return jnp.concatenate([lo, hi], axis=1).astype(BF16)


def _swiglu_accumulate(u_scr, wg_ref, wu_ref, wd_ref, acc_ref, rows=slice(None)):
    u = u_scr[rows, :]
    gate = jnp.dot(u, wg_ref[...].astype(BF16), preferred_element_type=F32)
    up = jnp.dot(u, wu_ref[...].astype(BF16), preferred_element_type=F32)
    hidden = (gate * jax.nn.sigmoid(gate) * up).astype(BF16)
    acc_ref[rows, :] += jnp.dot(hidden, wd_ref[...].astype(BF16), preferred_element_type=F32)


def _ffn_vmem_limit(tm, d, tf, x_bytes, scratch_bytes):
    return _vmem_limit(
        x_bytes + 3 * _nbytes((d, tf), F32) + _nbytes((tm, d), F32),
        _nbytes((tm, d), BF16) + scratch_bytes,
        3 * _nbytes((d, tf), BF16) + 4 * _nbytes((tm, tf), F32) + 2 * _nbytes((tm, d), F32))


def _dense_ffn_kernel(h_ref, gin_ref, wg_ref, wu_ref, wd_ref, gout_ref, out_ref, u_scr):
    j = pl.program_id(1)

    @pl.when(j == 0)
    def _():
        def norm_rows(rows):
            u_scr[rows, :] = _rms_norm_f32(h_ref[rows, :], gin_ref[...]).astype(BF16)
            out_ref[rows, :] = jnp.zeros((ROW_CHUNK, out_ref.shape[1]), F32)

        _for_row_chunks(u_scr.shape[0], norm_rows)

    _swiglu_accumulate(u_scr, wg_ref, wu_ref, wd_ref, out_ref)

    @pl.when(j == pl.num_programs(1) - 1)
    def _():
        def residual_rows(rows):
            out_ref[rows, :] = h_ref[rows, :] + _rms_norm_f32(out_ref[rows, :], gout_ref[...])

        _for_row_chunks(u_scr.shape[0], residual_rows)


def _dense_ffn(h, gain_in, gain_out, w_gate, w_up, w_down, *, tm=1024, tf=256):
    n, d = h.shape
    ff = w_gate.shape[1]
    assert n % tm == 0 and ff % tf == 0
    row_blk = pl.BlockSpec((tm, d), lambda i, j: (i, 0))
    gain_blk = pl.BlockSpec((1, d), lambda i, j: (0, 0))
    up_blk = pl.BlockSpec((d, tf), lambda i, j: (0, j))
    return pl.pallas_call(
        _dense_ffn_kernel,
        grid=(n // tm, ff // tf),
        in_specs=[row_blk, gain_blk, up_blk, up_blk, pl.BlockSpec((tf, d), lambda i, j: (j, 0)), gain_blk],
        out_specs=row_blk,
        out_shape=jax.ShapeDtypeStruct((n, d), F32),
        scratch_shapes=[pltpu.VMEM((tm, d), BF16)],
        compiler_params=pltpu.CompilerParams(
            dimension_semantics=("arbitrary", "arbitrary"),
            vmem_limit_bytes=_ffn_vmem_limit(tm, d, tf, _nbytes((tm, d), F32), 0)),
        name="swiglu_dense",
    )(h, gain_in, w_gate, w_up, w_down, gain_out)


def _expert_ffn_kernel(tile_expert_ref, tile_rows_ref, src_ref, x_hbm, wg_ref, wu_ref, wd_ref, out_ref,
                       u_scr, stage, sem):
    del tile_expert_ref
    i, j = pl.program_id(0), pl.program_id(1)
    tm = stage.shape[0]
    n_rows = tile_rows_ref[i]
    n_chunks = n_rows // ROW_CHUNK

    @pl.when(j == 0)
    def _():
        def zero_rows(rows):
            out_ref[rows, :] = jnp.zeros((ROW_CHUNK, out_ref.shape[1]), F32)

        _for_row_chunks(tm, zero_rows)

        def gather_chunk(c, carry):
            base = c * ROW_CHUNK

            def start(r, inner):
                pltpu.make_async_copy(x_hbm.at[pl.ds(src_ref[i * tm + base + r], 1), :],
                                      stage.at[pl.ds(base + r, 1), :], sem).start()
                return inner

            return lax.fori_loop(0, ROW_CHUNK, start, carry, unroll=8)

        lax.fori_loop(0, n_chunks, gather_chunk, 0)

        def wait_chunk(c, carry):
            pltpu.make_async_copy(x_hbm.at[pl.ds(0, ROW_CHUNK), :], stage.at[pl.ds(0, ROW_CHUNK), :], sem).wait()
            return carry

        lax.fori_loop(0, n_chunks, wait_chunk, 0)

        def unpack_chunk(c, carry):
            rows = pl.ds(pl.multiple_of(c * ROW_CHUNK, ROW_CHUNK), ROW_CHUNK)
            u_scr[rows, :] = _unpack_bf16_pair(stage[rows, :])
            return carry

        lax.fori_loop(0, n_chunks, unpack_chunk, 0)

    for k in range(1, tm // ROW_CHUNK + 1):
        @pl.when(n_chunks == k)
        def _(k=k):
            _swiglu_accumulate(u_scr, wg_ref, wu_ref, wd_ref, out_ref, slice(0, k * ROW_CHUNK))


def _expert_ffn(xp, src, tile_expert, tile_rows, w_gate, w_up, w_down, *, tm, tf):
    rows = src.shape[0]
    n_exp, d, ff = w_gate.shape
    assert rows % tm == 0 and tm % ROW_CHUNK == 0 and ff % tf == 0 and xp.shape[1] == d // 2
    nj = ff // tf

    def ff_tile(i, j, tr):
        return jnp.where(tr[i] > 0, j, nj - 1)

    up_blk = pl.BlockSpec((None, d, tf), lambda i, j, te, tr, src: (te[i], 0, ff_tile(i, j, tr)))
    down_blk = pl.BlockSpec((None, tf, d), lambda i, j, te, tr, src: (te[i], ff_tile(i, j, tr), 0))
    return pl.pallas_call(
        _expert_ffn_kernel,
        grid_spec=pltpu.PrefetchScalarGridSpec(
            num_scalar_prefetch=3,
            grid=(rows // tm, nj),
            in_specs=[pl.BlockSpec(memory_space=pl.ANY), up_blk, up_blk, down_blk],
            out_specs=pl.BlockSpec((tm, d), lambda i, j, te, tr, src: (i, 0)),
            scratch_shapes=[pltpu.VMEM((tm, d), BF16), pltpu.VMEM((tm, d // 2), jnp.uint32),
                            pltpu.SemaphoreType.DMA]),
        out_shape=jax.ShapeDtypeStruct((rows, d), F32),
        compiler_params=pltpu.CompilerParams(
            dimension_semantics=("arbitrary", "arbitrary"),
            vmem_limit_bytes=_ffn_vmem_limit(tm, d, tf, 0, _nbytes((tm, d // 2), jnp.uint32))),
        name="swiglu_experts",
    )(tile_expert, tile_rows, src, xp, w_gate, w_up, w_down)


def _router_kernel(h_ref, gain_ref, wrt_ref, xp_ref, idx_ref, wts_ref):
    u = _rms_norm_f32(h_ref[...], gain_ref[...])
    logits = lax.dot_general(wrt_ref[...], u, (((1,), (1,)), ((), ())),
                             precision=lax.Precision.HIGHEST, preferred_element_type=F32)
    n_exp = logits.shape[0]
    expert = lax.broadcasted_iota(jnp.int32, logits.shape, 0)
    v1 = jnp.max(logits, axis=0, keepdims=True)
    i1 = jnp.min(jnp.where(logits == v1, expert, n_exp), axis=0, keepdims=True)
    rest = jnp.where(expert == i1, -jnp.inf, logits)
    v2 = jnp.max(rest, axis=0, keepdims=True)
    i2 = jnp.min(jnp.where(rest == v2, expert, n_exp), axis=0, keepdims=True)
    e2 = jnp.exp(v2 - v1)
    idx_ref[...] = jnp.concatenate([i1, i2], axis=0)
    wts_ref[...] = jnp.concatenate([1.0 / (1.0 + e2), e2 / (1.0 + e2)], axis=0)

    bits = lax.bitcast_convert_type(u.astype(BF16).astype(F32), jnp.uint32)
    half = bits.shape[1] // 2
    xp_ref[...] = (bits[:, :half] >> 16) | bits[:, half:]


def _router(h, gain, w_router_t, *, tm=512):
    n, d = h.shape
    n_exp = w_router_t.shape[0]
    assert n % tm == 0
    limit = _vmem_limit(_nbytes((tm, d), F32) + _nbytes((tm, d // 2), jnp.uint32) + _nbytes((n_exp, d), F32),
                        0, 6 * _nbytes((tm, d), F32))
    return pl.pallas_call(
        _router_kernel,
        grid=(n // tm,),
        in_specs=[pl.BlockSpec((tm, d), lambda i: (i, 0)),
                  pl.BlockSpec((1, d), lambda i: (0, 0)),
                  pl.BlockSpec((n_exp, d), lambda i: (0, 0))],
        out_specs=[pl.BlockSpec((tm, d // 2), lambda i: (i, 0)),
                   pl.BlockSpec((TOP_K, tm), lambda i: (0, i)),
                   pl.BlockSpec((TOP_K, tm), lambda i: (0, i))],
        out_shape=[jax.ShapeDtypeStruct((n, d // 2), jnp.uint32),
                   jax.ShapeDtypeStruct((TOP_K, n), jnp.int32),
                   jax.ShapeDtypeStruct((TOP_K, n), F32)],
        compiler_params=pltpu.CompilerParams(dimension_semantics=("arbitrary",), vmem_limit_bytes=limit),
        name="moe_router",
    )(h, gain, w_router_t)


def _combine_kernel(pos_ref, y_hbm, wts_ref, h_ref, gain_ref, out_ref, buf0, buf1, sems, *, tm):
    base = pl.program_id(0) * tm
    bufs = (buf0, buf1)

    def start(r, carry):
        for k in range(TOP_K):
            pltpu.make_async_copy(y_hbm.at[pl.ds(pos_ref[TOP_K * (base + r) + k], 1), :],
                                  bufs[k].at[pl.ds(r, 1), :], sems.at[k]).start()
        return carry

    lax.fori_loop(0, tm, start, 0, unroll=8)
    for k in range(TOP_K):
        pltpu.make_async_copy(y_hbm.at[pl.ds(0, tm), :], bufs[k], sems.at[k]).wait()
    w = wts_ref[...]
    y = w[:, 0:1] * buf0[...] + w[:, 1:2] * buf1[...]
    out_ref[...] = h_ref[...] + _rms_norm_f32(y, gain_ref[...])


def _combine(y_rows, pos, wts, h, gain, *, tm=256):
    n, d = h.shape
    assert n % tm == 0
    limit = _vmem_limit(2 * _nbytes((tm, d), F32) + _nbytes((tm, 128), F32),
                        2 * _nbytes((tm, d), F32), 4 * _nbytes((tm, d), F32))
    return pl.pallas_call(
        functools.partial(_combine_kernel, tm=tm),
        grid_spec=pltpu.PrefetchScalarGridSpec(
            num_scalar_prefetch=1,
            grid=(n // tm,),
            in_specs=[pl.BlockSpec(memory_space=pl.ANY),
                      pl.BlockSpec((tm, TOP_K), lambda i, pos: (i, 0)),
                      pl.BlockSpec((tm, d), lambda i, pos: (i, 0)),
                      pl.BlockSpec((1, d), lambda i, pos: (0, 0))],
            out_specs=pl.BlockSpec((tm, d), lambda i, pos: (i, 0)),
            scratch_shapes=[pltpu.VMEM((tm, d), F32), pltpu.VMEM((tm, d), F32),
                            pltpu.SemaphoreType.DMA((TOP_K,))]),
        out_shape=jax.ShapeDtypeStruct((n, d), F32),
        compiler_params=pltpu.CompilerParams(dimension_semantics=("arbitrary",), vmem_limit_bytes=limit),
        name="moe_combine",
    )(pos, y_rows, wts, h, gain)


def _dispatch_plan(top_idx, n_exp, tm):
    n = top_idx.shape[1]
    n_assign = n * TOP_K
    n_tiles = n_assign // tm + n_exp
    expert = top_idx.T.reshape(n_assign)
    onehot = (expert[:, None] == jnp.arange(n_exp, dtype=jnp.int32)[None, :]).astype(jnp.int32)
    rank = jnp.sum((jnp.cumsum(onehot, axis=0) - onehot) * onehot, axis=1)
    counts = jnp.sum(onehot, axis=0)
    tiles_per_expert = (counts + tm - 1) // tm
    tile_ends = jnp.cumsum(tiles_per_expert)
    tile_starts = tile_ends - tiles_per_expert
    pos = tile_starts[expert] * tm + rank
    src = jnp.zeros((n_tiles * tm,), jnp.int32).at[pos].set(jnp.arange(n_assign, dtype=jnp.int32) // TOP_K)
    tile = jnp.arange(n_tiles, dtype=jnp.int32)
    tile_expert = jnp.sum((tile[:, None] >= tile_ends[None, :]).astype(jnp.int32), axis=1)
    used = tile < tile_ends[-1]
    tile_expert = jnp.where(used, tile_expert, tile_expert[tile_ends[-1] - 1])
    rows_left = (counts[tile_expert] + ROW_CHUNK - 1) // ROW_CHUNK * ROW_CHUNK - (tile - tile_starts[tile_expert]) * tm
    tile_rows = jnp.where(used, jnp.clip(rows_left, 0, tm), 0)
    return pos.astype(jnp.int32), src, tile_expert.astype(jnp.int32), tile_rows.astype(jnp.int32)


def _moe(h, gain_in, gain_out, w_router, w_gate, w_up, w_down, *, tm=1280, tf=256):
    n_exp = w_router.shape[1]
    xp, top_idx, top_w = _router(h, gain_in, w_router.T)
    pos, src, tile_expert, tile_rows = _dispatch_plan(top_idx, n_exp, tm)
    y_rows = _expert_ffn(xp, src, tile_expert, tile_rows, w_gate, w_up, w_down, tm=tm, tf=tf)
    return _combine(y_rows, pos, top_w.T, h, gain_out)


def kernel(x, w_in, rel_bias, w_pool, pool_scale, w_attn_out, w_pool_out, w_out, norm_gains,
           dense_w_gate, dense_w_up, dense_w_down, moe_w_router, moe_w_gate, moe_w_up, moe_w_down):
    batch, seq, d = x.shape
    depth = w_in.shape[0]
    attn_width = N_ATTN_GROUPS * GROUP_WIDTH
    h = x.reshape(batch * seq, d)
    biases = [_band_bias(rel_bias[:, g * HEADS_PER_GROUP:(g + 1) * HEADS_PER_GROUP], window, dilation)
              for g, (window, dilation) in enumerate(DILATION_PATTERNS)]
    for layer in range(depth):
        gains = norm_gains[layer].reshape(4, 1, d)
        z = _in_proj(h, gains[0], w_in, layer)
        o_attn = _attention(z, biases, [dilation for _, dilation in DILATION_PATTERNS],
                            batch=batch, seq=seq, attn_width=attn_width)
        h = _merge(h, z, o_attn, w_pool[layer].astype(BF16), pool_scale[layer].reshape(1, -1),
                   w_attn_out[layer].astype(BF16), w_pool_out[layer].astype(BF16), w_out[layer].astype(BF16),
                   gains[1], seq=seq, attn_width=attn_width)
        j = layer // 2
        if layer % 2 == 0:
            h = _dense_ffn(h, gains[2], gains[3], dense_w_gate[j], dense_w_up[j], dense_w_down[j])
        else:
            h = _moe(h, gains[2], gains[3], moe_w_router[j], moe_w_gate[j], moe_w_up[j], moe_w_down[j])
    return h.reshape(batch, seq, d)
```

```python
import functools
import math

import jax
import jax.numpy as jnp
from jax import lax
from jax.experimental import pallas as pl
from jax.experimental.pallas import tpu as pltpu

F32 = jnp.float32
BF16 = jnp.bfloat16

RMS_EPS = 1e-6
HEAD_DIM = 128
Q_BLOCK = 128
DILATION_PATTERNS = ((128, 1), (512, 4), (2048, 16))
N_ATTN_GROUPS = len(DILATION_PATTERNS)
HEADS_PER_GROUP = 4
GROUP_WIDTH = HEADS_PER_GROUP * HEAD_DIM
POOL_SIZES = (2, 4, 8, 16)
POOL_GROUP_WIDTH = 128
POOL_HALO = 16
N_BUCKETS = 32
MAX_DISTANCE = 2048
N_EXPERTS = 8
TOP_K = 2

V7X_VMEM_BYTES = 64 * 1024 * 1024


def _vmem_limit(pipelined_bytes, resident_bytes, temp_bytes):
    return min(2 * pipelined_bytes + resident_bytes + temp_bytes, V7X_VMEM_BYTES)


def _nbytes(shape, dtype):
    return math.prod(shape) * jnp.dtype(dtype).itemsize


def _rms_norm_f32(x, gain):
    ms = jnp.mean(x * x, axis=-1, keepdims=True)
    return x * lax.rsqrt(ms + RMS_EPS) * gain


ROW_CHUNK = 256


def _for_row_chunks(n_rows, body):
    assert n_rows % ROW_CHUNK == 0

    def step(c, carry):
        body(pl.ds(pl.multiple_of(c * ROW_CHUNK, ROW_CHUNK), ROW_CHUNK))
        return carry

    lax.fori_loop(0, n_rows // ROW_CHUNK, step, 0)


def _inproj_kernel(h_ref, g_ref, w_ref, z_ref, u_scr):
    @pl.when(pl.program_id(1) == 0)
    def _():
        def norm_rows(rows):
            u_scr[rows, :] = _rms_norm_f32(h_ref[rows, :], g_ref[...]).astype(BF16)

        _for_row_chunks(u_scr.shape[0], norm_rows)

    z_ref[...] = jnp.dot(u_scr[...], w_ref[...].astype(BF16), preferred_element_type=F32)


def _in_proj(h, gain, w_all, layer, *, tm=2048, tn=512):
    n, d = h.shape
    width = w_all.shape[2]
    assert n % tm == 0 and width % tn == 0
    limit = _vmem_limit(
        _nbytes((d, tn), F32) + _nbytes((tm, tn), F32),
        _nbytes((tm, d), F32) + _nbytes((tm, d), BF16),
        _nbytes((tm, d), F32) + _nbytes((d, tn), BF16) + _nbytes((tm, tn), F32))
    return pl.pallas_call(
        _inproj_kernel,
        grid=(n // tm, width // tn),
        in_specs=[pl.BlockSpec((tm, d), lambda i, j: (i, 0), pipeline_mode=pl.Buffered(1)),
                  pl.BlockSpec((1, d), lambda i, j: (0, 0)),
                  pl.BlockSpec((None, d, tn), lambda i, j: (layer, 0, j))],
        out_specs=pl.BlockSpec((tm, tn), lambda i, j: (i, j)),
        out_shape=jax.ShapeDtypeStruct((n, width), F32),
        scratch_shapes=[pltpu.VMEM((tm, d), BF16)],
        compiler_params=pltpu.CompilerParams(dimension_semantics=("arbitrary", "arbitrary"),
                                             vmem_limit_bytes=limit),
        name="in_proj",
    )(h, gain, w_all)


def _group_attention_into(q_ref, k_ref, v_ref, kh_ref, vh_ref, bias_ref, o_scr, l_scr, *, dilation, scale):
    first_chunk = pl.program_id(1) == 0
    n_sub = q_ref.shape[1] // (dilation * Q_BLOCK)
    key_col = lax.broadcasted_iota(jnp.int32, (Q_BLOCK, 2 * Q_BLOCK), 1)
    bias = bias_ref[0]

    def stream_rows(start):
        if dilation == 1:
            return pl.ds(start, Q_BLOCK)
        return pl.ds(start, Q_BLOCK, stride=dilation)

    for r in range(dilation):
        k_prev = kh_ref[0, stream_rows(r), :].astype(BF16)
        v_prev = vh_ref[0, stream_rows(r), :].astype(BF16)
        for n in range(n_sub):
            rows = stream_rows(r + n * Q_BLOCK * dilation)
            q = q_ref[0, rows, :].astype(BF16)
            k_cur = k_ref[0, rows, :].astype(BF16)
            v_cur = v_ref[0, rows, :].astype(BF16)
            kc = jnp.concatenate([k_prev, k_cur], axis=0)
            vc = jnp.concatenate([v_prev, v_cur], axis=0)
            s = lax.dot_general(q, kc, (((1,), (1,)), ((), ())), preferred_element_type=F32)
            s = s * scale + bias
            if n == 0:
                s = jnp.where(jnp.logical_and(first_chunk, key_col < Q_BLOCK), -jnp.inf, s)
            m = jnp.max(s, axis=-1, keepdims=True)
            p = jnp.exp(s - m)
            den = jnp.sum(p, axis=-1, keepdims=True)
            o_scr[rows, :] = jnp.dot(p.astype(BF16), vc, preferred_element_type=F32) / den
            l_scr[rows, :] = jnp.broadcast_to(m + jnp.log(den), (Q_BLOCK, HEAD_DIM))
            k_prev, v_prev = k_cur, v_cur


def _attn_kernel(*refs, dilations, scale):
    n_groups = len(dilations)
    group_refs = [refs[6 * g:6 * (g + 1)] for g in range(n_groups)]
    out_ref = refs[6 * n_groups]
    scratch = refs[6 * n_groups + 1:]
    o_scrs, l_scrs = scratch[:n_groups], scratch[n_groups:]
    for g, dilation in enumerate(dilations):
        _group_attention_into(*group_refs[g], o_scrs[g], l_scrs[g], dilation=dilation, scale=scale)

    def mix_rows(rows):
        lses = [l[rows, :] for l in l_scrs]
        m = functools.reduce(jnp.maximum, lses)
        es = [jnp.exp(l - m) for l in lses]
        num = sum(e * o[rows, :] for e, o in zip(es, o_scrs))
        out_ref[0, rows, :] = num / sum(es)

    _for_row_chunks(out_ref.shape[1], mix_rows)


def _t5_causal_bucket(dist):
    max_exact = N_BUCKETS // 2
    df = jnp.maximum(dist, 1).astype(F32)
    large = max_exact + (jnp.log(df / max_exact) / math.log(MAX_DISTANCE / max_exact)
                         * (N_BUCKETS - max_exact)).astype(jnp.int32)
    large = jnp.minimum(large, N_BUCKETS - 1)
    return jnp.where(dist < max_exact, dist, large)


def _band_bias(bias_table, window, dilation):
    span = window // dilation
    qi = jnp.arange(Q_BLOCK)[:, None]
    kj = jnp.arange(2 * Q_BLOCK)[None, :]
    step = qi + Q_BLOCK - kj
    valid = (step >= 0) & (step <= span)
    bucket = _t5_causal_bucket(jnp.clip(step, 0, span) * dilation)
    onehot = (bucket[:, :, None] == jnp.arange(N_BUCKETS)[None, None, :]).astype(F32)
    bias = jnp.einsum("qkb,bh->hqk", onehot, bias_table.astype(F32), precision=lax.Precision.HIGHEST)
    return jnp.where(valid[None], bias, -jnp.inf)


def _attention(z, biases, dilations, *, batch, seq, attn_width, chunk=2048):
    assert seq % chunk == 0 and attn_width % HEAD_DIM == 0
    zv = z.reshape(batch, seq, z.shape[1])
    blk = _nbytes((chunk, HEAD_DIM), F32)
    in_specs, operands, pipelined = [], [], blk
    for group, (dilation, bias) in enumerate(zip(dilations, biases)):
        hist = Q_BLOCK * dilation
        assert chunk % hist == 0
        q_col = group * HEADS_PER_GROUP
        k_col = q_col + attn_width // HEAD_DIM
        v_col = k_col + attn_width // HEAD_DIM

        def cur(col):
            return pl.BlockSpec((1, chunk, HEAD_DIM), lambda b, c, hh, col=col: (b, c, col + hh))

        def prev(col, hist=hist):
            return pl.BlockSpec(
                (1, hist, HEAD_DIM),
                lambda b, c, hh, col=col, hist=hist: (b, jnp.maximum(c * (chunk // hist) - 1, 0), col + hh))

        in_specs += [cur(q_col), cur(k_col), cur(v_col), prev(k_col), prev(v_col),
                     pl.BlockSpec((1,) + bias.shape[1:], lambda b, c, hh: (hh, 0, 0))]
        operands += [zv, zv, zv, zv, zv, bias]
        pipelined += 3 * blk + 2 * _nbytes((hist, HEAD_DIM), F32) + _nbytes(bias.shape[1:], F32)
    n_groups = len(dilations)
    o = pl.pallas_call(
        functools.partial(_attn_kernel, dilations=tuple(dilations), scale=HEAD_DIM ** -0.5),
        grid=(batch, seq // chunk, HEADS_PER_GROUP),
        in_specs=in_specs,
        out_specs=pl.BlockSpec((1, chunk, HEAD_DIM), lambda b, c, hh: (b, c, hh)),
        out_shape=jax.ShapeDtypeStruct((batch, seq, GROUP_WIDTH), F32),
        scratch_shapes=[pltpu.VMEM((chunk, HEAD_DIM), F32)] * (2 * n_groups),
        compiler_params=pltpu.CompilerParams(
            dimension_semantics=("arbitrary",) * 3,
            vmem_limit_bytes=_vmem_limit(pipelined, 2 * n_groups * blk, 4 * blk)),
        name="dilated_attention",
    )(*operands)
    return o.reshape(batch * seq, GROUP_WIDTH)


def _merge_kernel(oattn_ref, p_ref, pp_ref, ga0_ref, ga1_ref, gb0_ref, gb1_ref, h_ref,
                  wpool_ref, pscale_ref, wao_ref, wpo_ref, wout_ref, gain_ref, out_ref,
                  *, tm, tiles_per_seq):
    tile_in_seq = pl.program_id(0) % tiles_per_seq

    halo = jnp.where(tile_in_seq == 0, 0.0, pp_ref[...])
    xe = jnp.concatenate([halo, p_ref[...]], axis=0)
    t = tile_in_seq * tm + lax.broadcasted_iota(jnp.int32, (tm, 1), 0)
    pooled = []
    for g, size in enumerate(POOL_SIZES):
        a = xe[:, g * POOL_GROUP_WIDTH:(g + 1) * POOL_GROUP_WIDTH]
        s, shift = a, 1
        while shift < size:
            s = s + pltpu.roll(s, shift, axis=0)
            shift *= 2
        count = jnp.minimum(t + 1, size).astype(F32)
        y = (s[POOL_HALO:] / count - a[POOL_HALO:]).astype(BF16)
        pooled.append(jnp.dot(y, wpool_ref[g], preferred_element_type=F32))
    o_pool = jnp.concatenate(pooled, axis=1) * pscale_ref[...]

    attn_proj = jnp.dot(oattn_ref[...].astype(BF16), wao_ref[...], preferred_element_type=F32)
    pool_proj = jnp.dot(o_pool.astype(BF16), wpo_ref[...], preferred_element_type=F32)
    gate_a = jax.nn.sigmoid(jnp.concatenate([ga0_ref[...], ga1_ref[...]], axis=1))
    gate_b = jax.nn.sigmoid(jnp.concatenate([gb0_ref[...], gb1_ref[...]], axis=1))
    merged = gate_a * attn_proj + gate_b * pool_proj
    mix = jnp.dot(merged.astype(BF16), wout_ref[...], preferred_element_type=F32)
    out_ref[...] = h_ref[...] + _rms_norm_f32(mix, gain_ref[...])


def _merge(h, z, o_attn, w_pool, pool_scale, w_attn_out, w_pool_out, w_out, gain, *, seq, attn_width, tm=512):
    n, d = h.shape
    pool_width = w_pool.shape[0] * POOL_GROUP_WIDTH
    assert pool_width == GROUP_WIDTH
    assert n % tm == 0 and seq % tm == 0 and tm % POOL_HALO == 0
    p_col = 3 * attn_width // GROUP_WIDTH
    gate_w = d // 2
    gate_col = (3 * attn_width + pool_width) // gate_w
    assert (3 * attn_width + pool_width) % gate_w == 0

    row_blk = lambda c: pl.BlockSpec((tm, GROUP_WIDTH), lambda i: (i, c))
    gate_blk = lambda c: pl.BlockSpec((tm, gate_w), lambda i: (i, gate_col + c))
    halo_blk = pl.BlockSpec((POOL_HALO, GROUP_WIDTH),
                            lambda i: (jnp.maximum(i * (tm // POOL_HALO) - 1, 0), p_col))
    full = lambda a: pl.BlockSpec(a.shape, lambda i: (0,) * a.ndim)
    weights = (w_pool, pool_scale, w_attn_out, w_pool_out, w_out, gain)

    act = 2 * _nbytes((tm, GROUP_WIDTH), F32) + 4 * _nbytes((tm, gate_w), F32) + 2 * _nbytes((tm, d), F32)
    wbytes = sum(_nbytes(a.shape, a.dtype) for a in weights)
    limit = _vmem_limit(act + wbytes, 0, 8 * _nbytes((tm, d), F32))
    return pl.pallas_call(
        functools.partial(_merge_kernel, tm=tm, tiles_per_seq=seq // tm),
        grid=(n // tm,),
        in_specs=[row_blk(0), row_blk(p_col), halo_blk] + [gate_blk(c) for c in range(4)]
                 + [pl.BlockSpec((tm, d), lambda i: (i, 0))] + [full(a) for a in weights],
        out_specs=pl.BlockSpec((tm, d), lambda i: (i, 0)),
        out_shape=jax.ShapeDtypeStruct((n, d), F32),
        compiler_params=pltpu.CompilerParams(dimension_semantics=("arbitrary",), vmem_limit_bytes=limit),
        name="mixer_merge",
    )(o_attn, z, z, z, z, z, z, h, *weights)


def _unpack_bf16_pair(words):
    lo = lax.bitcast_convert_type(words << 16, F32)
    hi = lax.bitcast_convert_type(words & jnp.uint32(0xFFFF0000), F32)
    return jnp.concatenate([lo, hi], axis=1).astype(BF16)


def _swiglu_accumulate(u_scr, wg_ref, wu_ref, wd_ref, acc_ref, rows=slice(None)):
    u = u_scr[rows, :]
    gate = jnp.dot(u, wg_ref[...].astype(BF16), preferred_element_type=F32)
    up = jnp.dot(u, wu_ref[...].astype(BF16), preferred_element_type=F32)
    hidden = (gate * jax.nn.sigmoid(gate) * up).astype(BF16)
    acc_ref[rows, :] += jnp.dot(hidden, wd_ref[...].astype(BF16), preferred_element_type=F32)


def _ffn_vmem_limit(tm, d, tf, x_bytes, scratch_bytes):
    return _vmem_limit(
        x_bytes + 3 * _nbytes((d, tf), F32) + _nbytes((tm, d), F32),
        _nbytes((tm, d), BF16) + scratch_bytes,
        3 * _nbytes((d, tf), BF16) + 4 * _nbytes((tm, tf), F32) + 2 * _nbytes((tm, d), F32))


def _dense_ffn_kernel(h_ref, gin_ref, wg_ref, wu_ref, wd_ref, gout_ref, out_ref, u_scr):
    j = pl.program_id(1)

    @pl.when(j == 0)
    def _():
        def norm_rows(rows):
            u_scr[rows, :] = _rms_norm_f32(h_ref[rows, :], gin_ref[...]).astype(BF16)
            out_ref[rows, :] = jnp.zeros((ROW_CHUNK, out_ref.shape[1]), F32)

        _for_row_chunks(u_scr.shape[0], norm_rows)

    _swiglu_accumulate(u_scr, wg_ref, wu_ref, wd_ref, out_ref)

    @pl.when(j == pl.num_programs(1) - 1)
    def _():
        def residual_rows(rows):
            out_ref[rows, :] = h_ref[rows, :] + _rms_norm_f32(out_ref[rows, :], gout_ref[...])

        _for_row_chunks(u_scr.shape[0], residual_rows)


def _dense_ffn(h, gain_in, gain_out, w_gate, w_up, w_down, *, tm=1024, tf=512):
    n, d = h.shape
    ff = w_gate.shape[1]
    assert n % tm == 0 and ff % tf == 0
    row_blk = pl.BlockSpec((tm, d), lambda i, j: (i, 0), pipeline_mode=pl.Buffered(1))
    gain_blk = pl.BlockSpec((1, d), lambda i, j: (0, 0))
    up_blk = pl.BlockSpec((d, tf), lambda i, j: (0, j))
    return pl.pallas_call(
        _dense_ffn_kernel,
        grid=(n // tm, ff // tf),
        in_specs=[row_blk, gain_blk, up_blk, up_blk, pl.BlockSpec((tf, d), lambda i, j: (j, 0)), gain_blk],
        out_specs=row_blk,
        out_shape=jax.ShapeDtypeStruct((n, d), F32),
        scratch_shapes=[pltpu.VMEM((tm, d), BF16)],
        compiler_params=pltpu.CompilerParams(
            dimension_semantics=("arbitrary", "arbitrary"),
            vmem_limit_bytes=_ffn_vmem_limit(tm, d, tf, _nbytes((tm, d), F32), 0)),
        name="swiglu_dense",
    )(h, gain_in, w_gate, w_up, w_down, gain_out)


def _expert_ffn_kernel(tile_expert_ref, tile_rows_ref, src_ref, x_hbm, wg_ref, wu_ref, wd_ref, out_ref,
                       u_scr, stage, sem):
    del tile_expert_ref
    i, j = pl.program_id(0), pl.program_id(1)
    tm = stage.shape[0]
    n_rows = tile_rows_ref[i]
    n_chunks = n_rows // ROW_CHUNK

    @pl.when(j == 0)
    def _():
        def zero_rows(rows):
            out_ref[rows, :] = jnp.zeros((ROW_CHUNK, out_ref.shape[1]), F32)

        _for_row_chunks(tm, zero_rows)

        def gather_chunk(c, carry):
            base = c * ROW_CHUNK

            def start(r, inner):
                pltpu.make_async_copy(x_hbm.at[pl.ds(src_ref[i * tm + base + r], 1), :],
                                      stage.at[pl.ds(base + r, 1), :], sem).start()
                return inner

            return lax.fori_loop(0, ROW_CHUNK, start, carry, unroll=8)

        lax.fori_loop(0, n_chunks, gather_chunk, 0)

        def wait_chunk(c, carry):
            pltpu.make_async_copy(x_hbm.at[pl.ds(0, ROW_CHUNK), :], stage.at[pl.ds(0, ROW_CHUNK), :], sem).wait()
            return carry

        lax.fori_loop(0, n_chunks, wait_chunk, 0)

        def unpack_chunk(c, carry):
            rows = pl.ds(pl.multiple_of(c * ROW_CHUNK, ROW_CHUNK), ROW_CHUNK)
            u_scr[rows, :] = _unpack_bf16_pair(stage[rows, :])
            return carry

        lax.fori_loop(0, n_chunks, unpack_chunk, 0)

    for k in range(1, tm // ROW_CHUNK + 1):
        @pl.when(n_chunks == k)
        def _(k=k):
            _swiglu_accumulate(u_scr, wg_ref, wu_ref, wd_ref, out_ref, slice(0, k * ROW_CHUNK))


def _expert_ffn(xp, src, tile_expert, tile_rows, w_gate, w_up, w_down, *, tm, tf):
    rows = src.shape[0]
    n_exp, d, ff = w_gate.shape
    assert rows % tm == 0 and tm % ROW_CHUNK == 0 and ff % tf == 0 and xp.shape[1] == d // 2
    nj = ff // tf

    def ff_tile(i, j, tr):
        return jnp.where(tr[i] > 0, j, nj - 1)

    up_blk = pl.BlockSpec((None, d, tf), lambda i, j, te, tr, src: (te[i], 0, ff_tile(i, j, tr)))
    down_blk = pl.BlockSpec((None, tf, d), lambda i, j, te, tr, src: (te[i], ff_tile(i, j, tr), 0))
    return pl.pallas_call(
        _expert_ffn_kernel,
        grid_spec=pltpu.PrefetchScalarGridSpec(
            num_scalar_prefetch=3,
            grid=(rows // tm, nj),
            in_specs=[pl.BlockSpec(memory_space=pl.ANY), up_blk, up_blk, down_blk],
            out_specs=pl.BlockSpec((tm, d), lambda i, j, te, tr, src: (i, 0)),
            scratch_shapes=[pltpu.VMEM((tm, d), BF16), pltpu.VMEM((tm, d // 2), jnp.uint32),
                            pltpu.SemaphoreType.DMA]),
        out_shape=jax.ShapeDtypeStruct((rows, d), F32),
        compiler_params=pltpu.CompilerParams(
            dimension_semantics=("arbitrary", "arbitrary"),
            vmem_limit_bytes=_ffn_vmem_limit(tm, d, tf, 0, _nbytes((tm, d // 2), jnp.uint32))),
        name="swiglu_experts",
    )(tile_expert, tile_rows, src, xp, w_gate, w_up, w_down)


def _router_kernel(h_ref, gain_ref, wrt_ref, xp_ref, idx_ref, wts_ref):
    u = _rms_norm_f32(h_ref[...], gain_ref[...])
    logits = lax.dot_general(wrt_ref[...], u, (((1,), (1,)), ((), ())),
                             precision=lax.Precision.HIGHEST, preferred_element_type=F32)
    n_exp = logits.shape[0]
    expert = lax.broadcasted_iota(jnp.int32, logits.shape, 0)
    v1 = jnp.max(logits, axis=0, keepdims=True)
    i1 = jnp.min(jnp.where(logits == v1, expert, n_exp), axis=0, keepdims=True)
    rest = jnp.where(expert == i1, -jnp.inf, logits)
    v2 = jnp.max(rest, axis=0, keepdims=True)
    i2 = jnp.min(jnp.where(rest == v2, expert, n_exp), axis=0, keepdims=True)
    e2 = jnp.exp(v2 - v1)
    idx_ref[...] = jnp.concatenate([i1, i2], axis=0)
    wts_ref[...] = jnp.concatenate([1.0 / (1.0 + e2), e2 / (1.0 + e2)], axis=0)

    bits = lax.bitcast_convert_type(u.astype(BF16).astype(F32), jnp.uint32)
    half = bits.shape[1] // 2
    xp_ref[...] = (bits[:, :half] >> 16) | bits[:, half:]


def _router(h, gain, w_router_t, *, tm=512):
    n, d = h.shape
    n_exp = w_router_t.shape[0]
    assert n % tm == 0
    limit = _vmem_limit(_nbytes((tm, d), F32) + _nbytes((tm, d // 2), jnp.uint32) + _nbytes((n_exp, d), F32),
                        0, 6 * _nbytes((tm, d), F32))
    return pl.pallas_call(
        _router_kernel,
        grid=(n // tm,),
        in_specs=[pl.BlockSpec((tm, d), lambda i: (i, 0)),
                  pl.BlockSpec((1, d), lambda i: (0, 0)),
                  pl.BlockSpec((n_exp, d), lambda i: (0, 0))],
        out_specs=[pl.BlockSpec((tm, d // 2), lambda i: (i, 0)),
                   pl.BlockSpec((TOP_K, tm), lambda i: (0, i)),
                   pl.BlockSpec((TOP_K, tm), lambda i: (0, i))],
        out_shape=[jax.ShapeDtypeStruct((n, d // 2), jnp.uint32),
                   jax.ShapeDtypeStruct((TOP_K, n), jnp.int32),
                   jax.ShapeDtypeStruct((TOP_K, n), F32)],
        compiler_params=pltpu.CompilerParams(dimension_semantics=("arbitrary",), vmem_limit_bytes=limit),
        name="moe_router",
    )(h, gain, w_router_t)


def _combine_kernel(pos_ref, y_hbm, wts_ref, h_ref, gain_ref, out_ref, bufs, sems, *, tm):
    i = pl.program_id(0)

    def start_tile(tile, slot):
        def start(r, carry):
            for k in range(TOP_K):
                pltpu.make_async_copy(y_hbm.at[pl.ds(pos_ref[TOP_K * (tile * tm + r) + k], 1), :],
                                      bufs.at[slot, k, pl.ds(r, 1), :], sems.at[slot, k]).start()
            return carry

        lax.fori_loop(0, tm, start, 0, unroll=8)

    @pl.when(i == 0)
    def _():
        start_tile(0, 0)

    @pl.when(i + 1 < pl.num_programs(0))
    def _():
        start_tile(i + 1, (i + 1) % 2)

    slot = i % 2
    for k in range(TOP_K):
        pltpu.make_async_copy(y_hbm.at[pl.ds(0, tm), :], bufs.at[slot, k], sems.at[slot, k]).wait()
    w = wts_ref[...]
    y = w[:, 0:1] * bufs[slot, 0] + w[:, 1:2] * bufs[slot, 1]
    out_ref[...] = h_ref[...] + _rms_norm_f32(y, gain_ref[...])


def _combine(y_rows, pos, wts, h, gain, *, tm=256):
    n, d = h.shape
    assert n % tm == 0
    limit = _vmem_limit(2 * _nbytes((tm, d), F32) + _nbytes((tm, 128), F32),
                        2 * TOP_K * _nbytes((tm, d), F32), 4 * _nbytes((tm, d), F32))
    return pl.pallas_call(
        functools.partial(_combine_kernel, tm=tm),
        grid_spec=pltpu.PrefetchScalarGridSpec(
            num_scalar_prefetch=1,
            grid=(n // tm,),
            in_specs=[pl.BlockSpec(memory_space=pl.ANY),
                      pl.BlockSpec((tm, TOP_K), lambda i, pos: (i, 0)),
                      pl.BlockSpec((tm, d), lambda i, pos: (i, 0)),
                      pl.BlockSpec((1, d), lambda i, pos: (0, 0))],
            out_specs=pl.BlockSpec((tm, d), lambda i, pos: (i, 0)),
            scratch_shapes=[pltpu.VMEM((2, TOP_K, tm, d), F32), pltpu.SemaphoreType.DMA((2, TOP_K))]),
        out_shape=jax.ShapeDtypeStruct((n, d), F32),
        compiler_params=pltpu.CompilerParams(dimension_semantics=("arbitrary",), vmem_limit_bytes=limit),
        name="moe_combine",
    )(pos, y_rows, wts, h, gain)


def _dispatch_plan(top_idx, n_exp, tm):
    n = top_idx.shape[1]
    n_assign = n * TOP_K
    n_tiles = n_assign // tm + n_exp
    expert = top_idx.T.reshape(n_assign)
    onehot = (expert[:, None] == jnp.arange(n_exp, dtype=jnp.int32)[None, :]).astype(jnp.int32)
    rank = jnp.sum((jnp.cumsum(onehot, axis=0) - onehot) * onehot, axis=1)
    counts = jnp.sum(onehot, axis=0)
    tiles_per_expert = (counts + tm - 1) // tm
    tile_ends = jnp.cumsum(tiles_per_expert)
    tile_starts = tile_ends - tiles_per_expert
    pos = tile_starts[expert] * tm + rank
    src = jnp.zeros((n_tiles * tm,), jnp.int32).at[pos].set(jnp.arange(n_assign, dtype=jnp.int32) // TOP_K)
    tile = jnp.arange(n_tiles, dtype=jnp.int32)
    tile_expert = jnp.sum((tile[:, None] >= tile_ends[None, :]).astype(jnp.int32), axis=1)
    used = tile < tile_ends[-1]
    tile_expert = jnp.where(used, tile_expert, tile_expert[tile_ends[-1] - 1])
    rows_left = (counts[tile_expert] + ROW_CHUNK - 1) // ROW_CHUNK * ROW_CHUNK - (tile - tile_starts[tile_expert]) * tm
    tile_rows = jnp.where(used, jnp.clip(rows_left, 0, tm), 0)
    return pos.astype(jnp.int32), src, tile_expert.astype(jnp.int32), tile_rows.astype(jnp.int32)


def _moe(h, gain_in, gain_out, w_router, w_gate, w_up, w_down, *, tm=1280, tf=512):
    n_exp = w_router.shape[1]
    xp, top_idx, top_w = _router(h, gain_in, w_router.T)
    pos, src, tile_expert, tile_rows = _dispatch_plan(top_idx, n_exp, tm)
    y_rows = _expert_ffn(xp, src, tile_expert, tile_rows, w_gate, w_up, w_down, tm=tm, tf=tf)
    return _combine(y_rows, pos, top_w.T, h, gain_out)


def kernel(x, w_in, rel_bias, w_pool, pool_scale, w_attn_out, w_pool_out, w_out, norm_gains,
           dense_w_gate, dense_w_up, dense_w_down, moe_w_router, moe_w_gate, moe_w_up, moe_w_down):
    batch, seq, d = x.shape
    depth = w_in.shape[0]
    attn_width = N_ATTN_GROUPS * GROUP_WIDTH
    h = x.reshape(batch * seq, d)
    biases = [_band_bias(rel_bias[:, g * HEADS_PER_GROUP:(g + 1) * HEADS_PER_GROUP], window, dilation)
              for g, (window, dilation) in enumerate(DILATION_PATTERNS)]
    for layer in range(depth):
        gains = norm_gains[layer].reshape(4, 1, d)
        z = _in_proj(h, gains[0], w_in, layer)
        o_attn = _attention(z, biases, [dilation for _, dilation in DILATION_PATTERNS],
                            batch=batch, seq=seq, attn_width=attn_width)
        h = _merge(h, z, o_attn, w_pool[layer].astype(BF16), pool_scale[layer].reshape(1, -1),
                   w_attn_out[layer].astype(BF16), w_pool_out[layer].astype(BF16), w_out[layer].astype(BF16),
                   gains[1], seq=seq, attn_width=attn_width)
        j = layer // 2
        if layer % 2 == 0:
            h = _dense_ffn(h, gains[2], gains[3], dense_w_gate[j], dense_w_up[j], dense_w_down[j])
        else:
            h = _moe(h, gains[2], gains[3], moe_w_router[j], moe_w_gate[j], moe_w_up[j], moe_w_down[j])
    return h.reshape(batch, seq, d)
```

```python
import functools
import math

import jax
import jax.numpy as jnp
from jax import lax
from jax.experimental import pallas as pl
from jax.experimental.pallas import tpu as pltpu

F32 = jnp.float32
BF16 = jnp.bfloat16

RMS_EPS = 1e-6
HEAD_DIM = 128
Q_BLOCK = 128
DILATION_PATTERNS = ((128, 1), (512, 4), (2048, 16))
N_ATTN_GROUPS = len(DILATION_PATTERNS)
HEADS_PER_GROUP = 4
GROUP_WIDTH = HEADS_PER_GROUP * HEAD_DIM
POOL_SIZES = (2, 4, 8, 16)
POOL_GROUP_WIDTH = 128
POOL_HALO = 16
N_BUCKETS = 32
MAX_DISTANCE = 2048
N_EXPERTS = 8
TOP_K = 2

V7X_VMEM_BYTES = 64 * 1024 * 1024


def _vmem_limit(pipelined_bytes, resident_bytes, temp_bytes):
    return min(2 * pipelined_bytes + resident_bytes + temp_bytes, V7X_VMEM_BYTES)


def _nbytes(shape, dtype):
    return math.prod(shape) * jnp.dtype(dtype).itemsize


def _rms_norm_f32(x, gain):
    ms = jnp.mean(x * x, axis=-1, keepdims=True)
    return x * lax.rsqrt(ms + RMS_EPS) * gain


ROW_CHUNK = 256
EXPERT_ROW_GRANULE = 128


def _for_row_chunks(n_rows, body):
    assert n_rows % ROW_CHUNK == 0

    def step(c, carry):
        body(pl.ds(pl.multiple_of(c * ROW_CHUNK, ROW_CHUNK), ROW_CHUNK))
        return carry

    lax.fori_loop(0, n_rows // ROW_CHUNK, step, 0)


def _inproj_kernel(h_ref, g_ref, w_ref, z_ref, u_scr):
    @pl.when(pl.program_id(1) == 0)
    def _():
        def norm_rows(rows):
            u_scr[rows, :] = _rms_norm_f32(h_ref[rows, :], g_ref[...]).astype(BF16)

        _for_row_chunks(u_scr.shape[0], norm_rows)

    z_ref[...] = jnp.dot(u_scr[...], w_ref[...].astype(BF16), preferred_element_type=F32)


def _in_proj(h, gain, w_all, layer, *, tm=2048, tn=512):
    n, d = h.shape
    width = w_all.shape[2]
    assert n % tm == 0 and width % tn == 0
    limit = _vmem_limit(
        _nbytes((d, tn), F32) + _nbytes((tm, tn), F32),
        _nbytes((tm, d), F32) + _nbytes((tm, d), BF16),
        _nbytes((tm, d), F32) + _nbytes((d, tn), BF16) + _nbytes((tm, tn), F32))
    return pl.pallas_call(
        _inproj_kernel,
        grid=(n // tm, width // tn),
        in_specs=[pl.BlockSpec((tm, d), lambda i, j: (i, 0), pipeline_mode=pl.Buffered(1)),
                  pl.BlockSpec((1, d), lambda i, j: (0, 0)),
                  pl.BlockSpec((None, d, tn), lambda i, j: (layer, 0, j))],
        out_specs=pl.BlockSpec((tm, tn), lambda i, j: (i, j)),
        out_shape=jax.ShapeDtypeStruct((n, width), F32),
        scratch_shapes=[pltpu.VMEM((tm, d), BF16)],
        compiler_params=pltpu.CompilerParams(dimension_semantics=("arbitrary", "arbitrary"),
                                             vmem_limit_bytes=limit),
        name="in_proj",
    )(h, gain, w_all)


def _group_attention_into(q_ref, k_ref, v_ref, kh_ref, vh_ref, bias_ref, o_scr, l_scr, *, dilation, scale):
    first_chunk = pl.program_id(1) == 0
    n_sub = q_ref.shape[1] // (dilation * Q_BLOCK)
    key_col = lax.broadcasted_iota(jnp.int32, (Q_BLOCK, 2 * Q_BLOCK), 1)
    bias = bias_ref[0]

    def stream_rows(start):
        if dilation == 1:
            return pl.ds(start, Q_BLOCK)
        return pl.ds(start, Q_BLOCK, stride=dilation)

    for r in range(dilation):
        k_prev = kh_ref[0, stream_rows(r), :].astype(BF16)
        v_prev = vh_ref[0, stream_rows(r), :].astype(BF16)
        for n in range(n_sub):
            rows = stream_rows(r + n * Q_BLOCK * dilation)
            q = q_ref[0, rows, :].astype(BF16)
            k_cur = k_ref[0, rows, :].astype(BF16)
            v_cur = v_ref[0, rows, :].astype(BF16)
            kc = jnp.concatenate([k_prev, k_cur], axis=0)
            vc = jnp.concatenate([v_prev, v_cur], axis=0)
            s = lax.dot_general(q, kc, (((1,), (1,)), ((), ())), preferred_element_type=F32)
            s = s * scale + bias
            if n == 0:
                s = jnp.where(jnp.logical_and(first_chunk, key_col < Q_BLOCK), -jnp.inf, s)
            m = jnp.max(s, axis=-1, keepdims=True)
            p = jnp.exp(s - m)
            den = jnp.sum(p, axis=-1, keepdims=True)
            o_scr[rows, :] = jnp.dot(p.astype(BF16), vc, preferred_element_type=F32) / den
            l_scr[rows, :] = jnp.broadcast_to(m + jnp.log(den), (Q_BLOCK, HEAD_DIM))
            k_prev, v_prev = k_cur, v_cur


def _attn_kernel(*refs, dilations, scale):
    n_groups = len(dilations)
    group_refs = [refs[6 * g:6 * (g + 1)] for g in range(n_groups)]
    out_ref = refs[6 * n_groups]
    scratch = refs[6 * n_groups + 1:]
    o_scrs, l_scrs = scratch[:n_groups], scratch[n_groups:]
    for g, dilation in enumerate(dilations):
        _group_attention_into(*group_refs[g], o_scrs[g], l_scrs[g], dilation=dilation, scale=scale)

    def mix_rows(rows):
        lses = [l[rows, :] for l in l_scrs]
        m = functools.reduce(jnp.maximum, lses)
        es = [jnp.exp(l - m) for l in lses]
        num = sum(e * o[rows, :] for e, o in zip(es, o_scrs))
        out_ref[0, rows, :] = num / sum(es)

    _for_row_chunks(out_ref.shape[1], mix_rows)


def _t5_causal_bucket(dist):
    max_exact = N_BUCKETS // 2
    df = jnp.maximum(dist, 1).astype(F32)
    large = max_exact + (jnp.log(df / max_exact) / math.log(MAX_DISTANCE / max_exact)
                         * (N_BUCKETS - max_exact)).astype(jnp.int32)
    large = jnp.minimum(large, N_BUCKETS - 1)
    return jnp.where(dist < max_exact, dist, large)


def _band_bias(bias_table, window, dilation):
    span = window // dilation
    qi = jnp.arange(Q_BLOCK)[:, None]
    kj = jnp.arange(2 * Q_BLOCK)[None, :]
    step = qi + Q_BLOCK - kj
    valid = (step >= 0) & (step <= span)
    bucket = _t5_causal_bucket(jnp.clip(step, 0, span) * dilation)
    onehot = (bucket[:, :, None] == jnp.arange(N_BUCKETS)[None, None, :]).astype(F32)
    bias = jnp.einsum("qkb,bh->hqk", onehot, bias_table.astype(F32), precision=lax.Precision.HIGHEST)
    return jnp.where(valid[None], bias, -jnp.inf)


def _attention(z, biases, dilations, *, batch, seq, attn_width, chunk=2048):
    assert seq % chunk == 0 and attn_width % HEAD_DIM == 0
    zv = z.reshape(batch, seq, z.shape[1])
    blk = _nbytes((chunk, HEAD_DIM), F32)
    in_specs, operands, pipelined = [], [], blk
    for group, (dilation, bias) in enumerate(zip(dilations, biases)):
        hist = Q_BLOCK * dilation
        assert chunk % hist == 0
        q_col = group * HEADS_PER_GROUP
        k_col = q_col + attn_width // HEAD_DIM
        v_col = k_col + attn_width // HEAD_DIM

        def cur(col):
            return pl.BlockSpec((1, chunk, HEAD_DIM), lambda b, c, hh, col=col: (b, c, col + hh))

        def prev(col, hist=hist):
            return pl.BlockSpec(
                (1, hist, HEAD_DIM),
                lambda b, c, hh, col=col, hist=hist: (b, jnp.maximum(c * (chunk // hist) - 1, 0), col + hh))

        in_specs += [cur(q_col), cur(k_col), cur(v_col), prev(k_col), prev(v_col),
                     pl.BlockSpec((1,) + bias.shape[1:], lambda b, c, hh: (hh, 0, 0))]
        operands += [zv, zv, zv, zv, zv, bias]
        pipelined += 3 * blk + 2 * _nbytes((hist, HEAD_DIM), F32) + _nbytes(bias.shape[1:], F32)
    n_groups = len(dilations)
    o = pl.pallas_call(
        functools.partial(_attn_kernel, dilations=tuple(dilations), scale=HEAD_DIM ** -0.5),
        grid=(batch, seq // chunk, HEADS_PER_GROUP),
        in_specs=in_specs,
        out_specs=pl.BlockSpec((1, chunk, HEAD_DIM), lambda b, c, hh: (b, c, hh)),
        out_shape=jax.ShapeDtypeStruct((batch, seq, GROUP_WIDTH), F32),
        scratch_shapes=[pltpu.VMEM((chunk, HEAD_DIM), F32)] * (2 * n_groups),
        compiler_params=pltpu.CompilerParams(
            dimension_semantics=("arbitrary",) * 3,
            vmem_limit_bytes=_vmem_limit(pipelined, 2 * n_groups * blk, 4 * blk)),
        name="dilated_attention",
    )(*operands)
    return o.reshape(batch * seq, GROUP_WIDTH)


def _merge_kernel(oattn_ref, p_ref, pp_ref, ga0_ref, ga1_ref, gb0_ref, gb1_ref, h_ref,
                  wpool_ref, pscale_ref, wao_ref, wpo_ref, wout_ref, gain_ref, out_ref,
                  *, tm, tiles_per_seq):
    tile_in_seq = pl.program_id(0) % tiles_per_seq

    halo = jnp.where(tile_in_seq == 0, 0.0, pp_ref[...])
    xe = jnp.concatenate([halo, p_ref[...]], axis=0)
    t = tile_in_seq * tm + lax.broadcasted_iota(jnp.int32, (tm, 1), 0)
    pooled = []
    for g, size in enumerate(POOL_SIZES):
        a = xe[:, g * POOL_GROUP_WIDTH:(g + 1) * POOL_GROUP_WIDTH]
        s, shift = a, 1
        while shift < size:
            s = s + pltpu.roll(s, shift, axis=0)
            shift *= 2
        count = jnp.minimum(t + 1, size).astype(F32)
        y = (s[POOL_HALO:] / count - a[POOL_HALO:]).astype(BF16)
        pooled.append(jnp.dot(y, wpool_ref[g], preferred_element_type=F32))
    o_pool = jnp.concatenate(pooled, axis=1) * pscale_ref[...]

    attn_proj = jnp.dot(oattn_ref[...].astype(BF16), wao_ref[...], preferred_element_type=F32)
    pool_proj = jnp.dot(o_pool.astype(BF16), wpo_ref[...], preferred_element_type=F32)
    gate_a = jax.nn.sigmoid(jnp.concatenate([ga0_ref[...], ga1_ref[...]], axis=1))
    gate_b = jax.nn.sigmoid(jnp.concatenate([gb0_ref[...], gb1_ref[...]], axis=1))
    merged = gate_a * attn_proj + gate_b * pool_proj
    mix = jnp.dot(merged.astype(BF16), wout_ref[...], preferred_element_type=F32)
    out_ref[...] = h_ref[...] + _rms_norm_f32(mix, gain_ref[...])


def _merge(h, z, o_attn, w_pool, pool_scale, w_attn_out, w_pool_out, w_out, gain, *, seq, attn_width, tm=512):
    n, d = h.shape
    pool_width = w_pool.shape[0] * POOL_GROUP_WIDTH
    assert pool_width == GROUP_WIDTH
    assert n % tm == 0 and seq % tm == 0 and tm % POOL_HALO == 0
    p_col = 3 * attn_width // GROUP_WIDTH
    gate_w = d // 2
    gate_col = (3 * attn_width + pool_width) // gate_w
    assert (3 * attn_width + pool_width) % gate_w == 0

    row_blk = lambda c: pl.BlockSpec((tm, GROUP_WIDTH), lambda i: (i, c))
    gate_blk = lambda c: pl.BlockSpec((tm, gate_w), lambda i: (i, gate_col + c))
    halo_blk = pl.BlockSpec((POOL_HALO, GROUP_WIDTH),
                            lambda i: (jnp.maximum(i * (tm // POOL_HALO) - 1, 0), p_col))
    full = lambda a: pl.BlockSpec(a.shape, lambda i: (0,) * a.ndim)
    weights = (w_pool, pool_scale, w_attn_out, w_pool_out, w_out, gain)

    act = 2 * _nbytes((tm, GROUP_WIDTH), F32) + 4 * _nbytes((tm, gate_w), F32) + 2 * _nbytes((tm, d), F32)
    wbytes = sum(_nbytes(a.shape, a.dtype) for a in weights)
    limit = _vmem_limit(act + wbytes, 0, 8 * _nbytes((tm, d), F32))
    return pl.pallas_call(
        functools.partial(_merge_kernel, tm=tm, tiles_per_seq=seq // tm),
        grid=(n // tm,),
        in_specs=[row_blk(0), row_blk(p_col), halo_blk] + [gate_blk(c) for c in range(4)]
                 + [pl.BlockSpec((tm, d), lambda i: (i, 0))] + [full(a) for a in weights],
        out_specs=pl.BlockSpec((tm, d), lambda i: (i, 0)),
        out_shape=jax.ShapeDtypeStruct((n, d), F32),
        compiler_params=pltpu.CompilerParams(dimension_semantics=("arbitrary",), vmem_limit_bytes=limit),
        name="mixer_merge",
    )(o_attn, z, z, z, z, z, z, h, *weights)


def _unpack_bf16_pair(words):
    lo = lax.bitcast_convert_type(words << 16, F32)
    hi = lax.bitcast_convert_type(words & jnp.uint32(0xFFFF0000), F32)
    return jnp.concatenate([lo, hi], axis=1).astype(BF16)


def _swiglu_accumulate(u_scr, wg_ref, wu_ref, wd_ref, acc_ref, rows=slice(None)):
    u = u_scr[rows, :]
    gate = jnp.dot(u, wg_ref[...].astype(BF16), preferred_element_type=F32)
    up = jnp.dot(u, wu_ref[...].astype(BF16), preferred_element_type=F32)
    hidden = (gate * jax.nn.sigmoid(gate) * up).astype(BF16)
    acc_ref[rows, :] += jnp.dot(hidden, wd_ref[...].astype(BF16), preferred_element_type=F32)


def _ffn_vmem_limit(tm, d, tf, x_bytes, scratch_bytes):
    return _vmem_limit(
        x_bytes + 3 * _nbytes((d, tf), F32) + _nbytes((tm, d), F32),
        _nbytes((tm, d), BF16) + scratch_bytes,
        3 * _nbytes((d, tf), BF16) + 4 * _nbytes((tm, tf), F32) + 2 * _nbytes((tm, d), F32))


def _dense_ffn_kernel(h_ref, gin_ref, wg_ref, wu_ref, wd_ref, gout_ref, out_ref, u_scr):
    j = pl.program_id(1)

    @pl.when(j == 0)
    def _():
        def norm_rows(rows):
            u_scr[rows, :] = _rms_norm_f32(h_ref[rows, :], gin_ref[...]).astype(BF16)
            out_ref[rows, :] = jnp.zeros((ROW_CHUNK, out_ref.shape[1]), F32)

        _for_row_chunks(u_scr.shape[0], norm_rows)

    _swiglu_accumulate(u_scr, wg_ref, wu_ref, wd_ref, out_ref)

    @pl.when(j == pl.num_programs(1) - 1)
    def _():
        def residual_rows(rows):
            out_ref[rows, :] = h_ref[rows, :] + _rms_norm_f32(out_ref[rows, :], gout_ref[...])

        _for_row_chunks(u_scr.shape[0], residual_rows)


def _dense_ffn(h, gain_in, gain_out, w_gate, w_up, w_down, *, tm=1024, tf=256):
    n, d = h.shape
    ff = w_gate.shape[1]
    assert n % tm == 0 and ff % tf == 0
    row_blk = pl.BlockSpec((tm, d), lambda i, j: (i, 0))
    gain_blk = pl.BlockSpec((1, d), lambda i, j: (0, 0))
    up_blk = pl.BlockSpec((d, tf), lambda i, j: (0, j))
    return pl.pallas_call(
        _dense_ffn_kernel,
        grid=(n // tm, ff // tf),
        in_specs=[row_blk, gain_blk, up_blk, up_blk, pl.BlockSpec((tf, d), lambda i, j: (j, 0)), gain_blk],
        out_specs=row_blk,
        out_shape=jax.ShapeDtypeStruct((n, d), F32),
        scratch_shapes=[pltpu.VMEM((tm, d), BF16)],
        compiler_params=pltpu.CompilerParams(
            dimension_semantics=("arbitrary", "arbitrary"),
            vmem_limit_bytes=_ffn_vmem_limit(tm, d, tf, _nbytes((tm, d), F32), 0)),
        name="swiglu_dense",
    )(h, gain_in, w_gate, w_up, w_down, gain_out)


def _expert_ffn_kernel(tile_expert_ref, tile_rows_ref, src_ref, x_hbm, wg_ref, wu_ref, wd_ref, out_ref,
                       u_scr, stage, sem):
    del tile_expert_ref
    i, j = pl.program_id(0), pl.program_id(1)
    tm = stage.shape[0]
    granule = EXPERT_ROW_GRANULE
    n_chunks = tile_rows_ref[i] // granule

    @pl.when(j == 0)
    def _():
        def zero_rows(rows):
            out_ref[rows, :] = jnp.zeros((ROW_CHUNK, out_ref.shape[1]), F32)

        _for_row_chunks(tm, zero_rows)

        def gather_chunk(c, carry):
            base = c * granule

            def start(r, inner):
                pltpu.make_async_copy(x_hbm.at[pl.ds(src_ref[i * tm + base + r], 1), :],
                                      stage.at[pl.ds(base + r, 1), :], sem).start()
                return inner

            return lax.fori_loop(0, granule, start, carry, unroll=8)

        lax.fori_loop(0, n_chunks, gather_chunk, 0)

        def wait_chunk(c, carry):
            pltpu.make_async_copy(x_hbm.at[pl.ds(0, granule), :], stage.at[pl.ds(0, granule), :], sem).wait()
            return carry

        lax.fori_loop(0, n_chunks, wait_chunk, 0)

        def unpack_chunk(c, carry):
            rows = pl.ds(pl.multiple_of(c * granule, granule), granule)
            u_scr[rows, :] = _unpack_bf16_pair(stage[rows, :])
            return carry

        lax.fori_loop(0, n_chunks, unpack_chunk, 0)

    for k in range(1, tm // granule + 1):
        @pl.when(n_chunks == k)
        def _(k=k):
            _swiglu_accumulate(u_scr, wg_ref, wu_ref, wd_ref, out_ref, slice(0, k * granule))


def _expert_ffn(xp, src, tile_expert, tile_rows, w_gate, w_up, w_down, *, tm, tf):
    rows = src.shape[0]
    n_exp, d, ff = w_gate.shape
    assert rows % tm == 0 and tm % ROW_CHUNK == 0 and tm % EXPERT_ROW_GRANULE == 0
    assert ff % tf == 0 and xp.shape[1] == d // 2
    nj = ff // tf

    def ff_tile(i, j, tr):
        return jnp.where(tr[i] > 0, j, nj - 1)

    up_blk = pl.BlockSpec((None, d, tf), lambda i, j, te, tr, src: (te[i], 0, ff_tile(i, j, tr)))
    down_blk = pl.BlockSpec((None, tf, d), lambda i, j, te, tr, src: (te[i], ff_tile(i, j, tr), 0))
    return pl.pallas_call(
        _expert_ffn_kernel,
        grid_spec=pltpu.PrefetchScalarGridSpec(
            num_scalar_prefetch=3,
            grid=(rows // tm, nj),
            in_specs=[pl.BlockSpec(memory_space=pl.ANY), up_blk, up_blk, down_blk],
            out_specs=pl.BlockSpec((tm, d), lambda i, j, te, tr, src: (i, 0)),
            scratch_shapes=[pltpu.VMEM((tm, d), BF16), pltpu.VMEM((tm, d // 2), jnp.uint32),
                            pltpu.SemaphoreType.DMA]),
        out_shape=jax.ShapeDtypeStruct((rows, d), F32),
        compiler_params=pltpu.CompilerParams(
            dimension_semantics=("arbitrary", "arbitrary"),
            vmem_limit_bytes=_ffn_vmem_limit(tm, d, tf, 0, _nbytes((tm, d // 2), jnp.uint32))),
        name="swiglu_experts",
    )(tile_expert, tile_rows, src, xp, w_gate, w_up, w_down)


def _router_kernel(h_ref, gain_ref, wrt_ref, xp_ref, idx_ref, wts_ref):
    u = _rms_norm_f32(h_ref[...], gain_ref[...])
    logits = lax.dot_general(wrt_ref[...], u, (((1,), (1,)), ((), ())),
                             precision=lax.Precision.HIGHEST, preferred_element_type=F32)
    n_exp = logits.shape[0]
    expert = lax.broadcasted_iota(jnp.int32, logits.shape, 0)
    v1 = jnp.max(logits, axis=0, keepdims=True)
    i1 = jnp.min(jnp.where(logits == v1, expert, n_exp), axis=0, keepdims=True)
    rest = jnp.where(expert == i1, -jnp.inf, logits)
    v2 = jnp.max(rest, axis=0, keepdims=True)
    i2 = jnp.min(jnp.where(rest == v2, expert, n_exp), axis=0, keepdims=True)
    e2 = jnp.exp(v2 - v1)
    idx_ref[...] = jnp.concatenate([i1, i2], axis=0)
    wts_ref[...] = jnp.concatenate([1.0 / (1.0 + e2), e2 / (1.0 + e2)], axis=0)

    bits = lax.bitcast_convert_type(u.astype(BF16).astype(F32), jnp.uint32)
    half = bits.shape[1] // 2
    xp_ref[...] = (bits[:, :half] >> 16) | bits[:, half:]


def _router(h, gain, w_router_t, *, tm=512):
    n, d = h.shape
    n_exp = w_router_t.shape[0]
    assert n % tm == 0
    limit = _vmem_limit(_nbytes((tm, d), F32) + _nbytes((tm, d // 2), jnp.uint32) + _nbytes((n_exp, d), F32),
                        0, 6 * _nbytes((tm, d), F32))
    return pl.pallas_call(
        _router_kernel,
        grid=(n // tm,),
        in_specs=[pl.BlockSpec((tm, d), lambda i: (i, 0)),
                  pl.BlockSpec((1, d), lambda i: (0, 0)),
                  pl.BlockSpec((n_exp, d), lambda i: (0, 0))],
        out_specs=[pl.BlockSpec((tm, d // 2), lambda i: (i, 0)),
                   pl.BlockSpec((TOP_K, tm), lambda i: (0, i)),
                   pl.BlockSpec((TOP_K, tm), lambda i: (0, i))],
        out_shape=[jax.ShapeDtypeStruct((n, d // 2), jnp.uint32),
                   jax.ShapeDtypeStruct((TOP_K, n), jnp.int32),
                   jax.ShapeDtypeStruct((TOP_K, n), F32)],
        compiler_params=pltpu.CompilerParams(dimension_semantics=("arbitrary",), vmem_limit_bytes=limit),
        name="moe_router",
    )(h, gain, w_router_t)


def _combine_kernel(pos_ref, y_hbm, wts_ref, h_ref, gain_ref, out_ref, bufs, sems, *, tm):
    i = pl.program_id(0)

    def start_tile(tile, slot):
        def start(r, carry):
            for k in range(TOP_K):
                pltpu.make_async_copy(y_hbm.at[pl.ds(pos_ref[TOP_K * (tile * tm + r) + k], 1), :],
                                      bufs.at[slot, k, pl.ds(r, 1), :], sems.at[slot, k]).start()
            return carry

        lax.fori_loop(0, tm, start, 0, unroll=8)

    @pl.when(i == 0)
    def _():
        start_tile(0, 0)

    @pl.when(i + 1 < pl.num_programs(0))
    def _():
        start_tile(i + 1, (i + 1) % 2)

    slot = i % 2
    for k in range(TOP_K):
        pltpu.make_async_copy(y_hbm.at[pl.ds(0, tm), :], bufs.at[slot, k], sems.at[slot, k]).wait()
    w = wts_ref[...]
    y = w[:, 0:1] * bufs[slot, 0] + w[:, 1:2] * bufs[slot, 1]
    out_ref[...] = h_ref[...] + _rms_norm_f32(y, gain_ref[...])


def _combine(y_rows, pos, wts, h, gain, *, tm=256):
    n, d = h.shape
    assert n % tm == 0
    limit = _vmem_limit(2 * _nbytes((tm, d), F32) + _nbytes((tm, 128), F32),
                        2 * TOP_K * _nbytes((tm, d), F32), 4 * _nbytes((tm, d), F32))
    return pl.pallas_call(
        functools.partial(_combine_kernel, tm=tm),
        grid_spec=pltpu.PrefetchScalarGridSpec(
            num_scalar_prefetch=1,
            grid=(n // tm,),
            in_specs=[pl.BlockSpec(memory_space=pl.ANY),
                      pl.BlockSpec((tm, TOP_K), lambda i, pos: (i, 0)),
                      pl.BlockSpec((tm, d), lambda i, pos: (i, 0)),
                      pl.BlockSpec((1, d), lambda i, pos: (0, 0))],
            out_specs=pl.BlockSpec((tm, d), lambda i, pos: (i, 0)),
            scratch_shapes=[pltpu.VMEM((2, TOP_K, tm, d), F32), pltpu.SemaphoreType.DMA((2, TOP_K))]),
        out_shape=jax.ShapeDtypeStruct((n, d), F32),
        compiler_params=pltpu.CompilerParams(dimension_semantics=("arbitrary",), vmem_limit_bytes=limit),
        name="moe_combine",
    )(pos, y_rows, wts, h, gain)


def _dispatch_plan(top_idx, n_exp, tm):
    n = top_idx.shape[1]
    n_assign = n * TOP_K
    n_tiles = n_assign // tm + n_exp
    expert = top_idx.T.reshape(n_assign)
    onehot = (expert[:, None] == jnp.arange(n_exp, dtype=jnp.int32)[None, :]).astype(jnp.int32)
    rank = jnp.sum((jnp.cumsum(onehot, axis=0) - onehot) * onehot, axis=1)
    counts = jnp.sum(onehot, axis=0)
    tiles_per_expert = (counts + tm - 1) // tm
    tile_ends = jnp.cumsum(tiles_per_expert)
    tile_starts = tile_ends - tiles_per_expert
    pos = tile_starts[expert] * tm + rank
    src = jnp.zeros((n_tiles * tm,), jnp.int32).at[pos].set(jnp.arange(n_assign, dtype=jnp.int32) // TOP_K)
    tile = jnp.arange(n_tiles, dtype=jnp.int32)
    tile_expert = jnp.sum((tile[:, None] >= tile_ends[None, :]).astype(jnp.int32), axis=1)
    used = tile < tile_ends[-1]
    tile_expert = jnp.where(used, tile_expert, tile_expert[tile_ends[-1] - 1])
    granule = EXPERT_ROW_GRANULE
    rows_left = (counts[tile_expert] + granule - 1) // granule * granule - (tile - tile_starts[tile_expert]) * tm
    tile_rows = jnp.where(used, jnp.clip(rows_left, 0, tm), 0)
    return pos.astype(jnp.int32), src, tile_expert.astype(jnp.int32), tile_rows.astype(jnp.int32)


def _moe(h, gain_in, gain_out, w_router, w_gate, w_up, w_down, *, tm=1280, tf=512):
    n_exp = w_router.shape[1]
    xp, top_idx, top_w = _router(h, gain_in, w_router.T)
    pos, src, tile_expert, tile_rows = _dispatch_plan(top_idx, n_exp, tm)
    y_rows = _expert_ffn(xp, src, tile_expert, tile_rows, w_gate, w_up, w_down, tm=tm, tf=tf)
    return _combine(y_rows, pos, top_w.T, h, gain_out)


def kernel(x, w_in, rel_bias, w_pool, pool_scale, w_attn_out, w_pool_out, w_out, norm_gains,
           dense_w_gate, dense_w_up, dense_w_down, moe_w_router, moe_w_gate, moe_w_up, moe_w_down):
    batch, seq, d = x.shape
    depth = w_in.shape[0]
    attn_width = N_ATTN_GROUPS * GROUP_WIDTH
    h = x.reshape(batch * seq, d)
    biases = [_band_bias(rel_bias[:, g * HEADS_PER_GROUP:(g + 1) * HEADS_PER_GROUP], window, dilation)
              for g, (window, dilation) in enumerate(DILATION_PATTERNS)]
    for layer in range(depth):
        gains = norm_gains[layer].reshape(4, 1, d)
        z = _in_proj(h, gains[0], w_in, layer)
        o_attn = _attention(z, biases, [dilation for _, dilation in DILATION_PATTERNS],
                            batch=batch, seq=seq, attn_width=attn_width)
        h = _merge(h, z, o_attn, w_pool[layer].astype(BF16), pool_scale[layer].reshape(1, -1),
                   w_attn_out[layer].astype(BF16), w_pool_out[layer].astype(BF16), w_out[layer].astype(BF16),
                   gains[1], seq=seq, attn_width=attn_width)
        j = layer // 2
        if layer % 2 == 0:
            h = _dense_ffn(h, gains[2], gains[3], dense_w_gate[j], dense_w_up[j], dense_w_down[j])
        else:
            h = _moe(h, gains[2], gains[3], moe_w_router[j], moe_w_gate[j], moe_w_up[j], moe_w_down[j])
    return h.reshape(batch, seq, d)
```

```python
import functools
import math

import jax
import jax.numpy as jnp
from jax import lax
from jax.experimental import pallas as pl
from jax.experimental.pallas import tpu as pltpu

F32 = jnp.float32
BF16 = jnp.bfloat16

RMS_EPS = 1e-6
HEAD_DIM = 128
Q_BLOCK = 128
DILATION_PATTERNS = ((128, 1), (512, 4), (2048, 16))
N_ATTN_GROUPS = len(DILATION_PATTERNS)
HEADS_PER_GROUP = 4
GROUP_WIDTH = HEADS_PER_GROUP * HEAD_DIM
POOL_SIZES = (2, 4, 8, 16)
POOL_GROUP_WIDTH = 128
POOL_HALO = 16
N_BUCKETS = 32
MAX_DISTANCE = 2048
TOP_K = 2

V7X_VMEM_BYTES = 64 * 1024 * 1024


def _vmem_limit(pipelined_bytes, resident_bytes, temp_bytes):
    return min(2 * pipelined_bytes + resident_bytes + temp_bytes, V7X_VMEM_BYTES)


def _nbytes(shape, dtype):
    return math.prod(shape) * jnp.dtype(dtype).itemsize


def _rms_norm_f32(x, gain):
    ms = jnp.mean(x * x, axis=-1, keepdims=True)
    return x * lax.rsqrt(ms + RMS_EPS) * gain


ROW_CHUNK = 256
EXPERT_ROW_GRANULE = 128
DMA_ISSUE_UNROLL = 16


def _for_row_chunks(n_rows, body):
    assert n_rows % ROW_CHUNK == 0

    def step(c, carry):
        body(pl.ds(pl.multiple_of(c * ROW_CHUNK, ROW_CHUNK), ROW_CHUNK))
        return carry

    lax.fori_loop(0, n_rows // ROW_CHUNK, step, 0)


def _inproj_kernel(h_ref, g_ref, w_ref, z_ref, u_scr):
    @pl.when(pl.program_id(1) == 0)
    def _():
        def norm_rows(rows):
            u_scr[rows, :] = _rms_norm_f32(h_ref[rows, :], g_ref[...]).astype(BF16)

        _for_row_chunks(u_scr.shape[0], norm_rows)

    z_ref[...] = jnp.dot(u_scr[...], w_ref[...].astype(BF16), preferred_element_type=F32)


def _in_proj(h, gain, w_all, layer, *, tm=2048, tn=512):
    n, d = h.shape
    width = w_all.shape[2]
    assert n % tm == 0 and width % tn == 0
    limit = _vmem_limit(
        _nbytes((d, tn), F32) + _nbytes((tm, tn), F32),
        _nbytes((tm, d), F32) + _nbytes((tm, d), BF16),
        _nbytes((tm, d), F32) + _nbytes((d, tn), BF16) + _nbytes((tm, tn), F32))
    return pl.pallas_call(
        _inproj_kernel,
        grid=(n // tm, width // tn),
        in_specs=[pl.BlockSpec((tm, d), lambda i, j: (i, 0), pipeline_mode=pl.Buffered(1)),
                  pl.BlockSpec((1, d), lambda i, j: (0, 0)),
                  pl.BlockSpec((None, d, tn), lambda i, j: (layer, 0, j))],
        out_specs=pl.BlockSpec((tm, tn), lambda i, j: (i, j)),
        out_shape=jax.ShapeDtypeStruct((n, width), F32),
        scratch_shapes=[pltpu.VMEM((tm, d), BF16)],
        compiler_params=pltpu.CompilerParams(dimension_semantics=("arbitrary", "arbitrary"),
                                             vmem_limit_bytes=limit),
        name="in_proj",
    )(h, gain, w_all)


def _group_attention_into(q_ref, k_ref, v_ref, kh_ref, vh_ref, bias_ref, o_scr, l_scr, *, dilation, scale):
    first_chunk = pl.program_id(1) == 0
    n_sub = q_ref.shape[1] // (dilation * Q_BLOCK)

    def stream_rows(start):
        if dilation == 1:
            return pl.ds(start, Q_BLOCK)
        return pl.ds(start, Q_BLOCK, stride=dilation)

    for r in range(dilation):
        k_prev = kh_ref[0, stream_rows(r), :].astype(BF16)
        v_prev = vh_ref[0, stream_rows(r), :].astype(BF16)
        for n in range(n_sub):
            rows = stream_rows(r + n * Q_BLOCK * dilation)
            q = q_ref[0, rows, :].astype(BF16)
            k_cur = k_ref[0, rows, :].astype(BF16)
            v_cur = v_ref[0, rows, :].astype(BF16)
            kc = jnp.concatenate([k_prev, k_cur], axis=0)
            vc = jnp.concatenate([v_prev, v_cur], axis=0)
            s = lax.dot_general(q, kc, (((1,), (1,)), ((), ())), preferred_element_type=F32)
            s = s * scale + bias_ref[0]
            if n == 0:
                key_col = lax.broadcasted_iota(jnp.int32, s.shape, 1)
                s = jnp.where(jnp.logical_and(first_chunk, key_col < Q_BLOCK), -jnp.inf, s)
            m = jnp.max(s, axis=-1, keepdims=True)
            p = jnp.exp(s - m)
            den = jnp.sum(p, axis=-1, keepdims=True)
            o_scr[rows, :] = jnp.dot(p.astype(BF16), vc, preferred_element_type=F32) / den
            l_scr[rows, :] = jnp.broadcast_to(m + jnp.log(den), (Q_BLOCK, HEAD_DIM))
            k_prev, v_prev = k_cur, v_cur


def _attn_kernel(*refs, dilations, scale):
    n_groups = len(dilations)
    group_refs = [refs[6 * g:6 * (g + 1)] for g in range(n_groups)]
    out_ref = refs[6 * n_groups]
    scratch = refs[6 * n_groups + 1:]
    o_scrs, l_scrs = scratch[:n_groups], scratch[n_groups:]
    for g, dilation in enumerate(dilations):
        _group_attention_into(*group_refs[g], o_scrs[g], l_scrs[g], dilation=dilation, scale=scale)

    def mix_rows(rows):
        lses = [l[rows, :] for l in l_scrs]
        m = functools.reduce(jnp.maximum, lses)
        es = [jnp.exp(l - m) for l in lses]
        num = sum(e * o[rows, :] for e, o in zip(es, o_scrs))
        out_ref[0, rows, :] = num / sum(es)

    _for_row_chunks(out_ref.shape[1], mix_rows)


def _t5_causal_bucket(dist):
    max_exact = N_BUCKETS // 2
    df = jnp.maximum(dist, 1).astype(F32)
    large = max_exact + (jnp.log(df / max_exact) / math.log(MAX_DISTANCE / max_exact)
                         * (N_BUCKETS - max_exact)).astype(jnp.int32)
    large = jnp.minimum(large, N_BUCKETS - 1)
    return jnp.where(dist < max_exact, dist, large)


def _band_bias(bias_table, window, dilation):
    span = window // dilation
    qi = jnp.arange(Q_BLOCK)[:, None]
    kj = jnp.arange(2 * Q_BLOCK)[None, :]
    step = qi + Q_BLOCK - kj
    valid = (step >= 0) & (step <= span)
    bucket = _t5_causal_bucket(jnp.clip(step, 0, span) * dilation)
    onehot = (bucket[:, :, None] == jnp.arange(N_BUCKETS)[None, None, :]).astype(F32)
    bias = jnp.einsum("qkb,bh->hqk", onehot, bias_table, precision=lax.Precision.HIGHEST)
    return jnp.where(valid[None], bias, -jnp.inf)


def _attention(z, biases, dilations, *, batch, seq, attn_width, chunk=2048):
    assert seq % chunk == 0 and attn_width % HEAD_DIM == 0
    zv = z.reshape(batch, seq, z.shape[1])
    blk = _nbytes((chunk, HEAD_DIM), F32)
    in_specs, operands, pipelined = [], [], blk
    for group, (dilation, bias) in enumerate(zip(dilations, biases)):
        hist = Q_BLOCK * dilation
        assert chunk % hist == 0
        q_col = group * HEADS_PER_GROUP
        k_col = q_col + attn_width // HEAD_DIM
        v_col = k_col + attn_width // HEAD_DIM

        def cur(col):
            return pl.BlockSpec((1, chunk, HEAD_DIM), lambda b, c, hh, col=col: (b, c, col + hh))

        def prev(col, hist=hist):
            return pl.BlockSpec(
                (1, hist, HEAD_DIM),
                lambda b, c, hh, col=col, hist=hist: (b, jnp.maximum(c * (chunk // hist) - 1, 0), col + hh))

        in_specs += [cur(q_col), cur(k_col), cur(v_col), prev(k_col), prev(v_col),
                     pl.BlockSpec((1,) + bias.shape[1:], lambda b, c, hh: (hh, 0, 0))]
        operands += [zv, zv, zv, zv, zv, bias]
        pipelined += 3 * blk + 2 * _nbytes((hist, HEAD_DIM), F32) + _nbytes(bias.shape[1:], F32)
    n_groups = len(dilations)
    o = pl.pallas_call(
        functools.partial(_attn_kernel, dilations=tuple(dilations), scale=HEAD_DIM ** -0.5),
        grid=(batch, seq // chunk, HEADS_PER_GROUP),
        in_specs=in_specs,
        out_specs=pl.BlockSpec((1, chunk, HEAD_DIM), lambda b, c, hh: (b, c, hh)),
        out_shape=jax.ShapeDtypeStruct((batch, seq, GROUP_WIDTH), F32),
        scratch_shapes=[pltpu.VMEM((chunk, HEAD_DIM), F32)] * (2 * n_groups),
        compiler_params=pltpu.CompilerParams(
            dimension_semantics=("arbitrary",) * 3,
            vmem_limit_bytes=_vmem_limit(pipelined, 2 * n_groups * blk, 4 * blk)),
        name="dilated_attention",
    )(*operands)
    return o.reshape(batch * seq, GROUP_WIDTH)


def _merge_kernel(oattn_ref, p_ref, pp_ref, ga0_ref, ga1_ref, gb0_ref, gb1_ref, h_ref,
                  wpool_ref, pscale_ref, wao_ref, wpo_ref, wout_ref, gain_ref, out_ref,
                  *, tm, tiles_per_seq):
    tile_in_seq = pl.program_id(0) % tiles_per_seq

    halo = jnp.where(tile_in_seq == 0, 0.0, pp_ref[...])
    xe = jnp.concatenate([halo, p_ref[...]], axis=0)
    t = tile_in_seq * tm + lax.broadcasted_iota(jnp.int32, (tm, 1), 0)
    pooled = []
    for g, size in enumerate(POOL_SIZES):
        a = xe[:, g * POOL_GROUP_WIDTH:(g + 1) * POOL_GROUP_WIDTH]
        s, shift = a, 1
        while shift < size:
            s = s + pltpu.roll(s, shift, axis=0)
            shift *= 2
        count = jnp.minimum(t + 1, size).astype(F32)
        y = (s[POOL_HALO:] / count - a[POOL_HALO:]).astype(BF16)
        pooled.append(jnp.dot(y, wpool_ref[g], preferred_element_type=F32))
    o_pool = jnp.concatenate(pooled, axis=1) * pscale_ref[...]

    attn_proj = jnp.dot(oattn_ref[...].astype(BF16), wao_ref[...], preferred_element_type=F32)
    pool_proj = jnp.dot(o_pool.astype(BF16), wpo_ref[...], preferred_element_type=F32)
    gate_a = jax.nn.sigmoid(jnp.concatenate([ga0_ref[...], ga1_ref[...]], axis=1))
    gate_b = jax.nn.sigmoid(jnp.concatenate([gb0_ref[...], gb1_ref[...]], axis=1))
    merged = gate_a * attn_proj + gate_b * pool_proj
    mix = jnp.dot(merged.astype(BF16), wout_ref[...], preferred_element_type=F32)
    out_ref[...] = h_ref[...] + _rms_norm_f32(mix, gain_ref[...])


def _merge(h, z, o_attn, w_pool, pool_scale, w_attn_out, w_pool_out, w_out, gain, *, seq, attn_width, tm=512):
    n, d = h.shape
    pool_width = w_pool.shape[0] * POOL_GROUP_WIDTH
    assert pool_width == GROUP_WIDTH
    assert n % tm == 0 and seq % tm == 0 and tm % POOL_HALO == 0
    p_col = 3 * attn_width // GROUP_WIDTH
    gate_w = d // 2
    gate_col = (3 * attn_width + pool_width) // gate_w
    assert (3 * attn_width + pool_width) % gate_w == 0

    row_blk = lambda c: pl.BlockSpec((tm, GROUP_WIDTH), lambda i: (i, c))
    gate_blk = lambda c: pl.BlockSpec((tm, gate_w), lambda i: (i, gate_col + c))
    halo_blk = pl.BlockSpec((POOL_HALO, GROUP_WIDTH),
                            lambda i: (jnp.maximum(i * (tm // POOL_HALO) - 1, 0), p_col))
    full = lambda a: pl.BlockSpec(a.shape, lambda i: (0,) * a.ndim)
    weights = (w_pool, pool_scale, w_attn_out, w_pool_out, w_out, gain)

    act = 2 * _nbytes((tm, GROUP_WIDTH), F32) + 4 * _nbytes((tm, gate_w), F32) + 2 * _nbytes((tm, d), F32)
    wbytes = sum(_nbytes(a.shape, a.dtype) for a in weights)
    limit = _vmem_limit(act + wbytes, 0, 8 * _nbytes((tm, d), F32))
    return pl.pallas_call(
        functools.partial(_merge_kernel, tm=tm, tiles_per_seq=seq // tm),
        grid=(n // tm,),
        in_specs=[row_blk(0), row_blk(p_col), halo_blk] + [gate_blk(c) for c in range(4)]
                 + [pl.BlockSpec((tm, d), lambda i: (i, 0))] + [full(a) for a in weights],
        out_specs=pl.BlockSpec((tm, d), lambda i: (i, 0)),
        out_shape=jax.ShapeDtypeStruct((n, d), F32),
        compiler_params=pltpu.CompilerParams(dimension_semantics=("arbitrary",), vmem_limit_bytes=limit),
        name="mixer_merge",
    )(o_attn, z, z, z, z, z, z, h, *weights)


def _unpack_bf16_pair(words):
    lo = lax.bitcast_convert_type(words << 16, F32)
    hi = lax.bitcast_convert_type(words & jnp.uint32(0xFFFF0000), F32)
    return jnp.concatenate([lo, hi], axis=1).astype(BF16)


def _swiglu_accumulate(u_scr, wg_ref, wu_ref, wd_ref, acc_ref, rows=slice(None)):
    u = u_scr[rows, :]
    gate = jnp.dot(u, wg_ref[...].astype(BF16), preferred_element_type=F32)
    up = jnp.dot(u, wu_ref[...].astype(BF16), preferred_element_type=F32)
    hidden = (gate * jax.nn.sigmoid(gate) * up).astype(BF16)
    acc_ref[rows, :] += jnp.dot(hidden, wd_ref[...].astype(BF16), preferred_element_type=F32)


def _ffn_vmem_limit(tm, d, tf, x_bytes, scratch_bytes):
    return _vmem_limit(
        x_bytes + 3 * _nbytes((d, tf), F32) + _nbytes((tm, d), F32),
        _nbytes((tm, d), BF16) + scratch_bytes,
        3 * _nbytes((d, tf), BF16) + 4 * _nbytes((tm, tf), F32) + 2 * _nbytes((tm, d), F32))


def _dense_ffn_kernel(h_ref, gin_ref, wg_ref, wu_ref, wd_ref, gout_ref, out_ref, u_scr):
    j = pl.program_id(1)

    @pl.when(j == 0)
    def _():
        def norm_rows(rows):
            u_scr[rows, :] = _rms_norm_f32(h_ref[rows, :], gin_ref[...]).astype(BF16)
            out_ref[rows, :] = jnp.zeros((ROW_CHUNK, out_ref.shape[1]), F32)

        _for_row_chunks(u_scr.shape[0], norm_rows)

    _swiglu_accumulate(u_scr, wg_ref, wu_ref, wd_ref, out_ref)

    @pl.when(j == pl.num_programs(1) - 1)
    def _():
        def residual_rows(rows):
            out_ref[rows, :] = h_ref[rows, :] + _rms_norm_f32(out_ref[rows, :], gout_ref[...])

        _for_row_chunks(u_scr.shape[0], residual_rows)


def _dense_ffn(h, gain_in, gain_out, w_gate, w_up, w_down, *, tm=1024, tf=256):
    n, d = h.shape
    ff = w_gate.shape[1]
    assert n % tm == 0 and ff % tf == 0
    row_blk = pl.BlockSpec((tm, d), lambda i, j: (i, 0))
    gain_blk = pl.BlockSpec((1, d), lambda i, j: (0, 0))
    up_blk = pl.BlockSpec((d, tf), lambda i, j: (0, j))
    return pl.pallas_call(
        _dense_ffn_kernel,
        grid=(n // tm, ff // tf),
        in_specs=[row_blk, gain_blk, up_blk, up_blk, pl.BlockSpec((tf, d), lambda i, j: (j, 0)), gain_blk],
        out_specs=row_blk,
        out_shape=jax.ShapeDtypeStruct((n, d), F32),
        scratch_shapes=[pltpu.VMEM((tm, d), BF16)],
        compiler_params=pltpu.CompilerParams(
            dimension_semantics=("arbitrary", "arbitrary"),
            vmem_limit_bytes=_ffn_vmem_limit(tm, d, tf, _nbytes((tm, d), F32), 0)),
        name="swiglu_dense",
    )(h, gain_in, w_gate, w_up, w_down, gain_out)


def _expert_ffn_kernel(tile_expert_ref, tile_rows_ref, src_ref, x_hbm, wg_ref, wu_ref, wd_ref, out_ref,
                       u_scr, stage, sem):
    del tile_expert_ref
    i, j = pl.program_id(0), pl.program_id(1)
    tm = stage.shape[0]
    granule = EXPERT_ROW_GRANULE
    n_chunks = tile_rows_ref[i] // granule

    @pl.when(j == 0)
    def _():
        def zero_rows(rows):
            out_ref[rows, :] = jnp.zeros((ROW_CHUNK, out_ref.shape[1]), F32)

        _for_row_chunks(tm, zero_rows)

        def gather_chunk(c, carry):
            base = c * granule

            def start(r, inner):
                pltpu.make_async_copy(x_hbm.at[pl.ds(src_ref[i * tm + base + r], 1), :],
                                      stage.at[pl.ds(base + r, 1), :], sem).start()
                return inner

            return lax.fori_loop(0, granule, start, carry, unroll=DMA_ISSUE_UNROLL)

        lax.fori_loop(0, n_chunks, gather_chunk, 0)

        def wait_chunk(c, carry):
            pltpu.make_async_copy(x_hbm.at[pl.ds(0, granule), :], stage.at[pl.ds(0, granule), :], sem).wait()
            return carry

        lax.fori_loop(0, n_chunks, wait_chunk, 0)

        def unpack_chunk(c, carry):
            rows = pl.ds(pl.multiple_of(c * granule, granule), granule)
            u_scr[rows, :] = _unpack_bf16_pair(stage[rows, :])
            return carry

        lax.fori_loop(0, n_chunks, unpack_chunk, 0)

    for k in range(1, tm // granule + 1):
        @pl.when(n_chunks == k)
        def _(k=k):
            _swiglu_accumulate(u_scr, wg_ref, wu_ref, wd_ref, out_ref, slice(0, k * granule))


def _expert_ffn(xp, src, tile_expert, tile_rows, w_gate, w_up, w_down, *, tm, tf):
    rows = src.shape[0]
    n_exp, d, ff = w_gate.shape
    assert rows % tm == 0 and tm % ROW_CHUNK == 0 and tm % EXPERT_ROW_GRANULE == 0
    assert ff % tf == 0 and xp.shape[1] == d // 2
    nj = ff // tf

    def ff_tile(i, j, tr):
        return jnp.where(tr[i] > 0, j, nj - 1)

    up_blk = pl.BlockSpec((None, d, tf), lambda i, j, te, tr, src: (te[i], 0, ff_tile(i, j, tr)))
    down_blk = pl.BlockSpec((None, tf, d), lambda i, j, te, tr, src: (te[i], ff_tile(i, j, tr), 0))
    return pl.pallas_call(
        _expert_ffn_kernel,
        grid_spec=pltpu.PrefetchScalarGridSpec(
            num_scalar_prefetch=3,
            grid=(rows // tm, nj),
            in_specs=[pl.BlockSpec(memory_space=pl.ANY), up_blk, up_blk, down_blk],
            out_specs=pl.BlockSpec((tm, d), lambda i, j, te, tr, src: (i, 0)),
            scratch_shapes=[pltpu.VMEM((tm, d), BF16), pltpu.VMEM((tm, d // 2), jnp.uint32),
                            pltpu.SemaphoreType.DMA]),
        out_shape=jax.ShapeDtypeStruct((rows, d), F32),
        compiler_params=pltpu.CompilerParams(
            dimension_semantics=("arbitrary", "arbitrary"),
            vmem_limit_bytes=_ffn_vmem_limit(tm, d, tf, 0, _nbytes((tm, d // 2), jnp.uint32))),
        name="swiglu_experts",
    )(tile_expert, tile_rows, src, xp, w_gate, w_up, w_down)


def _router_kernel(h_ref, gain_ref, wrt_ref, xp_ref, idx_ref, wts_ref):
    u = _rms_norm_f32(h_ref[...], gain_ref[...])
    logits = lax.dot_general(wrt_ref[...], u, (((1,), (1,)), ((), ())),
                             precision=lax.Precision.HIGHEST, preferred_element_type=F32)
    n_exp = logits.shape[0]
    expert = lax.broadcasted_iota(jnp.int32, logits.shape, 0)
    v1 = jnp.max(logits, axis=0, keepdims=True)
    i1 = jnp.min(jnp.where(logits == v1, expert, n_exp), axis=0, keepdims=True)
    rest = jnp.where(expert == i1, -jnp.inf, logits)
    v2 = jnp.max(rest, axis=0, keepdims=True)
    i2 = jnp.min(jnp.where(rest == v2, expert, n_exp), axis=0, keepdims=True)
    e2 = jnp.exp(v2 - v1)
    idx_ref[...] = jnp.concatenate([i1, i2], axis=0)
    wts_ref[...] = jnp.concatenate([1.0 / (1.0 + e2), e2 / (1.0 + e2)], axis=0)

    bits = lax.bitcast_convert_type(u.astype(BF16).astype(F32), jnp.uint32)
    half = bits.shape[1] // 2
    xp_ref[...] = (bits[:, :half] >> 16) | bits[:, half:]


def _router(h, gain, w_router_t, *, tm=512):
    n, d = h.shape
    n_exp = w_router_t.shape[0]
    assert n % tm == 0
    limit = _vmem_limit(_nbytes((tm, d), F32) + _nbytes((tm, d // 2), jnp.uint32) + _nbytes((n_exp, d), F32),
                        0, 6 * _nbytes((tm, d), F32))
    return pl.pallas_call(
        _router_kernel,
        grid=(n // tm,),
        in_specs=[pl.BlockSpec((tm, d), lambda i: (i, 0)),
                  pl.BlockSpec((1, d), lambda i: (0, 0)),
                  pl.BlockSpec((n_exp, d), lambda i: (0, 0))],
        out_specs=[pl.BlockSpec((tm, d // 2), lambda i: (i, 0)),
                   pl.BlockSpec((TOP_K, tm), lambda i: (0, i)),
                   pl.BlockSpec((TOP_K, tm), lambda i: (0, i))],
        out_shape=[jax.ShapeDtypeStruct((n, d // 2), jnp.uint32),
                   jax.ShapeDtypeStruct((TOP_K, n), jnp.int32),
                   jax.ShapeDtypeStruct((TOP_K, n), F32)],
        compiler_params=pltpu.CompilerParams(dimension_semantics=("arbitrary",), vmem_limit_bytes=limit),
        name="moe_router",
    )(h, gain, w_router_t)


def _combine_kernel(pos_ref, y_hbm, wts_ref, h_ref, gain_ref, out_ref, bufs, sems, *, tm):
    i = pl.program_id(0)

    def start_tile(tile, slot):
        def start(r, carry):
            for k in range(TOP_K):
                pltpu.make_async_copy(y_hbm.at[pl.ds(pos_ref[TOP_K * (tile * tm + r) + k], 1), :],
                                      bufs.at[slot, k, pl.ds(r, 1), :], sems.at[slot, k]).start()
            return carry

        lax.fori_loop(0, tm, start, 0, unroll=DMA_ISSUE_UNROLL)

    @pl.when(i == 0)
    def _():
        start_tile(0, 0)

    @pl.when(i + 1 < pl.num_programs(0))
    def _():
        start_tile(i + 1, (i + 1) % 2)

    slot = i % 2
    for k in range(TOP_K):
        pltpu.make_async_copy(y_hbm.at[pl.ds(0, tm), :], bufs.at[slot, k], sems.at[slot, k]).wait()
    w = wts_ref[...]
    y = w[:, 0:1] * bufs[slot, 0] + w[:, 1:2] * bufs[slot, 1]
    out_ref[...] = h_ref[...] + _rms_norm_f32(y, gain_ref[...])


def _combine(y_rows, pos, wts, h, gain, *, tm=512):
    n, d = h.shape
    assert n % tm == 0
    limit = _vmem_limit(2 * _nbytes((tm, d), F32) + _nbytes((tm, 128), F32),
                        2 * TOP_K * _nbytes((tm, d), F32), 4 * _nbytes((tm, d), F32))
    return pl.pallas_call(
        functools.partial(_combine_kernel, tm=tm),
        grid_spec=pltpu.PrefetchScalarGridSpec(
            num_scalar_prefetch=1,
            grid=(n // tm,),
            in_specs=[pl.BlockSpec(memory_space=pl.ANY),
                      pl.BlockSpec((tm, TOP_K), lambda i, pos: (i, 0)),
                      pl.BlockSpec((tm, d), lambda i, pos: (i, 0)),
                      pl.BlockSpec((1, d), lambda i, pos: (0, 0))],
            out_specs=pl.BlockSpec((tm, d), lambda i, pos: (i, 0)),
            scratch_shapes=[pltpu.VMEM((2, TOP_K, tm, d), F32), pltpu.SemaphoreType.DMA((2, TOP_K))]),
        out_shape=jax.ShapeDtypeStruct((n, d), F32),
        compiler_params=pltpu.CompilerParams(dimension_semantics=("arbitrary",), vmem_limit_bytes=limit),
        name="moe_combine",
    )(pos, y_rows, wts, h, gain)


def _dispatch_plan(top_idx, n_exp, tm):
    n = top_idx.shape[1]
    n_assign = n * TOP_K
    n_tiles = n_assign // tm + n_exp
    expert = top_idx.T.reshape(n_assign)
    onehot = (expert[:, None] == jnp.arange(n_exp, dtype=jnp.int32)[None, :]).astype(jnp.int32)
    rank = jnp.sum((jnp.cumsum(onehot, axis=0) - onehot) * onehot, axis=1)
    counts = jnp.sum(onehot, axis=0)
    tiles_per_expert = (counts + tm - 1) // tm
    tile_ends = jnp.cumsum(tiles_per_expert)
    tile_starts = tile_ends - tiles_per_expert
    pos = tile_starts[expert] * tm + rank
    src = jnp.zeros((n_tiles * tm,), jnp.int32).at[pos].set(jnp.arange(n_assign, dtype=jnp.int32) // TOP_K)
    tile = jnp.arange(n_tiles, dtype=jnp.int32)
    tile_expert = jnp.sum((tile[:, None] >= tile_ends[None, :]).astype(jnp.int32), axis=1)
    used = tile < tile_ends[-1]
    tile_expert = jnp.where(used, tile_expert, tile_expert[tile_ends[-1] - 1])
    granule = EXPERT_ROW_GRANULE
    rows_left = (counts[tile_expert] + granule - 1) // granule * granule - (tile - tile_starts[tile_expert]) * tm
    tile_rows = jnp.where(used, jnp.clip(rows_left, 0, tm), 0)
    return pos.astype(jnp.int32), src, tile_expert.astype(jnp.int32), tile_rows.astype(jnp.int32)


def _moe(h, gain_in, gain_out, w_router, w_gate, w_up, w_down, *, tm=1280, tf=512):
    n_exp = w_router.shape[1]
    xp, top_idx, top_w = _router(h, gain_in, w_router.T)
    pos, src, tile_expert, tile_rows = _dispatch_plan(top_idx, n_exp, tm)
    y_rows = _expert_ffn(xp, src, tile_expert, tile_rows, w_gate, w_up, w_down, tm=tm, tf=tf)
    return _combine(y_rows, pos, top_w.T, h, gain_out)


def kernel(x, w_in, rel_bias, w_pool, pool_scale, w_attn_out, w_pool_out, w_out, norm_gains,
           dense_w_gate, dense_w_up, dense_w_down, moe_w_router, moe_w_gate, moe_w_up, moe_w_down):
    batch, seq, d = x.shape
    depth = w_in.shape[0]
    attn_width = N_ATTN_GROUPS * GROUP_WIDTH
    h = x.reshape(batch * seq, d)
    biases = [_band_bias(rel_bias[:, g * HEADS_PER_GROUP:(g + 1) * HEADS_PER_GROUP], window, dilation)
              for g, (window, dilation) in enumerate(DILATION_PATTERNS)]
    for layer in range(depth):
        gains = norm_gains[layer].reshape(4, 1, d)
        z = _in_proj(h, gains[0], w_in, layer)
        o_attn = _attention(z, biases, [dilation for _, dilation in DILATION_PATTERNS],
                            batch=batch, seq=seq, attn_width=attn_width)
        h = _merge(h, z, o_attn, w_pool[layer].astype(BF16), pool_scale[layer].reshape(1, -1),
                   w_attn_out[layer].astype(BF16), w_pool_out[layer].astype(BF16), w_out[layer].astype(BF16),
                   gains[1], seq=seq, attn_width=attn_width)
        j = layer // 2
        if layer % 2 == 0:
            h = _dense_ffn(h, gains[2], gains[3], dense_w_gate[j], dense_w_up[j], dense_w_down[j])
        else:
            h = _moe(h, gains[2], gains[3], moe_w_router[j], moe_w_gate[j], moe_w_up[j], moe_w_down[j])
    return h.reshape(batch, seq, d)
```

```python
import functools
import math

import jax
import jax.numpy as jnp
from jax import lax
from jax.experimental import pallas as pl
from jax.experimental.pallas import tpu as pltpu

F32 = jnp.float32
BF16 = jnp.bfloat16

RMS_EPS = 1e-6
HEAD_DIM = 128
Q_BLOCK = 128
DILATION_PATTERNS = ((128, 1), (512, 4), (2048, 16))
N_ATTN_GROUPS = len(DILATION_PATTERNS)
HEADS_PER_GROUP = 4
GROUP_WIDTH = HEADS_PER_GROUP * HEAD_DIM
POOL_SIZES = (2, 4, 8, 16)
POOL_GROUP_WIDTH = 128
POOL_HALO = 16
N_BUCKETS = 32
MAX_DISTANCE = 2048
TOP_K = 2

V7X_VMEM_BYTES = 64 * 1024 * 1024


def _vmem_limit(pipelined_bytes, resident_bytes, temp_bytes):
    return min(2 * pipelined_bytes + resident_bytes + temp_bytes, V7X_VMEM_BYTES)


def _nbytes(shape, dtype):
    return math.prod(shape) * jnp.dtype(dtype).itemsize


def _rms_norm_f32(x, gain):
    ms = jnp.mean(x * x, axis=-1, keepdims=True)
    return x * lax.rsqrt(ms + RMS_EPS) * gain


ROW_CHUNK = 256
EXPERT_ROW_GRANULE = 128
DMA_ISSUE_UNROLL = 16


def _for_row_chunks(n_rows, body):
    assert n_rows % ROW_CHUNK == 0

    def step(c, carry):
        body(pl.ds(pl.multiple_of(c * ROW_CHUNK, ROW_CHUNK), ROW_CHUNK))
        return carry

    lax.fori_loop(0, n_rows // ROW_CHUNK, step, 0)


def _static_row_chunks(n_rows):
    assert n_rows % ROW_CHUNK == 0
    return [slice(c * ROW_CHUNK, (c + 1) * ROW_CHUNK) for c in range(n_rows // ROW_CHUNK)]


def _inproj_kernel(h_ref, g_ref, w_ref, z_ref, u_scr):
    j = pl.program_id(1)

    @pl.when(j == 0)
    def _():
        for rows in _static_row_chunks(u_scr.shape[0]):
            u = _rms_norm_f32(h_ref[rows, :], g_ref[...]).astype(BF16)
            u_scr[rows, :] = u
            z_ref[rows, :] = jnp.dot(u, w_ref[...].astype(BF16), preferred_element_type=F32)

    @pl.when(j > 0)
    def _():
        z_ref[...] = jnp.dot(u_scr[...], w_ref[...].astype(BF16), preferred_element_type=F32)


def _in_proj(h, gain, w_all, layer, *, tm=2048, tn=512):
    n, d = h.shape
    width = w_all.shape[2]
    assert n % tm == 0 and width % tn == 0
    limit = _vmem_limit(
        _nbytes((d, tn), F32) + _nbytes((tm, tn), F32),
        _nbytes((tm, d), F32) + _nbytes((tm, d), BF16),
        _nbytes((tm, d), F32) + _nbytes((d, tn), BF16) + _nbytes((tm, tn), F32))
    return pl.pallas_call(
        _inproj_kernel,
        grid=(n // tm, width // tn),
        in_specs=[pl.BlockSpec((tm, d), lambda i, j: (i, 0), pipeline_mode=pl.Buffered(1)),
                  pl.BlockSpec((1, d), lambda i, j: (0, 0)),
                  pl.BlockSpec((None, d, tn), lambda i, j: (layer, 0, j))],
        out_specs=pl.BlockSpec((tm, tn), lambda i, j: (i, j)),
        out_shape=jax.ShapeDtypeStruct((n, width), F32),
        scratch_shapes=[pltpu.VMEM((tm, d), BF16)],
        compiler_params=pltpu.CompilerParams(dimension_semantics=("arbitrary", "arbitrary"),
                                             vmem_limit_bytes=limit),
        name="in_proj",
    )(h, gain, w_all)


def _group_attention_into(q_ref, k_ref, v_ref, kh_ref, vh_ref, bias_ref, o_scr, l_scr, *, dilation, scale):
    first_chunk = pl.program_id(1) == 0
    n_sub = q_ref.shape[1] // (dilation * Q_BLOCK)

    def stream_rows(start):
        if dilation == 1:
            return pl.ds(start, Q_BLOCK)
        return pl.ds(start, Q_BLOCK, stride=dilation)

    for r in range(dilation):
        k_prev = kh_ref[0, stream_rows(r), :].astype(BF16)
        v_prev = vh_ref[0, stream_rows(r), :].astype(BF16)
        for n in range(n_sub):
            rows = stream_rows(r + n * Q_BLOCK * dilation)
            q = q_ref[0, rows, :].astype(BF16)
            k_cur = k_ref[0, rows, :].astype(BF16)
            v_cur = v_ref[0, rows, :].astype(BF16)
            kc = jnp.concatenate([k_prev, k_cur], axis=0)
            vc = jnp.concatenate([v_prev, v_cur], axis=0)
            s = lax.dot_general(q, kc, (((1,), (1,)), ((), ())), preferred_element_type=F32)
            s = s * scale + bias_ref[0]
            if n == 0:
                key_col = lax.broadcasted_iota(jnp.int32, s.shape, 1)
                s = jnp.where(jnp.logical_and(first_chunk, key_col < Q_BLOCK), -jnp.inf, s)
            m = jnp.max(s, axis=-1, keepdims=True)
            p = jnp.exp(s - m)
            den = jnp.sum(p, axis=-1, keepdims=True)
            o_scr[rows, :] = jnp.dot(p.astype(BF16), vc, preferred_element_type=F32) / den
            l_scr[rows, :] = jnp.broadcast_to(m + jnp.log(den), (Q_BLOCK, HEAD_DIM))
            k_prev, v_prev = k_cur, v_cur


def _attn_kernel(*refs, dilations, scale):
    n_groups = len(dilations)
    group_refs = [refs[6 * g:6 * (g + 1)] for g in range(n_groups)]
    out_ref = refs[6 * n_groups]
    scratch = refs[6 * n_groups + 1:]
    o_scrs, l_scrs = scratch[:n_groups], scratch[n_groups:]
    for g, dilation in enumerate(dilations):
        _group_attention_into(*group_refs[g], o_scrs[g], l_scrs[g], dilation=dilation, scale=scale)

    def mix_rows(rows):
        lses = [l[rows, :] for l in l_scrs]
        m = functools.reduce(jnp.maximum, lses)
        es = [jnp.exp(l - m) for l in lses]
        num = sum(e * o[rows, :] for e, o in zip(es, o_scrs))
        out_ref[0, rows, :] = num / sum(es)

    _for_row_chunks(out_ref.shape[1], mix_rows)


def _t5_causal_bucket(dist):
    max_exact = N_BUCKETS // 2
    df = jnp.maximum(dist, 1).astype(F32)
    large = max_exact + (jnp.log(df / max_exact) / math.log(MAX_DISTANCE / max_exact)
                         * (N_BUCKETS - max_exact)).astype(jnp.int32)
    large = jnp.minimum(large, N_BUCKETS - 1)
    return jnp.where(dist < max_exact, dist, large)


def _band_bias(bias_table, window, dilation):
    span = window // dilation
    qi = jnp.arange(Q_BLOCK)[:, None]
    kj = jnp.arange(2 * Q_BLOCK)[None, :]
    step = qi + Q_BLOCK - kj
    valid = (step >= 0) & (step <= span)
    bucket = _t5_causal_bucket(jnp.clip(step, 0, span) * dilation)
    onehot = (bucket[:, :, None] == jnp.arange(N_BUCKETS)[None, None, :]).astype(F32)
    bias = jnp.einsum("qkb,bh->hqk", onehot, bias_table, precision=lax.Precision.HIGHEST)
    return jnp.where(valid[None], bias, -jnp.inf)


def _attention(z, biases, dilations, *, batch, seq, attn_width, chunk=2048):
    assert seq % chunk == 0 and attn_width % HEAD_DIM == 0
    zv = z.reshape(batch, seq, z.shape[1])
    blk = _nbytes((chunk, HEAD_DIM), F32)
    in_specs, operands, pipelined = [], [], blk
    for group, (dilation, bias) in enumerate(zip(dilations, biases)):
        hist = Q_BLOCK * dilation
        assert chunk % hist == 0
        q_col = group * HEADS_PER_GROUP
        k_col = q_col + attn_width // HEAD_DIM
        v_col = k_col + attn_width // HEAD_DIM

        def cur(col):
            return pl.BlockSpec((1, chunk, HEAD_DIM), lambda b, c, hh, col=col: (b, c, col + hh))

        def prev(col, hist=hist):
            return pl.BlockSpec(
                (1, hist, HEAD_DIM),
                lambda b, c, hh, col=col, hist=hist: (b, jnp.maximum(c * (chunk // hist) - 1, 0), col + hh))

        in_specs += [cur(q_col), cur(k_col), cur(v_col), prev(k_col), prev(v_col),
                     pl.BlockSpec((1,) + bias.shape[1:], lambda b, c, hh: (hh, 0, 0))]
        operands += [zv, zv, zv, zv, zv, bias]
        pipelined += 3 * blk + 2 * _nbytes((hist, HEAD_DIM), F32) + _nbytes(bias.shape[1:], F32)
    n_groups = len(dilations)
    o = pl.pallas_call(
        functools.partial(_attn_kernel, dilations=tuple(dilations), scale=HEAD_DIM ** -0.5),
        grid=(batch, seq // chunk, HEADS_PER_GROUP),
        in_specs=in_specs,
        out_specs=pl.BlockSpec((1, chunk, HEAD_DIM), lambda b, c, hh: (b, c, hh)),
        out_shape=jax.ShapeDtypeStruct((batch, seq, GROUP_WIDTH), F32),
        scratch_shapes=[pltpu.VMEM((chunk, HEAD_DIM), F32)] * (2 * n_groups),
        compiler_params=pltpu.CompilerParams(
            dimension_semantics=("arbitrary",) * 3,
            vmem_limit_bytes=_vmem_limit(pipelined, 2 * n_groups * blk, 4 * blk)),
        name="dilated_attention",
    )(*operands)
    return o.reshape(batch * seq, GROUP_WIDTH)


def _merge_kernel(oattn_ref, p_ref, pp_ref, ga0_ref, ga1_ref, gb0_ref, gb1_ref, h_ref,
                  wpool_ref, pscale_ref, wao_ref, wpo_ref, wout_ref, gain_ref, out_ref,
                  *, tm, tiles_per_seq):
    tile_in_seq = pl.program_id(0) % tiles_per_seq

    halo = jnp.where(tile_in_seq == 0, 0.0, pp_ref[...])
    xe = jnp.concatenate([halo, p_ref[...]], axis=0)
    t = tile_in_seq * tm + lax.broadcasted_iota(jnp.int32, (tm, 1), 0)
    pooled = []
    for g, size in enumerate(POOL_SIZES):
        a = xe[:, g * POOL_GROUP_WIDTH:(g + 1) * POOL_GROUP_WIDTH]
        s, shift = a, 1
        while shift < size:
            s = s + pltpu.roll(s, shift, axis=0)
            shift *= 2
        count = jnp.minimum(t + 1, size).astype(F32)
        y = (s[POOL_HALO:] / count - a[POOL_HALO:]).astype(BF16)
        pooled.append(jnp.dot(y, wpool_ref[g], preferred_element_type=F32))
    o_pool = jnp.concatenate(pooled, axis=1) * pscale_ref[...]

    attn_proj = jnp.dot(oattn_ref[...].astype(BF16), wao_ref[...], preferred_element_type=F32)
    pool_proj = jnp.dot(o_pool.astype(BF16), wpo_ref[...], preferred_element_type=F32)
    gate_a = jax.nn.sigmoid(jnp.concatenate([ga0_ref[...], ga1_ref[...]], axis=1))
    gate_b = jax.nn.sigmoid(jnp.concatenate([gb0_ref[...], gb1_ref[...]], axis=1))
    merged = gate_a * attn_proj + gate_b * pool_proj
    mix = jnp.dot(merged.astype(BF16), wout_ref[...], preferred_element_type=F32)
    out_ref[...] = h_ref[...] + _rms_norm_f32(mix, gain_ref[...])


def _merge(h, z, o_attn, w_pool, pool_scale, w_attn_out, w_pool_out, w_out, gain, *, seq, attn_width, tm=512):
    n, d = h.shape
    pool_width = w_pool.shape[0] * POOL_GROUP_WIDTH
    assert pool_width == GROUP_WIDTH
    assert n % tm == 0 and seq % tm == 0 and tm % POOL_HALO == 0
    p_col = 3 * attn_width // GROUP_WIDTH
    gate_w = d // 2
    gate_col = (3 * attn_width + pool_width) // gate_w
    assert (3 * attn_width + pool_width) % gate_w == 0

    row_blk = lambda c: pl.BlockSpec((tm, GROUP_WIDTH), lambda i: (i, c))
    gate_blk = lambda c: pl.BlockSpec((tm, gate_w), lambda i: (i, gate_col + c))
    halo_blk = pl.BlockSpec((POOL_HALO, GROUP_WIDTH),
                            lambda i: (jnp.maximum(i * (tm // POOL_HALO) - 1, 0), p_col))
    full = lambda a: pl.BlockSpec(a.shape, lambda i: (0,) * a.ndim)
    weights = (w_pool, pool_scale, w_attn_out, w_pool_out, w_out, gain)

    act = 2 * _nbytes((tm, GROUP_WIDTH), F32) + 4 * _nbytes((tm, gate_w), F32) + 2 * _nbytes((tm, d), F32)
    wbytes = sum(_nbytes(a.shape, a.dtype) for a in weights)
    limit = _vmem_limit(act + wbytes, 0, 8 * _nbytes((tm, d), F32))
    return pl.pallas_call(
        functools.partial(_merge_kernel, tm=tm, tiles_per_seq=seq // tm),
        grid=(n // tm,),
        in_specs=[row_blk(0), row_blk(p_col), halo_blk] + [gate_blk(c) for c in range(4)]
                 + [pl.BlockSpec((tm, d), lambda i: (i, 0))] + [full(a) for a in weights],
        out_specs=pl.BlockSpec((tm, d), lambda i: (i, 0)),
        out_shape=jax.ShapeDtypeStruct((n, d), F32),
        compiler_params=pltpu.CompilerParams(dimension_semantics=("arbitrary",), vmem_limit_bytes=limit),
        name="mixer_merge",
    )(o_attn, z, z, z, z, z, z, h, *weights)


def _unpack_bf16_pair(words):
    lo = lax.bitcast_convert_type(words << 16, F32)
    hi = lax.bitcast_convert_type(words & jnp.uint32(0xFFFF0000), F32)
    return jnp.concatenate([lo, hi], axis=1).astype(BF16)


def _swiglu(u, wg_ref, wu_ref, wd_ref):
    gate = jnp.dot(u, wg_ref[...].astype(BF16), preferred_element_type=F32)
    up = jnp.dot(u, wu_ref[...].astype(BF16), preferred_element_type=F32)
    hidden = (gate * jax.nn.sigmoid(gate) * up).astype(BF16)
    return jnp.dot(hidden, wd_ref[...].astype(BF16), preferred_element_type=F32)


def _swiglu_accumulate(u_scr, wg_ref, wu_ref, wd_ref, acc_ref, rows=slice(None)):
    acc_ref[rows, :] += _swiglu(u_scr[rows, :], wg_ref, wu_ref, wd_ref)


def _ffn_vmem_limit(tm, d, tf, x_bytes, scratch_bytes):
    return _vmem_limit(
        x_bytes + 3 * _nbytes((d, tf), F32) + _nbytes((tm, d), F32),
        _nbytes((tm, d), BF16) + scratch_bytes,
        3 * _nbytes((d, tf), BF16) + 4 * _nbytes((tm, tf), F32) + 2 * _nbytes((tm, d), F32))


def _dense_ffn_kernel(h_ref, gin_ref, wg_ref, wu_ref, wd_ref, gout_ref, out_ref, u_scr):
    j = pl.program_id(1)
    last = pl.num_programs(1) - 1
    chunks = _static_row_chunks(u_scr.shape[0])

    @pl.when(j == 0)
    def _():
        for rows in chunks:
            u = _rms_norm_f32(h_ref[rows, :], gin_ref[...]).astype(BF16)
            u_scr[rows, :] = u
            out_ref[rows, :] = _swiglu(u, wg_ref, wu_ref, wd_ref)

    @pl.when(jnp.logical_and(j > 0, j < last))
    def _():
        _swiglu_accumulate(u_scr, wg_ref, wu_ref, wd_ref, out_ref)

    @pl.when(j == last)
    def _():
        for rows in chunks:
            f = out_ref[rows, :] + _swiglu(u_scr[rows, :], wg_ref, wu_ref, wd_ref)
            out_ref[rows, :] = h_ref[rows, :] + _rms_norm_f32(f, gout_ref[...])


def _dense_ffn(h, gain_in, gain_out, w_gate, w_up, w_down, *, tm=1024, tf=256):
    n, d = h.shape
    ff = w_gate.shape[1]
    assert n % tm == 0 and ff % tf == 0 and ff // tf >= 2
    row_blk = pl.BlockSpec((tm, d), lambda i, j: (i, 0))
    gain_blk = pl.BlockSpec((1, d), lambda i, j: (0, 0))
    up_blk = pl.BlockSpec((d, tf), lambda i, j: (0, j))
    return pl.pallas_call(
        _dense_ffn_kernel,
        grid=(n // tm, ff // tf),
        in_specs=[row_blk, gain_blk, up_blk, up_blk, pl.BlockSpec((tf, d), lambda i, j: (j, 0)), gain_blk],
        out_specs=row_blk,
        out_shape=jax.ShapeDtypeStruct((n, d), F32),
        scratch_shapes=[pltpu.VMEM((tm, d), BF16)],
        compiler_params=pltpu.CompilerParams(
            dimension_semantics=("arbitrary", "arbitrary"),
            vmem_limit_bytes=_ffn_vmem_limit(tm, d, tf, _nbytes((tm, d), F32), 0)),
        name="swiglu_dense",
    )(h, gain_in, w_gate, w_up, w_down, gain_out)


def _expert_ffn_kernel(tile_expert_ref, tile_rows_ref, src_ref, x_hbm, wg_ref, wu_ref, wd_ref, out_ref,
                       u_scr, stage, sem):
    del tile_expert_ref
    i, j = pl.program_id(0), pl.program_id(1)
    tm = stage.shape[0]
    granule = EXPERT_ROW_GRANULE
    n_chunks = tile_rows_ref[i] // granule

    @pl.when(j == 0)
    def _():
        def zero_rows(rows):
            out_ref[rows, :] = jnp.zeros((ROW_CHUNK, out_ref.shape[1]), F32)

        _for_row_chunks(tm, zero_rows)

        def gather_chunk(c, carry):
            base = c * granule

            def start(r, inner):
                pltpu.make_async_copy(x_hbm.at[pl.ds(src_ref[i * tm + base + r], 1), :],
                                      stage.at[pl.ds(base + r, 1), :], sem).start()
                return inner

            return lax.fori_loop(0, granule, start, carry, unroll=DMA_ISSUE_UNROLL)

        lax.fori_loop(0, n_chunks, gather_chunk, 0)

        def wait_chunk(c, carry):
            pltpu.make_async_copy(x_hbm.at[pl.ds(0, granule), :], stage.at[pl.ds(0, granule), :], sem).wait()
            return carry

        lax.fori_loop(0, n_chunks, wait_chunk, 0)

        def unpack_chunk(c, carry):
            rows = pl.ds(pl.multiple_of(c * granule, granule), granule)
            u_scr[rows, :] = _unpack_bf16_pair(stage[rows, :])
            return carry

        lax.fori_loop(0, n_chunks, unpack_chunk, 0)

    for k in range(1, tm // granule + 1):
        @pl.when(n_chunks == k)
        def _(k=k):
            _swiglu_accumulate(u_scr, wg_ref, wu_ref, wd_ref, out_ref, slice(0, k * granule))


def _expert_ffn(xp, src, tile_expert, tile_rows, w_gate, w_up, w_down, *, tm, tf):
    rows = src.shape[0]
    n_exp, d, ff = w_gate.shape
    assert rows % tm == 0 and tm % ROW_CHUNK == 0 and tm % EXPERT_ROW_GRANULE == 0
    assert ff % tf == 0 and xp.shape[1] == d // 2
    nj = ff // tf

    def ff_tile(i, j, tr):
        return jnp.where(tr[i] > 0, j, nj - 1)

    up_blk = pl.BlockSpec((None, d, tf), lambda i, j, te, tr, src: (te[i], 0, ff_tile(i, j, tr)))
    down_blk = pl.BlockSpec((None, tf, d), lambda i, j, te, tr, src: (te[i], ff_tile(i, j, tr), 0))
    return pl.pallas_call(
        _expert_ffn_kernel,
        grid_spec=pltpu.PrefetchScalarGridSpec(
            num_scalar_prefetch=3,
            grid=(rows // tm, nj),
            in_specs=[pl.BlockSpec(memory_space=pl.ANY), up_blk, up_blk, down_blk],
            out_specs=pl.BlockSpec((tm, d), lambda i, j, te, tr, src: (i, 0)),
            scratch_shapes=[pltpu.VMEM((tm, d), BF16), pltpu.VMEM((tm, d // 2), jnp.uint32),
                            pltpu.SemaphoreType.DMA]),
        out_shape=jax.ShapeDtypeStruct((rows, d), F32),
        compiler_params=pltpu.CompilerParams(
            dimension_semantics=("arbitrary", "arbitrary"),
            vmem_limit_bytes=_ffn_vmem_limit(tm, d, tf, 0, _nbytes((tm, d // 2), jnp.uint32))),
        name="swiglu_experts",
    )(tile_expert, tile_rows, src, xp, w_gate, w_up, w_down)


def _router_kernel(h_ref, gain_ref, wrt_ref, xp_ref, idx_ref, wts_ref):
    u = _rms_norm_f32(h_ref[...], gain_ref[...])
    logits = lax.dot_general(wrt_ref[...], u, (((1,), (1,)), ((), ())),
                             precision=lax.Precision.HIGHEST, preferred_element_type=F32)
    n_exp = logits.shape[0]
    expert = lax.broadcasted_iota(jnp.int32, logits.shape, 0)
    v1 = jnp.max(logits, axis=0, keepdims=True)
    i1 = jnp.min(jnp.where(logits == v1, expert, n_exp), axis=0, keepdims=True)
    rest = jnp.where(expert == i1, -jnp.inf, logits)
    v2 = jnp.max(rest, axis=0, keepdims=True)
    i2 = jnp.min(jnp.where(rest == v2, expert, n_exp), axis=0, keepdims=True)
    e2 = jnp.exp(v2 - v1)
    idx_ref[...] = jnp.concatenate([i1, i2], axis=0)
    wts_ref[...] = jnp.concatenate([1.0 / (1.0 + e2), e2 / (1.0 + e2)], axis=0)

    bits = lax.bitcast_convert_type(u.astype(BF16).astype(F32), jnp.uint32)
    half = bits.shape[1] // 2
    xp_ref[...] = (bits[:, :half] >> 16) | bits[:, half:]


def _router(h, gain, w_router_t, *, tm=512):
    n, d = h.shape
    n_exp = w_router_t.shape[0]
    assert n % tm == 0
    limit = _vmem_limit(_nbytes((tm, d), F32) + _nbytes((tm, d // 2), jnp.uint32) + _nbytes((n_exp, d), F32),
                        0, 6 * _nbytes((tm, d), F32))
    return pl.pallas_call(
        _router_kernel,
        grid=(n // tm,),
        in_specs=[pl.BlockSpec((tm, d), lambda i: (i, 0)),
                  pl.BlockSpec((1, d), lambda i: (0, 0)),
                  pl.BlockSpec((n_exp, d), lambda i: (0, 0))],
        out_specs=[pl.BlockSpec((tm, d // 2), lambda i: (i, 0)),
                   pl.BlockSpec((TOP_K, tm), lambda i: (0, i)),
                   pl.BlockSpec((TOP_K, tm), lambda i: (0, i))],
        out_shape=[jax.ShapeDtypeStruct((n, d // 2), jnp.uint32),
                   jax.ShapeDtypeStruct((TOP_K, n), jnp.int32),
                   jax.ShapeDtypeStruct((TOP_K, n), F32)],
        compiler_params=pltpu.CompilerParams(dimension_semantics=("arbitrary",), vmem_limit_bytes=limit),
        name="moe_router",
    )(h, gain, w_router_t)


def _combine_kernel(pos_ref, y_hbm, wts_ref, h_ref, gain_ref, out_ref, bufs, sems, *, tm):
    i = pl.program_id(0)

    def start_tile(tile, slot):
        def start(r, carry):
            for k in range(TOP_K):
                pltpu.make_async_copy(y_hbm.at[pl.ds(pos_ref[TOP_K * (tile * tm + r) + k], 1), :],
                                      bufs.at[slot, k, pl.ds(r, 1), :], sems.at[slot, k]).start()
            return carry

        lax.fori_loop(0, tm, start, 0, unroll=DMA_ISSUE_UNROLL)

    @pl.when(i == 0)
    def _():
        start_tile(0, 0)

    @pl.when(i + 1 < pl.num_programs(0))
    def _():
        start_tile(i + 1, (i + 1) % 2)

    slot = i % 2
    for k in range(TOP_K):
        pltpu.make_async_copy(y_hbm.at[pl.ds(0, tm), :], bufs.at[slot, k], sems.at[slot, k]).wait()
    w = wts_ref[...]
    y = w[:, 0:1] * bufs[slot, 0] + w[:, 1:2] * bufs[slot, 1]
    out_ref[...] = h_ref[...] + _rms_norm_f32(y, gain_ref[...])


def _combine(y_rows, pos, wts, h, gain, *, tm=512):
    n, d = h.shape
    assert n % tm == 0
    limit = _vmem_limit(2 * _nbytes((tm, d), F32) + _nbytes((tm, 128), F32),
                        2 * TOP_K * _nbytes((tm, d), F32), 4 * _nbytes((tm, d), F32))
    return pl.pallas_call(
        functools.partial(_combine_kernel, tm=tm),
        grid_spec=pltpu.PrefetchScalarGridSpec(
            num_scalar_prefetch=1,
            grid=(n // tm,),
            in_specs=[pl.BlockSpec(memory_space=pl.ANY),
                      pl.BlockSpec((tm, TOP_K), lambda i, pos: (i, 0)),
                      pl.BlockSpec((tm, d), lambda i, pos: (i, 0)),
                      pl.BlockSpec((1, d), lambda i, pos: (0, 0))],
            out_specs=pl.BlockSpec((tm, d), lambda i, pos: (i, 0)),
            scratch_shapes=[pltpu.VMEM((2, TOP_K, tm, d), F32), pltpu.SemaphoreType.DMA((2, TOP_K))]),
        out_shape=jax.ShapeDtypeStruct((n, d), F32),
        compiler_params=pltpu.CompilerParams(dimension_semantics=("arbitrary",), vmem_limit_bytes=limit),
        name="moe_combine",
    )(pos, y_rows, wts, h, gain)


def _dispatch_plan(top_idx, n_exp, tm):
    n = top_idx.shape[1]
    n_assign = n * TOP_K
    n_tiles = n_assign // tm + n_exp
    expert = top_idx.T.reshape(n_assign)
    onehot = (expert[:, None] == jnp.arange(n_exp, dtype=jnp.int32)[None, :]).astype(jnp.int32)
    rank = jnp.sum((jnp.cumsum(onehot, axis=0) - onehot) * onehot, axis=1)
    counts = jnp.sum(onehot, axis=0)
    tiles_per_expert = (counts + tm - 1) // tm
    tile_ends = jnp.cumsum(tiles_per_expert)
    tile_starts = tile_ends - tiles_per_expert
    pos = tile_starts[expert] * tm + rank
    src = jnp.zeros((n_tiles * tm,), jnp.int32).at[pos].set(jnp.arange(n_assign, dtype=jnp.int32) // TOP_K)
    tile = jnp.arange(n_tiles, dtype=jnp.int32)
    tile_expert = jnp.sum((tile[:, None] >= tile_ends[None, :]).astype(jnp.int32), axis=1)
    used = tile < tile_ends[-1]
    tile_expert = jnp.where(used, tile_expert, tile_expert[tile_ends[-1] - 1])
    granule = EXPERT_ROW_GRANULE
    rows_left = (counts[tile_expert] + granule - 1) // granule * granule - (tile - tile_starts[tile_expert]) * tm
    tile_rows = jnp.where(used, jnp.clip(rows_left, 0, tm), 0)
    return pos.astype(jnp.int32), src, tile_expert.astype(jnp.int32), tile_rows.astype(jnp.int32)


def _moe(h, gain_in, gain_out, w_router, w_gate, w_up, w_down, *, tm=1280, tf=512):
    n_exp = w_router.shape[1]
    xp, top_idx, top_w = _router(h, gain_in, w_router.T)
    pos, src, tile_expert, tile_rows = _dispatch_plan(top_idx, n_exp, tm)
    y_rows = _expert_ffn(xp, src, tile_expert, tile_rows, w_gate, w_up, w_down, tm=tm, tf=tf)
    return _combine(y_rows, pos, top_w.T, h, gain_out)


def kernel(x, w_in, rel_bias, w_pool, pool_scale, w_attn_out, w_pool_out, w_out, norm_gains,
           dense_w_gate, dense_w_up, dense_w_down, moe_w_router, moe_w_gate, moe_w_up, moe_w_down):
    batch, seq, d = x.shape
    depth = w_in.shape[0]
    attn_width = N_ATTN_GROUPS * GROUP_WIDTH
    h = x.reshape(batch * seq, d)
    biases = [_band_bias(rel_bias[:, g * HEADS_PER_GROUP:(g + 1) * HEADS_PER_GROUP], window, dilation)
              for g, (window, dilation) in enumerate(DILATION_PATTERNS)]
    for layer in range(depth):
        gains = norm_gains[layer].reshape(4, 1, d)
        z = _in_proj(h, gains[0], w_in, layer)
        o_attn = _attention(z, biases, [dilation for _, dilation in DILATION_PATTERNS],
                            batch=batch, seq=seq, attn_width=attn_width)
        h = _merge(h, z, o_attn, w_pool[layer].astype(BF16), pool_scale[layer].reshape(1, -1),
                   w_attn_out[layer].astype(BF16), w_pool_out[layer].astype(BF16), w_out[layer].astype(BF16),
                   gains[1], seq=seq, attn_width=attn_width)
        j = layer // 2
        if layer % 2 == 0:
            h = _dense_ffn(h, gains[2], gains[3], dense_w_gate[j], dense_w_up[j], dense_w_down[j])
        else:
            h = _moe(h, gains[2], gains[3], moe_w_router[j], moe_w_gate[j], moe_w_up[j], moe_w_down[j])
    return h.reshape(batch, seq, d)
```

```python
import functools
import math

import jax
import jax.numpy as jnp
from jax import lax
from jax.experimental import pallas as pl
from jax.experimental.pallas import tpu as pltpu

F32 = jnp.float32
BF16 = jnp.bfloat16

RMS_EPS = 1e-6
HEAD_DIM = 128
Q_BLOCK = 128
DILATION_PATTERNS = ((128, 1), (512, 4), (2048, 16))
N_ATTN_GROUPS = len(DILATION_PATTERNS)
HEADS_PER_GROUP = 4
GROUP_WIDTH = HEADS_PER_GROUP * HEAD_DIM
POOL_SIZES = (2, 4, 8, 16)
POOL_GROUP_WIDTH = 128
POOL_HALO = 16
N_BUCKETS = 32
MAX_DISTANCE = 2048
TOP_K = 2

V7X_VMEM_BYTES = 64 * 1024 * 1024


def _vmem_limit(pipelined_bytes, resident_bytes, temp_bytes):
    return min(2 * pipelined_bytes + resident_bytes + temp_bytes, V7X_VMEM_BYTES)


def _nbytes(shape, dtype):
    return math.prod(shape) * jnp.dtype(dtype).itemsize


def _rms_norm_f32(x, gain):
    ms = jnp.mean(x * x, axis=-1, keepdims=True)
    return x * lax.rsqrt(ms + RMS_EPS) * gain


ROW_CHUNK = 256
EXPERT_ROW_GRANULE = 128
DMA_ISSUE_UNROLL = 16


def _for_row_chunks(n_rows, body):
    assert n_rows % ROW_CHUNK == 0

    def step(c, carry):
        body(pl.ds(pl.multiple_of(c * ROW_CHUNK, ROW_CHUNK), ROW_CHUNK))
        return carry

    lax.fori_loop(0, n_rows // ROW_CHUNK, step, 0)


def _static_row_chunks(n_rows):
    assert n_rows % ROW_CHUNK == 0
    return [slice(c * ROW_CHUNK, (c + 1) * ROW_CHUNK) for c in range(n_rows // ROW_CHUNK)]


def _inproj_kernel(h_ref, g_ref, w_ref, z_ref, u_scr):
    j = pl.program_id(1)

    @pl.when(j == 0)
    def _():
        for rows in _static_row_chunks(u_scr.shape[0]):
            u = _rms_norm_f32(h_ref[rows, :], g_ref[...]).astype(BF16)
            u_scr[rows, :] = u
            z_ref[rows, :] = jnp.dot(u, w_ref[...].astype(BF16), preferred_element_type=F32)

    @pl.when(j > 0)
    def _():
        z_ref[...] = jnp.dot(u_scr[...], w_ref[...].astype(BF16), preferred_element_type=F32)


def _in_proj(h, gain, w_all, layer, *, tm=2048, tn=512):
    n, d = h.shape
    width = w_all.shape[2]
    assert n % tm == 0 and width % tn == 0
    limit = _vmem_limit(
        _nbytes((d, tn), F32) + _nbytes((tm, tn), F32),
        _nbytes((tm, d), F32) + _nbytes((tm, d), BF16),
        _nbytes((tm, d), F32) + _nbytes((d, tn), BF16) + _nbytes((tm, tn), F32))
    return pl.pallas_call(
        _inproj_kernel,
        grid=(n // tm, width // tn),
        in_specs=[pl.BlockSpec((tm, d), lambda i, j: (i, 0), pipeline_mode=pl.Buffered(1)),
                  pl.BlockSpec((1, d), lambda i, j: (0, 0)),
                  pl.BlockSpec((None, d, tn), lambda i, j: (layer, 0, j))],
        out_specs=pl.BlockSpec((tm, tn), lambda i, j: (i, j)),
        out_shape=jax.ShapeDtypeStruct((n, width), F32),
        scratch_shapes=[pltpu.VMEM((tm, d), BF16)],
        compiler_params=pltpu.CompilerParams(dimension_semantics=("arbitrary", "arbitrary"),
                                             vmem_limit_bytes=limit),
        name="in_proj",
    )(h, gain, w_all)


def _group_attention_into(q_ref, k_ref, v_ref, kh_ref, vh_ref, bias_ref, o_scr, l_scr, *, dilation, scale):
    first_chunk = pl.program_id(1) == 0
    n_sub = q_ref.shape[1] // (dilation * Q_BLOCK)

    def stream_rows(start):
        if dilation == 1:
            return pl.ds(start, Q_BLOCK)
        return pl.ds(start, Q_BLOCK, stride=dilation)

    for r in range(dilation):
        k_prev = kh_ref[0, stream_rows(r), :].astype(BF16)
        v_prev = vh_ref[0, stream_rows(r), :].astype(BF16)
        for n in range(n_sub):
            rows = stream_rows(r + n * Q_BLOCK * dilation)
            q = q_ref[0, rows, :].astype(BF16)
            k_cur = k_ref[0, rows, :].astype(BF16)
            v_cur = v_ref[0, rows, :].astype(BF16)
            kc = jnp.concatenate([k_prev, k_cur], axis=0)
            vc = jnp.concatenate([v_prev, v_cur], axis=0)
            s = lax.dot_general(q, kc, (((1,), (1,)), ((), ())), preferred_element_type=F32)
            s = s * scale + bias_ref[0]
            if n == 0:
                key_col = lax.broadcasted_iota(jnp.int32, s.shape, 1)
                s = jnp.where(jnp.logical_and(first_chunk, key_col < Q_BLOCK), -jnp.inf, s)
            m = jnp.max(s, axis=-1, keepdims=True)
            p = jnp.exp(s - m)
            den = jnp.sum(p, axis=-1, keepdims=True)
            o_scr[rows, :] = jnp.dot(p.astype(BF16), vc, preferred_element_type=F32) / den
            l_scr[rows, :] = jnp.broadcast_to(m + jnp.log(den), (Q_BLOCK, HEAD_DIM))
            k_prev, v_prev = k_cur, v_cur


def _attn_kernel(*refs, dilations, scale):
    n_groups = len(dilations)
    group_refs = [refs[6 * g:6 * (g + 1)] for g in range(n_groups)]
    out_ref = refs[6 * n_groups]
    scratch = refs[6 * n_groups + 1:]
    o_scrs, l_scrs = scratch[:n_groups], scratch[n_groups:]
    for g, dilation in enumerate(dilations):
        _group_attention_into(*group_refs[g], o_scrs[g], l_scrs[g], dilation=dilation, scale=scale)

    def mix_rows(rows):
        lses = [l[rows, :] for l in l_scrs]
        m = functools.reduce(jnp.maximum, lses)
        es = [jnp.exp(l - m) for l in lses]
        num = sum(e * o[rows, :] for e, o in zip(es, o_scrs))
        out_ref[0, rows, :] = num / sum(es)

    _for_row_chunks(out_ref.shape[1], mix_rows)


def _t5_causal_bucket(dist):
    max_exact = N_BUCKETS // 2
    df = jnp.maximum(dist, 1).astype(F32)
    large = max_exact + (jnp.log(df / max_exact) / math.log(MAX_DISTANCE / max_exact)
                         * (N_BUCKETS - max_exact)).astype(jnp.int32)
    large = jnp.minimum(large, N_BUCKETS - 1)
    return jnp.where(dist < max_exact, dist, large)


def _band_bias(bias_table, window, dilation):
    span = window // dilation
    qi = jnp.arange(Q_BLOCK)[:, None]
    kj = jnp.arange(2 * Q_BLOCK)[None, :]
    step = qi + Q_BLOCK - kj
    valid = (step >= 0) & (step <= span)
    bucket = _t5_causal_bucket(jnp.clip(step, 0, span) * dilation)
    onehot = (bucket[:, :, None] == jnp.arange(N_BUCKETS)[None, None, :]).astype(F32)
    bias = jnp.einsum("qkb,bh->hqk", onehot, bias_table, precision=lax.Precision.HIGHEST)
    return jnp.where(valid[None], bias, -jnp.inf)


def _attention(z, biases, dilations, *, batch, seq, attn_width, chunk=2048):
    assert seq % chunk == 0 and attn_width % HEAD_DIM == 0
    zv = z.reshape(batch, seq, z.shape[1])
    blk = _nbytes((chunk, HEAD_DIM), F32)
    in_specs, operands, pipelined = [], [], blk
    for group, (dilation, bias) in enumerate(zip(dilations, biases)):
        hist = Q_BLOCK * dilation
        assert chunk % hist == 0
        q_col = group * HEADS_PER_GROUP
        k_col = q_col + attn_width // HEAD_DIM
        v_col = k_col + attn_width // HEAD_DIM

        def cur(col):
            return pl.BlockSpec((1, chunk, HEAD_DIM), lambda b, c, hh, col=col: (b, c, col + hh))

        def prev(col, hist=hist):
            return pl.BlockSpec(
                (1, hist, HEAD_DIM),
                lambda b, c, hh, col=col, hist=hist: (b, jnp.maximum(c * (chunk // hist) - 1, 0), col + hh))

        in_specs += [cur(q_col), cur(k_col), cur(v_col), prev(k_col), prev(v_col),
                     pl.BlockSpec((1,) + bias.shape[1:], lambda b, c, hh: (hh, 0, 0))]
        operands += [zv, zv, zv, zv, zv, bias]
        pipelined += 3 * blk + 2 * _nbytes((hist, HEAD_DIM), F32) + _nbytes(bias.shape[1:], F32)
    n_groups = len(dilations)
    o = pl.pallas_call(
        functools.partial(_attn_kernel, dilations=tuple(dilations), scale=HEAD_DIM ** -0.5),
        grid=(batch, seq // chunk, HEADS_PER_GROUP),
        in_specs=in_specs,
        out_specs=pl.BlockSpec((1, chunk, HEAD_DIM), lambda b, c, hh: (b, c, hh)),
        out_shape=jax.ShapeDtypeStruct((batch, seq, GROUP_WIDTH), F32),
        scratch_shapes=[pltpu.VMEM((chunk, HEAD_DIM), F32)] * (2 * n_groups),
        compiler_params=pltpu.CompilerParams(
            dimension_semantics=("arbitrary",) * 3,
            vmem_limit_bytes=_vmem_limit(pipelined, 2 * n_groups * blk, 4 * blk)),
        name="dilated_attention",
    )(*operands)
    return o.reshape(batch * seq, GROUP_WIDTH)


def _merge_kernel(oattn_ref, p_ref, pp_ref, ga0_ref, ga1_ref, gb0_ref, gb1_ref, h_ref,
                  wpool_ref, pscale_ref, wao_ref, wpo_ref, wout_ref, gain_ref, out_ref,
                  *, tm, tiles_per_seq):
    tile_in_seq = pl.program_id(0) % tiles_per_seq

    halo = jnp.where(tile_in_seq == 0, 0.0, pp_ref[...])
    xe = jnp.concatenate([halo, p_ref[...]], axis=0)
    t = tile_in_seq * tm + lax.broadcasted_iota(jnp.int32, (tm, 1), 0)
    pooled = []
    for g, size in enumerate(POOL_SIZES):
        a = xe[:, g * POOL_GROUP_WIDTH:(g + 1) * POOL_GROUP_WIDTH]
        s, shift = a, 1
        while shift < size:
            s = s + pltpu.roll(s, shift, axis=0)
            shift *= 2
        count = jnp.minimum(t + 1, size).astype(F32)
        y = (s[POOL_HALO:] / count - a[POOL_HALO:]).astype(BF16)
        pooled.append(jnp.dot(y, wpool_ref[g], preferred_element_type=F32))
    o_pool = jnp.concatenate(pooled, axis=1) * pscale_ref[...]

    attn_proj = jnp.dot(oattn_ref[...].astype(BF16), wao_ref[...], preferred_element_type=F32)
    pool_proj = jnp.dot(o_pool.astype(BF16), wpo_ref[...], preferred_element_type=F32)
    gate_a = jax.nn.sigmoid(jnp.concatenate([ga0_ref[...], ga1_ref[...]], axis=1))
    gate_b = jax.nn.sigmoid(jnp.concatenate([gb0_ref[...], gb1_ref[...]], axis=1))
    merged = gate_a * attn_proj + gate_b * pool_proj
    mix = jnp.dot(merged.astype(BF16), wout_ref[...], preferred_element_type=F32)
    out_ref[...] = h_ref[...] + _rms_norm_f32(mix, gain_ref[...])


def _merge(h, z, o_attn, w_pool, pool_scale, w_attn_out, w_pool_out, w_out, gain, *, seq, attn_width, tm=512):
    n, d = h.shape
    pool_width = w_pool.shape[0] * POOL_GROUP_WIDTH
    assert pool_width == GROUP_WIDTH
    assert n % tm == 0 and seq % tm == 0 and tm % POOL_HALO == 0
    p_col = 3 * attn_width // GROUP_WIDTH
    gate_w = d // 2
    gate_col = (3 * attn_width + pool_width) // gate_w
    assert (3 * attn_width + pool_width) % gate_w == 0

    row_blk = lambda c: pl.BlockSpec((tm, GROUP_WIDTH), lambda i: (i, c))
    gate_blk = lambda c: pl.BlockSpec((tm, gate_w), lambda i: (i, gate_col + c))
    halo_blk = pl.BlockSpec((POOL_HALO, GROUP_WIDTH),
                            lambda i: (jnp.maximum(i * (tm // POOL_HALO) - 1, 0), p_col))
    full = lambda a: pl.BlockSpec(a.shape, lambda i: (0,) * a.ndim)
    weights = (w_pool, pool_scale, w_attn_out, w_pool_out, w_out, gain)

    act = 2 * _nbytes((tm, GROUP_WIDTH), F32) + 4 * _nbytes((tm, gate_w), F32) + 2 * _nbytes((tm, d), F32)
    wbytes = sum(_nbytes(a.shape, a.dtype) for a in weights)
    limit = _vmem_limit(act + wbytes, 0, 8 * _nbytes((tm, d), F32))
    return pl.pallas_call(
        functools.partial(_merge_kernel, tm=tm, tiles_per_seq=seq // tm),
        grid=(n // tm,),
        in_specs=[row_blk(0), row_blk(p_col), halo_blk] + [gate_blk(c) for c in range(4)]
                 + [pl.BlockSpec((tm, d), lambda i: (i, 0))] + [full(a) for a in weights],
        out_specs=pl.BlockSpec((tm, d), lambda i: (i, 0)),
        out_shape=jax.ShapeDtypeStruct((n, d), F32),
        compiler_params=pltpu.CompilerParams(dimension_semantics=("arbitrary",), vmem_limit_bytes=limit),
        name="mixer_merge",
    )(o_attn, z, z, z, z, z, z, h, *weights)


def _unpack_bf16_pair(words):
    lo = lax.bitcast_convert_type(words << 16, F32)
    hi = lax.bitcast_convert_type(words & jnp.uint32(0xFFFF0000), F32)
    return jnp.concatenate([lo, hi], axis=1).astype(BF16)


def _swiglu(u, wg_ref, wu_ref, wd_ref):
    gate = jnp.dot(u, wg_ref[...].astype(BF16), preferred_element_type=F32)
    up = jnp.dot(u, wu_ref[...].astype(BF16), preferred_element_type=F32)
    hidden = (gate * jax.nn.sigmoid(gate) * up).astype(BF16)
    return jnp.dot(hidden, wd_ref[...].astype(BF16), preferred_element_type=F32)


def _swiglu_accumulate(u_scr, wg_ref, wu_ref, wd_ref, acc_ref, rows=slice(None)):
    acc_ref[rows, :] += _swiglu(u_scr[rows, :], wg_ref, wu_ref, wd_ref)


def _ffn_vmem_limit(tm, d, tf, x_bytes, scratch_bytes):
    return _vmem_limit(
        x_bytes + 3 * _nbytes((d, tf), F32) + _nbytes((tm, d), F32),
        _nbytes((tm, d), BF16) + scratch_bytes,
        3 * _nbytes((d, tf), BF16) + 4 * _nbytes((tm, tf), F32) + 2 * _nbytes((tm, d), F32))


def _dense_ffn_kernel(h_ref, gin_ref, wg_ref, wu_ref, wd_ref, gout_ref, out_ref, u_scr):
    j = pl.program_id(1)
    last = pl.num_programs(1) - 1
    chunks = _static_row_chunks(u_scr.shape[0])

    @pl.when(j == 0)
    def _():
        for rows in chunks:
            u = _rms_norm_f32(h_ref[rows, :], gin_ref[...]).astype(BF16)
            u_scr[rows, :] = u
            out_ref[rows, :] = _swiglu(u, wg_ref, wu_ref, wd_ref)

    @pl.when(jnp.logical_and(j > 0, j < last))
    def _():
        _swiglu_accumulate(u_scr, wg_ref, wu_ref, wd_ref, out_ref)

    @pl.when(j == last)
    def _():
        for rows in chunks:
            f = out_ref[rows, :] + _swiglu(u_scr[rows, :], wg_ref, wu_ref, wd_ref)
            out_ref[rows, :] = h_ref[rows, :] + _rms_norm_f32(f, gout_ref[...])


def _dense_ffn(h, gain_in, gain_out, w_gate, w_up, w_down, *, tm=2048, tf=256):
    n, d = h.shape
    ff = w_gate.shape[1]
    assert n % tm == 0 and ff % tf == 0 and ff // tf >= 2
    row_blk = pl.BlockSpec((tm, d), lambda i, j: (i, 0), pipeline_mode=pl.Buffered(1))
    gain_blk = pl.BlockSpec((1, d), lambda i, j: (0, 0))
    up_blk = pl.BlockSpec((d, tf), lambda i, j: (0, j))
    return pl.pallas_call(
        _dense_ffn_kernel,
        grid=(n // tm, ff // tf),
        in_specs=[row_blk, gain_blk, up_blk, up_blk, pl.BlockSpec((tf, d), lambda i, j: (j, 0)), gain_blk],
        out_specs=row_blk,
        out_shape=jax.ShapeDtypeStruct((n, d), F32),
        scratch_shapes=[pltpu.VMEM((tm, d), BF16)],
        compiler_params=pltpu.CompilerParams(
            dimension_semantics=("arbitrary", "arbitrary"),
            vmem_limit_bytes=_ffn_vmem_limit(tm, d, tf, _nbytes((tm, d), F32), 0)),
        name="swiglu_dense",
    )(h, gain_in, w_gate, w_up, w_down, gain_out)


def _expert_ffn_kernel(tile_expert_ref, tile_rows_ref, src_ref, x_hbm, wg_ref, wu_ref, wd_ref, out_ref,
                       u_scr, stage, sem):
    del tile_expert_ref
    i, j = pl.program_id(0), pl.program_id(1)
    tm = stage.shape[0]
    granule = EXPERT_ROW_GRANULE
    n_chunks = tile_rows_ref[i] // granule

    @pl.when(j == 0)
    def _():
        def zero_chunk(c, carry):
            rows = pl.ds(pl.multiple_of(c * granule, granule), granule)
            out_ref[rows, :] = jnp.zeros((granule, out_ref.shape[1]), F32)
            return carry

        lax.fori_loop(0, tm // granule, zero_chunk, 0)

        def gather_chunk(c, carry):
            base = c * granule

            def start(r, inner):
                pltpu.make_async_copy(x_hbm.at[pl.ds(src_ref[i * tm + base + r], 1), :],
                                      stage.at[pl.ds(base + r, 1), :], sem).start()
                return inner

            return lax.fori_loop(0, granule, start, carry, unroll=DMA_ISSUE_UNROLL)

        lax.fori_loop(0, n_chunks, gather_chunk, 0)

        def wait_chunk(c, carry):
            pltpu.make_async_copy(x_hbm.at[pl.ds(0, granule), :], stage.at[pl.ds(0, granule), :], sem).wait()
            return carry

        lax.fori_loop(0, n_chunks, wait_chunk, 0)

        def unpack_chunk(c, carry):
            rows = pl.ds(pl.multiple_of(c * granule, granule), granule)
            u_scr[rows, :] = _unpack_bf16_pair(stage[rows, :])
            return carry

        lax.fori_loop(0, n_chunks, unpack_chunk, 0)

    for k in range(1, tm // granule + 1):
        @pl.when(n_chunks == k)
        def _(k=k):
            _swiglu_accumulate(u_scr, wg_ref, wu_ref, wd_ref, out_ref, slice(0, k * granule))


def _expert_ffn(xp, src, tile_expert, tile_rows, w_gate, w_up, w_down, *, tm, tf):
    rows = src.shape[0]
    n_exp, d, ff = w_gate.shape
    assert rows % tm == 0 and tm % EXPERT_ROW_GRANULE == 0
    assert ff % tf == 0 and xp.shape[1] == d // 2
    nj = ff // tf

    def ff_tile(i, j, tr):
        return jnp.where(tr[i] > 0, j, nj - 1)

    up_blk = pl.BlockSpec((None, d, tf), lambda i, j, te, tr, src: (te[i], 0, ff_tile(i, j, tr)))
    down_blk = pl.BlockSpec((None, tf, d), lambda i, j, te, tr, src: (te[i], ff_tile(i, j, tr), 0))
    return pl.pallas_call(
        _expert_ffn_kernel,
        grid_spec=pltpu.PrefetchScalarGridSpec(
            num_scalar_prefetch=3,
            grid=(rows // tm, nj),
            in_specs=[pl.BlockSpec(memory_space=pl.ANY), up_blk, up_blk, down_blk],
            out_specs=pl.BlockSpec((tm, d), lambda i, j, te, tr, src: (i, 0)),
            scratch_shapes=[pltpu.VMEM((tm, d), BF16), pltpu.VMEM((tm, d // 2), jnp.uint32),
                            pltpu.SemaphoreType.DMA]),
        out_shape=jax.ShapeDtypeStruct((rows, d), F32),
        compiler_params=pltpu.CompilerParams(
            dimension_semantics=("arbitrary", "arbitrary"),
            vmem_limit_bytes=_ffn_vmem_limit(tm, d, tf, 0, _nbytes((tm, d // 2), jnp.uint32))),
        name="swiglu_experts",
    )(tile_expert, tile_rows, src, xp, w_gate, w_up, w_down)


def _router_kernel(h_ref, gain_ref, wrt_ref, xp_ref, idx_ref, wts_ref):
    u = _rms_norm_f32(h_ref[...], gain_ref[...])
    logits = lax.dot_general(wrt_ref[...], u, (((1,), (1,)), ((), ())),
                             precision=lax.Precision.HIGHEST, preferred_element_type=F32)
    n_exp = logits.shape[0]
    expert = lax.broadcasted_iota(jnp.int32, logits.shape, 0)
    v1 = jnp.max(logits, axis=0, keepdims=True)
    i1 = jnp.min(jnp.where(logits == v1, expert, n_exp), axis=0, keepdims=True)
    rest = jnp.where(expert == i1, -jnp.inf, logits)
    v2 = jnp.max(rest, axis=0, keepdims=True)
    i2 = jnp.min(jnp.where(rest == v2, expert, n_exp), axis=0, keepdims=True)
    e2 = jnp.exp(v2 - v1)
    idx_ref[...] = jnp.concatenate([i1, i2], axis=0)
    wts_ref[...] = jnp.concatenate([1.0 / (1.0 + e2), e2 / (1.0 + e2)], axis=0)

    bits = lax.bitcast_convert_type(u.astype(BF16).astype(F32), jnp.uint32)
    half = bits.shape[1] // 2
    xp_ref[...] = (bits[:, :half] >> 16) | bits[:, half:]


def _router(h, gain, w_router_t, *, tm=512):
    n, d = h.shape
    n_exp = w_router_t.shape[0]
    assert n % tm == 0
    limit = _vmem_limit(_nbytes((tm, d), F32) + _nbytes((tm, d // 2), jnp.uint32) + _nbytes((n_exp, d), F32),
                        0, 6 * _nbytes((tm, d), F32))
    return pl.pallas_call(
        _router_kernel,
        grid=(n // tm,),
        in_specs=[pl.BlockSpec((tm, d), lambda i: (i, 0)),
                  pl.BlockSpec((1, d), lambda i: (0, 0)),
                  pl.BlockSpec((n_exp, d), lambda i: (0, 0))],
        out_specs=[pl.BlockSpec((tm, d // 2), lambda i: (i, 0)),
                   pl.BlockSpec((TOP_K, tm), lambda i: (0, i)),
                   pl.BlockSpec((TOP_K, tm), lambda i: (0, i))],
        out_shape=[jax.ShapeDtypeStruct((n, d // 2), jnp.uint32),
                   jax.ShapeDtypeStruct((TOP_K, n), jnp.int32),
                   jax.ShapeDtypeStruct((TOP_K, n), F32)],
        compiler_params=pltpu.CompilerParams(dimension_semantics=("arbitrary",), vmem_limit_bytes=limit),
        name="moe_router",
    )(h, gain, w_router_t)


def _combine_kernel(pos_ref, y_hbm, wts_ref, h_ref, gain_ref, out_ref, bufs, sems, *, tm):
    i = pl.program_id(0)

    def start_tile(tile, slot):
        def start(r, carry):
            for k in range(TOP_K):
                pltpu.make_async_copy(y_hbm.at[pl.ds(pos_ref[TOP_K * (tile * tm + r) + k], 1), :],
                                      bufs.at[slot, k, pl.ds(r, 1), :], sems.at[slot, k]).start()
            return carry

        lax.fori_loop(0, tm, start, 0, unroll=DMA_ISSUE_UNROLL)

    @pl.when(i == 0)
    def _():
        start_tile(0, 0)

    @pl.when(i + 1 < pl.num_programs(0))
    def _():
        start_tile(i + 1, (i + 1) % 2)

    slot = i % 2
    for k in range(TOP_K):
        pltpu.make_async_copy(y_hbm.at[pl.ds(0, tm), :], bufs.at[slot, k], sems.at[slot, k]).wait()
    w = wts_ref[...]
    y = w[:, 0:1] * bufs[slot, 0] + w[:, 1:2] * bufs[slot, 1]
    out_ref[...] = h_ref[...] + _rms_norm_f32(y, gain_ref[...])


def _combine(y_rows, pos, wts, h, gain, *, tm=512):
    n, d = h.shape
    assert n % tm == 0
    limit = _vmem_limit(2 * _nbytes((tm, d), F32) + _nbytes((tm, 128), F32),
                        2 * TOP_K * _nbytes((tm, d), F32), 4 * _nbytes((tm, d), F32))
    return pl.pallas_call(
        functools.partial(_combine_kernel, tm=tm),
        grid_spec=pltpu.PrefetchScalarGridSpec(
            num_scalar_prefetch=1,
            grid=(n // tm,),
            in_specs=[pl.BlockSpec(memory_space=pl.ANY),
                      pl.BlockSpec((tm, TOP_K), lambda i, pos: (i, 0)),
                      pl.BlockSpec((tm, d), lambda i, pos: (i, 0)),
                      pl.BlockSpec((1, d), lambda i, pos: (0, 0))],
            out_specs=pl.BlockSpec((tm, d), lambda i, pos: (i, 0)),
            scratch_shapes=[pltpu.VMEM((2, TOP_K, tm, d), F32), pltpu.SemaphoreType.DMA((2, TOP_K))]),
        out_shape=jax.ShapeDtypeStruct((n, d), F32),
        compiler_params=pltpu.CompilerParams(dimension_semantics=("arbitrary",), vmem_limit_bytes=limit),
        name="moe_combine",
    )(pos, y_rows, wts, h, gain)


def _dispatch_plan(top_idx, n_exp, tm):
    n = top_idx.shape[1]
    n_assign = n * TOP_K
    n_tiles = n_assign // tm + n_exp
    expert = top_idx.T.reshape(n_assign)
    onehot = (expert[:, None] == jnp.arange(n_exp, dtype=jnp.int32)[None, :]).astype(jnp.int32)
    rank = jnp.sum((jnp.cumsum(onehot, axis=0) - onehot) * onehot, axis=1)
    counts = jnp.sum(onehot, axis=0)
    tiles_per_expert = (counts + tm - 1) // tm
    tile_ends = jnp.cumsum(tiles_per_expert)
    tile_starts = tile_ends - tiles_per_expert
    pos = tile_starts[expert] * tm + rank
    src = jnp.zeros((n_tiles * tm,), jnp.int32).at[pos].set(jnp.arange(n_assign, dtype=jnp.int32) // TOP_K)
    tile = jnp.arange(n_tiles, dtype=jnp.int32)
    tile_expert = jnp.sum((tile[:, None] >= tile_ends[None, :]).astype(jnp.int32), axis=1)
    used = tile < tile_ends[-1]
    tile_expert = jnp.where(used, tile_expert, tile_expert[tile_ends[-1] - 1])
    granule = EXPERT_ROW_GRANULE
    rows_left = (counts[tile_expert] + granule - 1) // granule * granule - (tile - tile_starts[tile_expert]) * tm
    tile_rows = jnp.where(used, jnp.clip(rows_left, 0, tm), 0)
    return pos.astype(jnp.int32), src, tile_expert.astype(jnp.int32), tile_rows.astype(jnp.int32)


def _moe(h, gain_in, gain_out, w_router, w_gate, w_up, w_down, *, tm=1152, tf=512):
    n_exp = w_router.shape[1]
    xp, top_idx, top_w = _router(h, gain_in, w_router.T)
    pos, src, tile_expert, tile_rows = _dispatch_plan(top_idx, n_exp, tm)
    y_rows = _expert_ffn(xp, src, tile_expert, tile_rows, w_gate, w_up, w_down, tm=tm, tf=tf)
    return _combine(y_rows, pos, top_w.T, h, gain_out)


def kernel(x, w_in, rel_bias, w_pool, pool_scale, w_attn_out, w_pool_out, w_out, norm_gains,
           dense_w_gate, dense_w_up, dense_w_down, moe_w_router, moe_w_gate, moe_w_up, moe_w_down):
    batch, seq, d = x.shape
    depth = w_in.shape[0]
    attn_width = N_ATTN_GROUPS * GROUP_WIDTH
    h = x.reshape(batch * seq, d)
    biases = [_band_bias(rel_bias[:, g * HEADS_PER_GROUP:(g + 1) * HEADS_PER_GROUP], window, dilation)
              for g, (window, dilation) in enumerate(DILATION_PATTERNS)]
    for layer in range(depth):
        gains = norm_gains[layer].reshape(4, 1, d)
        z = _in_proj(h, gains[0], w_in, layer)
        o_attn = _attention(z, biases, [dilation for _, dilation in DILATION_PATTERNS],
                            batch=batch, seq=seq, attn_width=attn_width)
        h = _merge(h, z, o_attn, w_pool[layer].astype(BF16), pool_scale[layer].reshape(1, -1),
                   w_attn_out[layer].astype(BF16), w_pool_out[layer].astype(BF16), w_out[layer].astype(BF16),
                   gains[1], seq=seq, attn_width=attn_width)
        j = layer // 2
        if layer % 2 == 0:
            h = _dense_ffn(h, gains[2], gains[3], dense_w_gate[j], dense_w_up[j], dense_w_down[j])
        else:
            h = _moe(h, gains[2], gains[3], moe_w_router[j], moe_w_gate[j], moe_w_up[j], moe_w_down[j])
    return h.reshape(batch, seq, d)
```

```python
import functools
import math

import jax
import jax.numpy as jnp
from jax import lax
from jax.experimental import pallas as pl
from jax.experimental.pallas import tpu as pltpu

F32 = jnp.float32
BF16 = jnp.bfloat16

RMS_EPS = 1e-6
HEAD_DIM = 128
SLAB_WIDTH = HEAD_DIM
Q_BLOCK = 128
DILATION_PATTERNS = ((128, 1), (512, 4), (2048, 16))
N_ATTN_GROUPS = len(DILATION_PATTERNS)
HEADS_PER_GROUP = 4
GROUP_WIDTH = HEADS_PER_GROUP * HEAD_DIM
POOL_SIZES = (2, 4, 8, 16)
POOL_GROUP_WIDTH = 128
POOL_HALO = 16
N_BUCKETS = 32
MAX_DISTANCE = 2048
TOP_K = 2

V7X_VMEM_BYTES = 64 * 1024 * 1024


def _vmem_limit(pipelined_bytes, resident_bytes, temp_bytes):
    return min(2 * pipelined_bytes + resident_bytes + temp_bytes, V7X_VMEM_BYTES)


def _nbytes(shape, dtype):
    return math.prod(shape) * jnp.dtype(dtype).itemsize


def _rms_norm_f32(x, gain):
    ms = jnp.mean(x * x, axis=-1, keepdims=True)
    return x * lax.rsqrt(ms + RMS_EPS) * gain


ROW_CHUNK = 256
EXPERT_ROW_GRANULE = 128
DMA_ISSUE_UNROLL = 16


def _for_row_chunks(n_rows, body):
    assert n_rows % ROW_CHUNK == 0

    def step(c, carry):
        body(pl.ds(pl.multiple_of(c * ROW_CHUNK, ROW_CHUNK), ROW_CHUNK))
        return carry

    lax.fori_loop(0, n_rows // ROW_CHUNK, step, 0)


def _static_row_chunks(n_rows):
    assert n_rows % ROW_CHUNK == 0
    return [slice(c * ROW_CHUNK, (c + 1) * ROW_CHUNK) for c in range(n_rows // ROW_CHUNK)]


def _store_column_slabs(z_ref, rows, values):
    for a in range(z_ref.shape[0]):
        z_ref[a, rows, :] = values[:, a * SLAB_WIDTH:(a + 1) * SLAB_WIDTH]


def _load_column_slabs(ref):
    return jnp.concatenate([ref[a] for a in range(ref.shape[0])], axis=1)


def _inproj_kernel(h_ref, g_ref, w_ref, z_ref, u_scr):
    j = pl.program_id(1)

    @pl.when(j == 0)
    def _():
        for rows in _static_row_chunks(u_scr.shape[0]):
            u = _rms_norm_f32(h_ref[rows, :], g_ref[...]).astype(BF16)
            u_scr[rows, :] = u
            _store_column_slabs(z_ref, rows, jnp.dot(u, w_ref[...].astype(BF16), preferred_element_type=F32))

    @pl.when(j > 0)
    def _():
        _store_column_slabs(z_ref, slice(None),
                            jnp.dot(u_scr[...], w_ref[...].astype(BF16), preferred_element_type=F32))


def _in_proj(h, gain, w_all, layer, *, tm=2048, tn=512):
    n, d = h.shape
    width = w_all.shape[2]
    assert n % tm == 0 and width % tn == 0 and tn % SLAB_WIDTH == 0
    limit = _vmem_limit(
        _nbytes((d, tn), F32) + _nbytes((tm, tn), F32),
        _nbytes((tm, d), F32) + _nbytes((tm, d), BF16),
        _nbytes((tm, d), F32) + _nbytes((d, tn), BF16) + _nbytes((tm, tn), F32))
    return pl.pallas_call(
        _inproj_kernel,
        grid=(n // tm, width // tn),
        in_specs=[pl.BlockSpec((tm, d), lambda i, j: (i, 0), pipeline_mode=pl.Buffered(1)),
                  pl.BlockSpec((1, d), lambda i, j: (0, 0)),
                  pl.BlockSpec((None, d, tn), lambda i, j: (layer, 0, j))],
        out_specs=pl.BlockSpec((tn // SLAB_WIDTH, tm, SLAB_WIDTH), lambda i, j: (j, i, 0)),
        out_shape=jax.ShapeDtypeStruct((width // SLAB_WIDTH, n, SLAB_WIDTH), F32),
        scratch_shapes=[pltpu.VMEM((tm, d), BF16)],
        compiler_params=pltpu.CompilerParams(dimension_semantics=("arbitrary", "arbitrary"),
                                             vmem_limit_bytes=limit),
        name="in_proj",
    )(h, gain, w_all)


def _group_attention_into(q_ref, k_ref, v_ref, kh_ref, vh_ref, bias_ref, o_scr, l_scr, *, dilation, scale):
    first_chunk = pl.program_id(1) == 0
    n_sub = q_ref.shape[1] // (dilation * Q_BLOCK)

    def stream_rows(start):
        if dilation == 1:
            return pl.ds(start, Q_BLOCK)
        return pl.ds(start, Q_BLOCK, stride=dilation)

    for r in range(dilation):
        k_prev = kh_ref[0, stream_rows(r), :].astype(BF16)
        v_prev = vh_ref[0, stream_rows(r), :].astype(BF16)
        for n in range(n_sub):
            rows = stream_rows(r + n * Q_BLOCK * dilation)
            q = q_ref[0, rows, :].astype(BF16)
            k_cur = k_ref[0, rows, :].astype(BF16)
            v_cur = v_ref[0, rows, :].astype(BF16)
            kc = jnp.concatenate([k_prev, k_cur], axis=0)
            vc = jnp.concatenate([v_prev, v_cur], axis=0)
            s = lax.dot_general(q, kc, (((1,), (1,)), ((), ())), preferred_element_type=F32)
            s = s * scale + bias_ref[0]
            if n == 0:
                key_col = lax.broadcasted_iota(jnp.int32, s.shape, 1)
                s = jnp.where(jnp.logical_and(first_chunk, key_col < Q_BLOCK), -jnp.inf, s)
            m = jnp.max(s, axis=-1, keepdims=True)
            p = jnp.exp(s - m)
            den = jnp.sum(p, axis=-1, keepdims=True)
            o_scr[rows, :] = jnp.dot(p.astype(BF16), vc, preferred_element_type=F32) / den
            l_scr[rows, :] = jnp.broadcast_to(m + jnp.log(den), (Q_BLOCK, HEAD_DIM))
            k_prev, v_prev = k_cur, v_cur


def _attn_kernel(*refs, dilations, scale):
    n_groups = len(dilations)
    group_refs = [refs[6 * g:6 * (g + 1)] for g in range(n_groups)]
    out_ref = refs[6 * n_groups]
    scratch = refs[6 * n_groups + 1:]
    o_scrs, l_scrs = scratch[:n_groups], scratch[n_groups:]
    for g, dilation in enumerate(dilations):
        _group_attention_into(*group_refs[g], o_scrs[g], l_scrs[g], dilation=dilation, scale=scale)

    def mix_rows(rows):
        lses = [l[rows, :] for l in l_scrs]
        m = functools.reduce(jnp.maximum, lses)
        es = [jnp.exp(l - m) for l in lses]
        num = sum(e * o[rows, :] for e, o in zip(es, o_scrs))
        out_ref[0, rows, :] = num / sum(es)

    _for_row_chunks(out_ref.shape[1], mix_rows)


def _t5_causal_bucket(dist):
    max_exact = N_BUCKETS // 2
    df = jnp.maximum(dist, 1).astype(F32)
    large = max_exact + (jnp.log(df / max_exact) / math.log(MAX_DISTANCE / max_exact)
                         * (N_BUCKETS - max_exact)).astype(jnp.int32)
    large = jnp.minimum(large, N_BUCKETS - 1)
    return jnp.where(dist < max_exact, dist, large)


def _band_bias(bias_table, window, dilation):
    span = window // dilation
    qi = jnp.arange(Q_BLOCK)[:, None]
    kj = jnp.arange(2 * Q_BLOCK)[None, :]
    step = qi + Q_BLOCK - kj
    valid = (step >= 0) & (step <= span)
    bucket = _t5_causal_bucket(jnp.clip(step, 0, span) * dilation)
    onehot = (bucket[:, :, None] == jnp.arange(N_BUCKETS)[None, None, :]).astype(F32)
    bias = jnp.einsum("qkb,bh->hqk", onehot, bias_table, precision=lax.Precision.HIGHEST)
    return jnp.where(valid[None], bias, -jnp.inf)


def _attention(z, biases, dilations, *, batch, seq, attn_width, chunk=2048):
    assert seq % chunk == 0 and attn_width % HEAD_DIM == 0 and z.shape[2] == HEAD_DIM
    zv = z.reshape(z.shape[0], batch, seq, HEAD_DIM)
    blk = _nbytes((chunk, HEAD_DIM), F32)
    in_specs, operands, pipelined = [], [], blk
    for group, (dilation, bias) in enumerate(zip(dilations, biases)):
        hist = Q_BLOCK * dilation
        assert chunk % hist == 0
        q_col = group * HEADS_PER_GROUP
        k_col = q_col + attn_width // HEAD_DIM
        v_col = k_col + attn_width // HEAD_DIM

        def cur(col):
            return pl.BlockSpec((None, 1, chunk, HEAD_DIM), lambda b, c, hh, col=col: (col + hh, b, c, 0))

        def prev(col, hist=hist):
            return pl.BlockSpec(
                (None, 1, hist, HEAD_DIM),
                lambda b, c, hh, col=col, hist=hist: (col + hh, b, jnp.maximum(c * (chunk // hist) - 1, 0), 0))

        in_specs += [cur(q_col), cur(k_col), cur(v_col), prev(k_col), prev(v_col),
                     pl.BlockSpec((1,) + bias.shape[1:], lambda b, c, hh: (hh, 0, 0))]
        operands += [zv, zv, zv, zv, zv, bias]
        pipelined += 3 * blk + 2 * _nbytes((hist, HEAD_DIM), F32) + _nbytes(bias.shape[1:], F32)
    n_groups = len(dilations)
    o = pl.pallas_call(
        functools.partial(_attn_kernel, dilations=tuple(dilations), scale=HEAD_DIM ** -0.5),
        grid=(batch, seq // chunk, HEADS_PER_GROUP),
        in_specs=in_specs,
        out_specs=pl.BlockSpec((1, chunk, HEAD_DIM), lambda b, c, hh: (b, c, hh)),
        out_shape=jax.ShapeDtypeStruct((batch, seq, GROUP_WIDTH), F32),
        scratch_shapes=[pltpu.VMEM((chunk, HEAD_DIM), F32)] * (2 * n_groups),
        compiler_params=pltpu.CompilerParams(
            dimension_semantics=("arbitrary",) * 3,
            vmem_limit_bytes=_vmem_limit(pipelined, 2 * n_groups * blk, 4 * blk)),
        name="dilated_attention",
    )(*operands)
    return o.reshape(batch * seq, GROUP_WIDTH)


def _merge_kernel(oattn_ref, p_ref, pp_ref, ga0_ref, ga1_ref, gb0_ref, gb1_ref, h_ref,
                  wpool_ref, pscale_ref, wao_ref, wpo_ref, wout_ref, gain_ref, out_ref,
                  *, tm, tiles_per_seq):
    tile_in_seq = pl.program_id(0) % tiles_per_seq

    halo = jnp.where(tile_in_seq == 0, 0.0, _load_column_slabs(pp_ref))
    xe = jnp.concatenate([halo, _load_column_slabs(p_ref)], axis=0)
    t = tile_in_seq * tm + lax.broadcasted_iota(jnp.int32, (tm, 1), 0)
    pooled = []
    for g, size in enumerate(POOL_SIZES):
        a = xe[:, g * POOL_GROUP_WIDTH:(g + 1) * POOL_GROUP_WIDTH]
        s, shift = a, 1
        while shift < size:
            s = s + pltpu.roll(s, shift, axis=0)
            shift *= 2
        count = jnp.minimum(t + 1, size).astype(F32)
        y = (s[POOL_HALO:] / count - a[POOL_HALO:]).astype(BF16)
        pooled.append(jnp.dot(y, wpool_ref[g], preferred_element_type=F32))
    o_pool = jnp.concatenate(pooled, axis=1) * pscale_ref[...]

    attn_proj = jnp.dot(oattn_ref[...].astype(BF16), wao_ref[...], preferred_element_type=F32)
    pool_proj = jnp.dot(o_pool.astype(BF16), wpo_ref[...], preferred_element_type=F32)
    gate_a = jax.nn.sigmoid(jnp.concatenate([_load_column_slabs(ga0_ref), _load_column_slabs(ga1_ref)], axis=1))
    gate_b = jax.nn.sigmoid(jnp.concatenate([_load_column_slabs(gb0_ref), _load_column_slabs(gb1_ref)], axis=1))
    merged = gate_a * attn_proj + gate_b * pool_proj
    mix = jnp.dot(merged.astype(BF16), wout_ref[...], preferred_element_type=F32)
    out_ref[...] = h_ref[...] + _rms_norm_f32(mix, gain_ref[...])


def _merge(h, z, o_attn, w_pool, pool_scale, w_attn_out, w_pool_out, w_out, gain, *, seq, attn_width, tm=512):
    n, d = h.shape
    pool_width = w_pool.shape[0] * POOL_GROUP_WIDTH
    assert pool_width == GROUP_WIDTH
    assert n % tm == 0 and seq % tm == 0 and tm % POOL_HALO == 0
    p_col = 3 * attn_width // GROUP_WIDTH
    gate_w = d // 2
    gate_col = (3 * attn_width + pool_width) // gate_w
    assert (3 * attn_width + pool_width) % gate_w == 0

    pool_slabs, gate_slabs = GROUP_WIDTH // SLAB_WIDTH, gate_w // SLAB_WIDTH
    attn_blk = pl.BlockSpec((tm, GROUP_WIDTH), lambda i: (i, 0))
    pool_blk = pl.BlockSpec((pool_slabs, tm, SLAB_WIDTH), lambda i: (p_col, i, 0))
    gate_blk = lambda c: pl.BlockSpec((gate_slabs, tm, SLAB_WIDTH), lambda i: (gate_col + c, i, 0))
    halo_blk = pl.BlockSpec((pool_slabs, POOL_HALO, SLAB_WIDTH),
                            lambda i: (p_col, jnp.maximum(i * (tm // POOL_HALO) - 1, 0), 0))
    full = lambda a: pl.BlockSpec(a.shape, lambda i: (0,) * a.ndim)
    weights = (w_pool, pool_scale, w_attn_out, w_pool_out, w_out, gain)

    act = 2 * _nbytes((tm, GROUP_WIDTH), F32) + 4 * _nbytes((tm, gate_w), F32) + 2 * _nbytes((tm, d), F32)
    wbytes = sum(_nbytes(a.shape, a.dtype) for a in weights)
    limit = _vmem_limit(act + wbytes, 0, 8 * _nbytes((tm, d), F32))
    return pl.pallas_call(
        functools.partial(_merge_kernel, tm=tm, tiles_per_seq=seq // tm),
        grid=(n // tm,),
        in_specs=[attn_blk, pool_blk, halo_blk] + [gate_blk(c) for c in range(4)]
                 + [pl.BlockSpec((tm, d), lambda i: (i, 0))] + [full(a) for a in weights],
        out_specs=pl.BlockSpec((tm, d), lambda i: (i, 0)),
        out_shape=jax.ShapeDtypeStruct((n, d), F32),
        compiler_params=pltpu.CompilerParams(dimension_semantics=("arbitrary",), vmem_limit_bytes=limit),
        name="mixer_merge",
    )(o_attn, z, z, z, z, z, z, h, *weights)


def _unpack_bf16_pair(words):
    lo = lax.bitcast_convert_type(words << 16, F32)
    hi = lax.bitcast_convert_type(words & jnp.uint32(0xFFFF0000), F32)
    return jnp.concatenate([lo, hi], axis=1).astype(BF16)


def _swiglu(u, wg_ref, wu_ref, wd_ref):
    gate = jnp.dot(u, wg_ref[...].astype(BF16), preferred_element_type=F32)
    up = jnp.dot(u, wu_ref[...].astype(BF16), preferred_element_type=F32)
    hidden = (gate * jax.nn.sigmoid(gate) * up).astype(BF16)
    return jnp.dot(hidden, wd_ref[...].astype(BF16), preferred_element_type=F32)


def _swiglu_accumulate(u_scr, wg_ref, wu_ref, wd_ref, acc_ref, rows=slice(None)):
    acc_ref[rows, :] += _swiglu(u_scr[rows, :], wg_ref, wu_ref, wd_ref)


def _ffn_vmem_limit(tm, d, tf, x_bytes, scratch_bytes):
    return _vmem_limit(
        x_bytes + 3 * _nbytes((d, tf), F32) + _nbytes((tm, d), F32),
        _nbytes((tm, d), BF16) + scratch_bytes,
        3 * _nbytes((d, tf), BF16) + 4 * _nbytes((tm, tf), F32) + 2 * _nbytes((tm, d), F32))


def _dense_ffn_kernel(h_ref, gin_ref, wg_ref, wu_ref, wd_ref, gout_ref, out_ref, u_scr):
    j = pl.program_id(1)
    last = pl.num_programs(1) - 1
    chunks = _static_row_chunks(u_scr.shape[0])

    @pl.when(j == 0)
    def _():
        for rows in chunks:
            u = _rms_norm_f32(h_ref[rows, :], gin_ref[...]).astype(BF16)
            u_scr[rows, :] = u
            out_ref[rows, :] = _swiglu(u, wg_ref, wu_ref, wd_ref)

    @pl.when(jnp.logical_and(j > 0, j < last))
    def _():
        _swiglu_accumulate(u_scr, wg_ref, wu_ref, wd_ref, out_ref)

    @pl.when(j == last)
    def _():
        for rows in chunks:
            f = out_ref[rows, :] + _swiglu(u_scr[rows, :], wg_ref, wu_ref, wd_ref)
            out_ref[rows, :] = h_ref[rows, :] + _rms_norm_f32(f, gout_ref[...])


def _dense_ffn(h, gain_in, gain_out, w_gate, w_up, w_down, *, tm=1024, tf=256):
    n, d = h.shape
    ff = w_gate.shape[1]
    assert n % tm == 0 and ff % tf == 0 and ff // tf >= 2
    row_blk = pl.BlockSpec((tm, d), lambda i, j: (i, 0))
    gain_blk = pl.BlockSpec((1, d), lambda i, j: (0, 0))
    up_blk = pl.BlockSpec((d, tf), lambda i, j: (0, j))
    return pl.pallas_call(
        _dense_ffn_kernel,
        grid=(n // tm, ff // tf),
        in_specs=[row_blk, gain_blk, up_blk, up_blk, pl.BlockSpec((tf, d), lambda i, j: (j, 0)), gain_blk],
        out_specs=row_blk,
        out_shape=jax.ShapeDtypeStruct((n, d), F32),
        scratch_shapes=[pltpu.VMEM((tm, d), BF16)],
        compiler_params=pltpu.CompilerParams(
            dimension_semantics=("arbitrary", "arbitrary"),
            vmem_limit_bytes=_ffn_vmem_limit(tm, d, tf, _nbytes((tm, d), F32), 0)),
        name="swiglu_dense",
    )(h, gain_in, w_gate, w_up, w_down, gain_out)


def _expert_ffn_kernel(tile_expert_ref, tile_rows_ref, src_ref, x_hbm, wg_ref, wu_ref, wd_ref, out_ref,
                       u_scr, stage, sem):
    del tile_expert_ref
    i, j = pl.program_id(0), pl.program_id(1)
    tm = stage.shape[0]
    granule = EXPERT_ROW_GRANULE
    n_chunks = tile_rows_ref[i] // granule

    @pl.when(j == 0)
    def _():
        def zero_rows(rows):
            out_ref[rows, :] = jnp.zeros((ROW_CHUNK, out_ref.shape[1]), F32)

        _for_row_chunks(tm, zero_rows)

        def gather_chunk(c, carry):
            base = c * granule

            def start(r, inner):
                pltpu.make_async_copy(x_hbm.at[pl.ds(src_ref[i * tm + base + r], 1), :],
                                      stage.at[pl.ds(base + r, 1), :], sem).start()
                return inner

            return lax.fori_loop(0, granule, start, carry, unroll=DMA_ISSUE_UNROLL)

        lax.fori_loop(0, n_chunks, gather_chunk, 0)

        def wait_chunk(c, carry):
            pltpu.make_async_copy(x_hbm.at[pl.ds(0, granule), :], stage.at[pl.ds(0, granule), :], sem).wait()
            return carry

        lax.fori_loop(0, n_chunks, wait_chunk, 0)

        def unpack_chunk(c, carry):
            rows = pl.ds(pl.multiple_of(c * granule, granule), granule)
            u_scr[rows, :] = _unpack_bf16_pair(stage[rows, :])
            return carry

        lax.fori_loop(0, n_chunks, unpack_chunk, 0)

    for k in range(1, tm // granule + 1):
        @pl.when(n_chunks == k)
        def _(k=k):
            _swiglu_accumulate(u_scr, wg_ref, wu_ref, wd_ref, out_ref, slice(0, k * granule))


def _expert_ffn(xp, src, tile_expert, tile_rows, w_gate, w_up, w_down, *, tm, tf):
    rows = src.shape[0]
    n_exp, d, ff = w_gate.shape
    assert rows % tm == 0 and tm % ROW_CHUNK == 0 and tm % EXPERT_ROW_GRANULE == 0
    assert ff % tf == 0 and xp.shape[1] == d // 2
    nj = ff // tf

    def ff_tile(i, j, tr):
        return jnp.where(tr[i] > 0, j, nj - 1)

    up_blk = pl.BlockSpec((None, d, tf), lambda i, j, te, tr, src: (te[i], 0, ff_tile(i, j, tr)))
    down_blk = pl.BlockSpec((None, tf, d), lambda i, j, te, tr, src: (te[i], ff_tile(i, j, tr), 0))
    return pl.pallas_call(
        _expert_ffn_kernel,
        grid_spec=pltpu.PrefetchScalarGridSpec(
            num_scalar_prefetch=3,
            grid=(rows // tm, nj),
            in_specs=[pl.BlockSpec(memory_space=pl.ANY), up_blk, up_blk, down_blk],
            out_specs=pl.BlockSpec((tm, d), lambda i, j, te, tr, src: (i, 0)),
            scratch_shapes=[pltpu.VMEM((tm, d), BF16), pltpu.VMEM((tm, d // 2), jnp.uint32),
                            pltpu.SemaphoreType.DMA]),
        out_shape=jax.ShapeDtypeStruct((rows, d), F32),
        compiler_params=pltpu.CompilerParams(
            dimension_semantics=("arbitrary", "arbitrary"),
            vmem_limit_bytes=_ffn_vmem_limit(tm, d, tf, 0, _nbytes((tm, d // 2), jnp.uint32))),
        name="swiglu_experts",
    )(tile_expert, tile_rows, src, xp, w_gate, w_up, w_down)


def _router_kernel(h_ref, gain_ref, wrt_ref, xp_ref, idx_ref, wts_ref):
    u = _rms_norm_f32(h_ref[...], gain_ref[...])
    logits = lax.dot_general(wrt_ref[...], u, (((1,), (1,)), ((), ())),
                             precision=lax.Precision.HIGHEST, preferred_element_type=F32)
    n_exp = logits.shape[0]
    expert = lax.broadcasted_iota(jnp.int32, logits.shape, 0)
    v1 = jnp.max(logits, axis=0, keepdims=True)
    i1 = jnp.min(jnp.where(logits == v1, expert, n_exp), axis=0, keepdims=True)
    rest = jnp.where(expert == i1, -jnp.inf, logits)
    v2 = jnp.max(rest, axis=0, keepdims=True)
    i2 = jnp.min(jnp.where(rest == v2, expert, n_exp), axis=0, keepdims=True)
    e2 = jnp.exp(v2 - v1)
    idx_ref[...] = jnp.concatenate([i1, i2], axis=0)
    wts_ref[...] = jnp.concatenate([1.0 / (1.0 + e2), e2 / (1.0 + e2)], axis=0)

    bits = lax.bitcast_convert_type(u.astype(BF16).astype(F32), jnp.uint32)
    half = bits.shape[1] // 2
    xp_ref[...] = (bits[:, :half] >> 16) | bits[:, half:]


def _router(h, gain, w_router_t, *, tm=512):
    n, d = h.shape
    n_exp = w_router_t.shape[0]
    assert n % tm == 0
    limit = _vmem_limit(_nbytes((tm, d), F32) + _nbytes((tm, d // 2), jnp.uint32) + _nbytes((n_exp, d), F32),
                        0, 6 * _nbytes((tm, d), F32))
    return pl.pallas_call(
        _router_kernel,
        grid=(n // tm,),
        in_specs=[pl.BlockSpec((tm, d), lambda i: (i, 0)),
                  pl.BlockSpec((1, d), lambda i: (0, 0)),
                  pl.BlockSpec((n_exp, d), lambda i: (0, 0))],
        out_specs=[pl.BlockSpec((tm, d // 2), lambda i: (i, 0)),
                   pl.BlockSpec((TOP_K, tm), lambda i: (0, i)),
                   pl.BlockSpec((TOP_K, tm), lambda i: (0, i))],
        out_shape=[jax.ShapeDtypeStruct((n, d // 2), jnp.uint32),
                   jax.ShapeDtypeStruct((TOP_K, n), jnp.int32),
                   jax.ShapeDtypeStruct((TOP_K, n), F32)],
        compiler_params=pltpu.CompilerParams(dimension_semantics=("arbitrary",), vmem_limit_bytes=limit),
        name="moe_router",
    )(h, gain, w_router_t)


def _combine_kernel(pos_ref, y_hbm, wts_ref, h_ref, gain_ref, out_ref, bufs, sems, *, tm):
    i = pl.program_id(0)

    def start_tile(tile, slot):
        def start(r, carry):
            for k in range(TOP_K):
                pltpu.make_async_copy(y_hbm.at[pl.ds(pos_ref[TOP_K * (tile * tm + r) + k], 1), :],
                                      bufs.at[slot, k, pl.ds(r, 1), :], sems.at[slot, k]).start()
            return carry

        lax.fori_loop(0, tm, start, 0, unroll=DMA_ISSUE_UNROLL)

    @pl.when(i == 0)
    def _():
        start_tile(0, 0)

    @pl.when(i + 1 < pl.num_programs(0))
    def _():
        start_tile(i + 1, (i + 1) % 2)

    slot = i % 2
    for k in range(TOP_K):
        pltpu.make_async_copy(y_hbm.at[pl.ds(0, tm), :], bufs.at[slot, k], sems.at[slot, k]).wait()
    w = wts_ref[...]
    y = w[:, 0:1] * bufs[slot, 0] + w[:, 1:2] * bufs[slot, 1]
    out_ref[...] = h_ref[...] + _rms_norm_f32(y, gain_ref[...])


def _combine(y_rows, pos, wts, h, gain, *, tm=512):
    n, d = h.shape
    assert n % tm == 0
    limit = _vmem_limit(2 * _nbytes((tm, d), F32) + _nbytes((tm, 128), F32),
                        2 * TOP_K * _nbytes((tm, d), F32), 4 * _nbytes((tm, d), F32))
    return pl.pallas_call(
        functools.partial(_combine_kernel, tm=tm),
        grid_spec=pltpu.PrefetchScalarGridSpec(
            num_scalar_prefetch=1,
            grid=(n // tm,),
            in_specs=[pl.BlockSpec(memory_space=pl.ANY),
                      pl.BlockSpec((tm, TOP_K), lambda i, pos: (i, 0)),
                      pl.BlockSpec((tm, d), lambda i, pos: (i, 0)),
                      pl.BlockSpec((1, d), lambda i, pos: (0, 0))],
            out_specs=pl.BlockSpec((tm, d), lambda i, pos: (i, 0)),
            scratch_shapes=[pltpu.VMEM((2, TOP_K, tm, d), F32), pltpu.SemaphoreType.DMA((2, TOP_K))]),
        out_shape=jax.ShapeDtypeStruct((n, d), F32),
        compiler_params=pltpu.CompilerParams(dimension_semantics=("arbitrary",), vmem_limit_bytes=limit),
        name="moe_combine",
    )(pos, y_rows, wts, h, gain)


def _dispatch_plan(top_idx, n_exp, tm):
    n = top_idx.shape[1]
    n_assign = n * TOP_K
    n_tiles = n_assign // tm + n_exp
    expert = top_idx.T.reshape(n_assign)
    onehot = (expert[:, None] == jnp.arange(n_exp, dtype=jnp.int32)[None, :]).astype(jnp.int32)
    rank = jnp.sum((jnp.cumsum(onehot, axis=0) - onehot) * onehot, axis=1)
    counts = jnp.sum(onehot, axis=0)
    tiles_per_expert = (counts + tm - 1) // tm
    tile_ends = jnp.cumsum(tiles_per_expert)
    tile_starts = tile_ends - tiles_per_expert
    pos = tile_starts[expert] * tm + rank
    src = jnp.zeros((n_tiles * tm,), jnp.int32).at[pos].set(jnp.arange(n_assign, dtype=jnp.int32) // TOP_K)
    tile = jnp.arange(n_tiles, dtype=jnp.int32)
    tile_expert = jnp.sum((tile[:, None] >= tile_ends[None, :]).astype(jnp.int32), axis=1)
    used = tile < tile_ends[-1]
    tile_expert = jnp.where(used, tile_expert, tile_expert[tile_ends[-1] - 1])
    granule = EXPERT_ROW_GRANULE
    rows_left = (counts[tile_expert] + granule - 1) // granule * granule - (tile - tile_starts[tile_expert]) * tm
    tile_rows = jnp.where(used, jnp.clip(rows_left, 0, tm), 0)
    return pos.astype(jnp.int32), src, tile_expert.astype(jnp.int32), tile_rows.astype(jnp.int32)


def _moe(h, gain_in, gain_out, w_router, w_gate, w_up, w_down, *, tm=1280, tf=512):
    n_exp = w_router.shape[1]
    xp, top_idx, top_w = _router(h, gain_in, w_router.T)
    pos, src, tile_expert, tile_rows = _dispatch_plan(top_idx, n_exp, tm)
    y_rows = _expert_ffn(xp, src, tile_expert, tile_rows, w_gate, w_up, w_down, tm=tm, tf=tf)
    return _combine(y_rows, pos, top_w.T, h, gain_out)


def kernel(x, w_in, rel_bias, w_pool, pool_scale, w_attn_out, w_pool_out, w_out, norm_gains,
           dense_w_gate, dense_w_up, dense_w_down, moe_w_router, moe_w_gate, moe_w_up, moe_w_down):
    batch, seq, d = x.shape
    depth = w_in.shape[0]
    attn_width = N_ATTN_GROUPS * GROUP_WIDTH
    h = x.reshape(batch * seq, d)
    biases = [_band_bias(rel_bias[:, g * HEADS_PER_GROUP:(g + 1) * HEADS_PER_GROUP], window, dilation)
              for g, (window, dilation) in enumerate(DILATION_PATTERNS)]
    for layer in range(depth):
        gains = norm_gains[layer].reshape(4, 1, d)
        z = _in_proj(h, gains[0], w_in, layer)
        o_attn = _attention(z, biases, [dilation for _, dilation in DILATION_PATTERNS],
                            batch=batch, seq=seq, attn_width=attn_width)
        h = _merge(h, z, o_attn, w_pool[layer].astype(BF16), pool_scale[layer].reshape(1, -1),
                   w_attn_out[layer].astype(BF16), w_pool_out[layer].astype(BF16), w_out[layer].astype(BF16),
                   gains[1], seq=seq, attn_width=attn_width)
        j = layer // 2
        if layer % 2 == 0:
            h = _dense_ffn(h, gains[2], gains[3], dense_w_gate[j], dense_w_up[j], dense_w_down[j])
        else:
            h = _moe(h, gains[2], gains[3], moe_w_router[j], moe_w_gate[j], moe_w_up[j], moe_w_down[j])
    return h.reshape(batch, seq, d)
```

```python
import functools
import math

import jax
import jax.numpy as jnp
from jax import lax
from jax.experimental import pallas as pl
from jax.experimental.pallas import tpu as pltpu

F32 = jnp.float32
BF16 = jnp.bfloat16

RMS_EPS = 1e-6
HEAD_DIM = 128
Q_BLOCK = 128
DILATION_PATTERNS = ((128, 1), (512, 4), (2048, 16))
N_ATTN_GROUPS = len(DILATION_PATTERNS)
HEADS_PER_GROUP = 4
GROUP_WIDTH = HEADS_PER_GROUP * HEAD_DIM
POOL_SIZES = (2, 4, 8, 16)
POOL_GROUP_WIDTH = 128
POOL_HALO = 16
N_BUCKETS = 32
MAX_DISTANCE = 2048
TOP_K = 2

V7X_VMEM_BYTES = 64 * 1024 * 1024


def _vmem_limit(pipelined_bytes, resident_bytes, temp_bytes):
    return min(2 * pipelined_bytes + resident_bytes + temp_bytes, V7X_VMEM_BYTES)


def _nbytes(shape, dtype):
    return math.prod(shape) * jnp.dtype(dtype).itemsize


def _rms_norm_f32(x, gain):
    ms = jnp.mean(x * x, axis=-1, keepdims=True)
    return x * lax.rsqrt(ms + RMS_EPS) * gain


ROW_CHUNK = 256
EXPERT_ROW_GRANULE = 64
DMA_ISSUE_UNROLL = 16


def _for_row_chunks(n_rows, body):
    assert n_rows % ROW_CHUNK == 0

    def step(c, carry):
        body(pl.ds(pl.multiple_of(c * ROW_CHUNK, ROW_CHUNK), ROW_CHUNK))
        return carry

    lax.fori_loop(0, n_rows // ROW_CHUNK, step, 0)


def _static_row_chunks(n_rows):
    assert n_rows % ROW_CHUNK == 0
    return [slice(c * ROW_CHUNK, (c + 1) * ROW_CHUNK) for c in range(n_rows // ROW_CHUNK)]


def _inproj_kernel(h_ref, g_ref, w_ref, z_ref, u_scr):
    j = pl.program_id(1)

    @pl.when(j == 0)
    def _():
        for rows in _static_row_chunks(u_scr.shape[0]):
            u = _rms_norm_f32(h_ref[rows, :], g_ref[...]).astype(BF16)
            u_scr[rows, :] = u
            z_ref[rows, :] = jnp.dot(u, w_ref[...].astype(BF16), preferred_element_type=F32)

    @pl.when(j > 0)
    def _():
        z_ref[...] = jnp.dot(u_scr[...], w_ref[...].astype(BF16), preferred_element_type=F32)


def _in_proj(h, gain, w_all, layer, *, tm=2048, tn=512):
    n, d = h.shape
    width = w_all.shape[2]
    assert n % tm == 0 and width % tn == 0
    limit = _vmem_limit(
        _nbytes((d, tn), F32) + _nbytes((tm, tn), F32),
        _nbytes((tm, d), F32) + _nbytes((tm, d), BF16),
        _nbytes((tm, d), F32) + _nbytes((d, tn), BF16) + _nbytes((tm, tn), F32))
    return pl.pallas_call(
        _inproj_kernel,
        grid=(n // tm, width // tn),
        in_specs=[pl.BlockSpec((tm, d), lambda i, j: (i, 0), pipeline_mode=pl.Buffered(1)),
                  pl.BlockSpec((1, d), lambda i, j: (0, 0)),
                  pl.BlockSpec((None, d, tn), lambda i, j: (layer, 0, j))],
        out_specs=pl.BlockSpec((tm, tn), lambda i, j: (i, j)),
        out_shape=jax.ShapeDtypeStruct((n, width), F32),
        scratch_shapes=[pltpu.VMEM((tm, d), BF16)],
        compiler_params=pltpu.CompilerParams(dimension_semantics=("arbitrary", "arbitrary"),
                                             vmem_limit_bytes=limit),
        name="in_proj",
    )(h, gain, w_all)


def _group_attention_into(q_ref, k_ref, v_ref, kh_ref, vh_ref, bias_ref, o_scr, l_scr, *, dilation, scale):
    first_chunk = pl.program_id(1) == 0
    n_sub = q_ref.shape[1] // (dilation * Q_BLOCK)

    def stream_rows(start):
        if dilation == 1:
            return pl.ds(start, Q_BLOCK)
        return pl.ds(start, Q_BLOCK, stride=dilation)

    for r in range(dilation):
        k_prev = kh_ref[0, stream_rows(r), :].astype(BF16)
        v_prev = vh_ref[0, stream_rows(r), :].astype(BF16)
        for n in range(n_sub):
            rows = stream_rows(r + n * Q_BLOCK * dilation)
            q = q_ref[0, rows, :].astype(BF16)
            k_cur = k_ref[0, rows, :].astype(BF16)
            v_cur = v_ref[0, rows, :].astype(BF16)
            kc = jnp.concatenate([k_prev, k_cur], axis=0)
            vc = jnp.concatenate([v_prev, v_cur], axis=0)
            s = lax.dot_general(q, kc, (((1,), (1,)), ((), ())), preferred_element_type=F32)
            s = s * scale + bias_ref[0]
            if n == 0:
                key_col = lax.broadcasted_iota(jnp.int32, s.shape, 1)
                s = jnp.where(jnp.logical_and(first_chunk, key_col < Q_BLOCK), -jnp.inf, s)
            m = jnp.max(s, axis=-1, keepdims=True)
            p = jnp.exp(s - m)
            den = jnp.sum(p, axis=-1, keepdims=True)
            o_scr[rows, :] = jnp.dot(p.astype(BF16), vc, preferred_element_type=F32) / den
            l_scr[rows, :] = jnp.broadcast_to(m + jnp.log(den), (Q_BLOCK, HEAD_DIM))
            k_prev, v_prev = k_cur, v_cur


def _attn_kernel(*refs, dilations, scale):
    n_groups = len(dilations)
    group_refs = [refs[6 * g:6 * (g + 1)] for g in range(n_groups)]
    out_ref = refs[6 * n_groups]
    scratch = refs[6 * n_groups + 1:]
    o_scrs, l_scrs = scratch[:n_groups], scratch[n_groups:]
    for g, dilation in enumerate(dilations):
        _group_attention_into(*group_refs[g], o_scrs[g], l_scrs[g], dilation=dilation, scale=scale)

    def mix_rows(rows):
        lses = [l[rows, :] for l in l_scrs]
        m = functools.reduce(jnp.maximum, lses)
        es = [jnp.exp(l - m) for l in lses]
        num = sum(e * o[rows, :] for e, o in zip(es, o_scrs))
        out_ref[0, rows, :] = num / sum(es)

    _for_row_chunks(out_ref.shape[1], mix_rows)


def _t5_causal_bucket(dist):
    max_exact = N_BUCKETS // 2
    df = jnp.maximum(dist, 1).astype(F32)
    large = max_exact + (jnp.log(df / max_exact) / math.log(MAX_DISTANCE / max_exact)
                         * (N_BUCKETS - max_exact)).astype(jnp.int32)
    large = jnp.minimum(large, N_BUCKETS - 1)
    return jnp.where(dist < max_exact, dist, large)


def _band_bias(bias_table, window, dilation):
    span = window // dilation
    qi = jnp.arange(Q_BLOCK)[:, None]
    kj = jnp.arange(2 * Q_BLOCK)[None, :]
    step = qi + Q_BLOCK - kj
    valid = (step >= 0) & (step <= span)
    bucket = _t5_causal_bucket(jnp.clip(step, 0, span) * dilation)
    onehot = (bucket[:, :, None] == jnp.arange(N_BUCKETS)[None, None, :]).astype(F32)
    bias = jnp.einsum("qkb,bh->hqk", onehot, bias_table, precision=lax.Precision.HIGHEST)
    return jnp.where(valid[None], bias, -jnp.inf)


def _attention(z, biases, dilations, *, batch, seq, attn_width, chunk=2048):
    assert seq % chunk == 0 and attn_width % HEAD_DIM == 0
    zv = z.reshape(batch, seq, z.shape[1])
    blk = _nbytes((chunk, HEAD_DIM), F32)
    in_specs, operands, pipelined = [], [], blk
    for group, (dilation, bias) in enumerate(zip(dilations, biases)):
        hist = Q_BLOCK * dilation
        assert chunk % hist == 0
        q_col = group * HEADS_PER_GROUP
        k_col = q_col + attn_width // HEAD_DIM
        v_col = k_col + attn_width // HEAD_DIM

        def cur(col):
            return pl.BlockSpec((1, chunk, HEAD_DIM), lambda b, c, hh, col=col: (b, c, col + hh))

        def prev(col, hist=hist):
            return pl.BlockSpec(
                (1, hist, HEAD_DIM),
                lambda b, c, hh, col=col, hist=hist: (b, jnp.maximum(c * (chunk // hist) - 1, 0), col + hh))

        in_specs += [cur(q_col), cur(k_col), cur(v_col), prev(k_col), prev(v_col),
                     pl.BlockSpec((1,) + bias.shape[1:], lambda b, c, hh: (hh, 0, 0))]
        operands += [zv, zv, zv, zv, zv, bias]
        pipelined += 3 * blk + 2 * _nbytes((hist, HEAD_DIM), F32) + _nbytes(bias.shape[1:], F32)
    n_groups = len(dilations)
    o = pl.pallas_call(
        functools.partial(_attn_kernel, dilations=tuple(dilations), scale=HEAD_DIM ** -0.5),
        grid=(batch, seq // chunk, HEADS_PER_GROUP),
        in_specs=in_specs,
        out_specs=pl.BlockSpec((1, chunk, HEAD_DIM), lambda b, c, hh: (b, c, hh)),
        out_shape=jax.ShapeDtypeStruct((batch, seq, GROUP_WIDTH), F32),
        scratch_shapes=[pltpu.VMEM((chunk, HEAD_DIM), F32)] * (2 * n_groups),
        compiler_params=pltpu.CompilerParams(
            dimension_semantics=("arbitrary",) * 3,
            vmem_limit_bytes=_vmem_limit(pipelined, 2 * n_groups * blk, 4 * blk)),
        name="dilated_attention",
    )(*operands)
    return o.reshape(batch * seq, GROUP_WIDTH)


def _merge_kernel(oattn_ref, p_ref, pp_ref, ga0_ref, ga1_ref, gb0_ref, gb1_ref, h_ref,
                  wpool_ref, pscale_ref, wao_ref, wpo_ref, wout_ref, gain_ref, out_ref,
                  *, tm, tiles_per_seq):
    tile_in_seq = pl.program_id(0) % tiles_per_seq

    halo = jnp.where(tile_in_seq == 0, 0.0, pp_ref[...])
    xe = jnp.concatenate([halo, p_ref[...]], axis=0)
    t = tile_in_seq * tm + lax.broadcasted_iota(jnp.int32, (tm, 1), 0)
    pooled = []
    for g, size in enumerate(POOL_SIZES):
        a = xe[:, g * POOL_GROUP_WIDTH:(g + 1) * POOL_GROUP_WIDTH]
        s, shift = a, 1
        while shift < size:
            s = s + pltpu.roll(s, shift, axis=0)
            shift *= 2
        count = jnp.minimum(t + 1, size).astype(F32)
        y = (s[POOL_HALO:] / count - a[POOL_HALO:]).astype(BF16)
        pooled.append(jnp.dot(y, wpool_ref[g], preferred_element_type=F32))
    o_pool = jnp.concatenate(pooled, axis=1) * pscale_ref[...]

    attn_proj = jnp.dot(oattn_ref[...].astype(BF16), wao_ref[...], preferred_element_type=F32)
    pool_proj = jnp.dot(o_pool.astype(BF16), wpo_ref[...], preferred_element_type=F32)
    gate_a = jax.nn.sigmoid(jnp.concatenate([ga0_ref[...], ga1_ref[...]], axis=1))
    gate_b = jax.nn.sigmoid(jnp.concatenate([gb0_ref[...], gb1_ref[...]], axis=1))
    merged = gate_a * attn_proj + gate_b * pool_proj
    mix = jnp.dot(merged.astype(BF16), wout_ref[...], preferred_element_type=F32)
    out_ref[...] = h_ref[...] + _rms_norm_f32(mix, gain_ref[...])


def _merge(h, z, o_attn, w_pool, pool_scale, w_attn_out, w_pool_out, w_out, gain, *, seq, attn_width, tm=512):
    n, d = h.shape
    pool_width = w_pool.shape[0] * POOL_GROUP_WIDTH
    assert pool_width == GROUP_WIDTH
    assert n % tm == 0 and seq % tm == 0 and tm % POOL_HALO == 0
    p_col = 3 * attn_width // GROUP_WIDTH
    gate_w = d // 2
    gate_col = (3 * attn_width + pool_width) // gate_w
    assert (3 * attn_width + pool_width) % gate_w == 0

    row_blk = lambda c: pl.BlockSpec((tm, GROUP_WIDTH), lambda i: (i, c))
    gate_blk = lambda c: pl.BlockSpec((tm, gate_w), lambda i: (i, gate_col + c))
    halo_blk = pl.BlockSpec((POOL_HALO, GROUP_WIDTH),
                            lambda i: (jnp.maximum(i * (tm // POOL_HALO) - 1, 0), p_col))
    full = lambda a: pl.BlockSpec(a.shape, lambda i: (0,) * a.ndim)
    weights = (w_pool, pool_scale, w_attn_out, w_pool_out, w_out, gain)

    act = 2 * _nbytes((tm, GROUP_WIDTH), F32) + 4 * _nbytes((tm, gate_w), F32) + 2 * _nbytes((tm, d), F32)
    wbytes = sum(_nbytes(a.shape, a.dtype) for a in weights)
    limit = _vmem_limit(act + wbytes, 0, 8 * _nbytes((tm, d), F32))
    return pl.pallas_call(
        functools.partial(_merge_kernel, tm=tm, tiles_per_seq=seq // tm),
        grid=(n // tm,),
        in_specs=[row_blk(0), row_blk(p_col), halo_blk] + [gate_blk(c) for c in range(4)]
                 + [pl.BlockSpec((tm, d), lambda i: (i, 0))] + [full(a) for a in weights],
        out_specs=pl.BlockSpec((tm, d), lambda i: (i, 0)),
        out_shape=jax.ShapeDtypeStruct((n, d), F32),
        compiler_params=pltpu.CompilerParams(dimension_semantics=("arbitrary",), vmem_limit_bytes=limit),
        name="mixer_merge",
    )(o_attn, z, z, z, z, z, z, h, *weights)


def _unpack_bf16_pair(words):
    lo = lax.bitcast_convert_type(words << 16, F32)
    hi = lax.bitcast_convert_type(words & jnp.uint32(0xFFFF0000), F32)
    return jnp.concatenate([lo, hi], axis=1).astype(BF16)


def _swiglu(u, wg_ref, wu_ref, wd_ref):
    gate = jnp.dot(u, wg_ref[...].astype(BF16), preferred_element_type=F32)
    up = jnp.dot(u, wu_ref[...].astype(BF16), preferred_element_type=F32)
    hidden = (gate * jax.nn.sigmoid(gate) * up).astype(BF16)
    return jnp.dot(hidden, wd_ref[...].astype(BF16), preferred_element_type=F32)


def _swiglu_accumulate(u_scr, wg_ref, wu_ref, wd_ref, acc_ref, rows=slice(None)):
    acc_ref[rows, :] += _swiglu(u_scr[rows, :], wg_ref, wu_ref, wd_ref)


def _ffn_vmem_limit(tm, d, tf, x_bytes, scratch_bytes):
    return _vmem_limit(
        x_bytes + 3 * _nbytes((d, tf), F32) + _nbytes((tm, d), F32),
        _nbytes((tm, d), BF16) + scratch_bytes,
        3 * _nbytes((d, tf), BF16) + 4 * _nbytes((tm, tf), F32) + 2 * _nbytes((tm, d), F32))


def _dense_ffn_kernel(h_ref, gin_ref, wg_ref, wu_ref, wd_ref, gout_ref, out_ref, u_scr):
    j = pl.program_id(1)
    last = pl.num_programs(1) - 1
    chunks = _static_row_chunks(u_scr.shape[0])

    @pl.when(j == 0)
    def _():
        for rows in chunks:
            u = _rms_norm_f32(h_ref[rows, :], gin_ref[...]).astype(BF16)
            u_scr[rows, :] = u
            out_ref[rows, :] = _swiglu(u, wg_ref, wu_ref, wd_ref)

    @pl.when(jnp.logical_and(j > 0, j < last))
    def _():
        _swiglu_accumulate(u_scr, wg_ref, wu_ref, wd_ref, out_ref)

    @pl.when(j == last)
    def _():
        for rows in chunks:
            f = out_ref[rows, :] + _swiglu(u_scr[rows, :], wg_ref, wu_ref, wd_ref)
            out_ref[rows, :] = h_ref[rows, :] + _rms_norm_f32(f, gout_ref[...])


def _dense_ffn(h, gain_in, gain_out, w_gate, w_up, w_down, *, tm=1024, tf=256):
    n, d = h.shape
    ff = w_gate.shape[1]
    assert n % tm == 0 and ff % tf == 0 and ff // tf >= 2
    row_blk = pl.BlockSpec((tm, d), lambda i, j: (i, 0))
    gain_blk = pl.BlockSpec((1, d), lambda i, j: (0, 0))
    up_blk = pl.BlockSpec((d, tf), lambda i, j: (0, j))
    return pl.pallas_call(
        _dense_ffn_kernel,
        grid=(n // tm, ff // tf),
        in_specs=[row_blk, gain_blk, up_blk, up_blk, pl.BlockSpec((tf, d), lambda i, j: (j, 0)), gain_blk],
        out_specs=row_blk,
        out_shape=jax.ShapeDtypeStruct((n, d), F32),
        scratch_shapes=[pltpu.VMEM((tm, d), BF16)],
        compiler_params=pltpu.CompilerParams(
            dimension_semantics=("arbitrary", "arbitrary"),
            vmem_limit_bytes=_ffn_vmem_limit(tm, d, tf, _nbytes((tm, d), F32), 0)),
        name="swiglu_dense",
    )(h, gain_in, w_gate, w_up, w_down, gain_out)


def _expert_ffn_kernel(tile_expert_ref, tile_rows_ref, src_ref, x_hbm, wg_ref, wu_ref, wd_ref, out_ref,
                       u_scr, stage, sem):
    del tile_expert_ref
    i, j = pl.program_id(0), pl.program_id(1)
    tm = stage.shape[0]
    granule = EXPERT_ROW_GRANULE
    n_chunks = tile_rows_ref[i] // granule

    def start_gather(tile):
        def gather_chunk(c, carry):
            base = c * granule

            def start(r, inner):
                pltpu.make_async_copy(x_hbm.at[pl.ds(src_ref[tile * tm + base + r], 1), :],
                                      stage.at[pl.ds(base + r, 1), :], sem).start()
                return inner

            return lax.fori_loop(0, granule, start, carry, unroll=DMA_ISSUE_UNROLL)

        lax.fori_loop(0, tile_rows_ref[tile] // granule, gather_chunk, 0)

    @pl.when(jnp.logical_and(i == 0, j == 0))
    def _():
        start_gather(0)

    @pl.when(jnp.logical_and(j == 1, i + 1 < pl.num_programs(0)))
    def _():
        start_gather(i + 1)

    @pl.when(j == 0)
    def _():
        def zero_rows(rows):
            out_ref[rows, :] = jnp.zeros((ROW_CHUNK, out_ref.shape[1]), F32)

        _for_row_chunks(tm, zero_rows)

        def wait_chunk(c, carry):
            pltpu.make_async_copy(x_hbm.at[pl.ds(0, granule), :], stage.at[pl.ds(0, granule), :], sem).wait()
            return carry

        lax.fori_loop(0, n_chunks, wait_chunk, 0)

        def unpack_chunk(c, carry):
            rows = pl.ds(pl.multiple_of(c * granule, granule), granule)
            u_scr[rows, :] = _unpack_bf16_pair(stage[rows, :])
            return carry

        lax.fori_loop(0, n_chunks, unpack_chunk, 0)

    for k in range(1, tm // granule + 1):
        @pl.when(n_chunks == k)
        def _(k=k):
            _swiglu_accumulate(u_scr, wg_ref, wu_ref, wd_ref, out_ref, slice(0, k * granule))


def _expert_ffn(xp, src, tile_expert, tile_rows, w_gate, w_up, w_down, *, tm, tf):
    rows = src.shape[0]
    n_exp, d, ff = w_gate.shape
    assert rows % tm == 0 and tm % ROW_CHUNK == 0 and tm % EXPERT_ROW_GRANULE == 0
    assert ff % tf == 0 and xp.shape[1] == d // 2
    nj = ff // tf
    assert nj >= 2

    def ff_tile(i, j, tr):
        return jnp.where(tr[i] > 0, j, nj - 1)

    up_blk = pl.BlockSpec((None, d, tf), lambda i, j, te, tr, src: (te[i], 0, ff_tile(i, j, tr)))
    down_blk = pl.BlockSpec((None, tf, d), lambda i, j, te, tr, src: (te[i], ff_tile(i, j, tr), 0))
    return pl.pallas_call(
        _expert_ffn_kernel,
        grid_spec=pltpu.PrefetchScalarGridSpec(
            num_scalar_prefetch=3,
            grid=(rows // tm, nj),
            in_specs=[pl.BlockSpec(memory_space=pl.ANY), up_blk, up_blk, down_blk],
            out_specs=pl.BlockSpec((tm, d), lambda i, j, te, tr, src: (i, 0)),
            scratch_shapes=[pltpu.VMEM((tm, d), BF16), pltpu.VMEM((tm, d // 2), jnp.uint32),
                            pltpu.SemaphoreType.DMA]),
        out_shape=jax.ShapeDtypeStruct((rows, d), F32),
        compiler_params=pltpu.CompilerParams(
            dimension_semantics=("arbitrary", "arbitrary"),
            vmem_limit_bytes=_ffn_vmem_limit(tm, d, tf, 0, _nbytes((tm, d // 2), jnp.uint32))),
        name="swiglu_experts",
    )(tile_expert, tile_rows, src, xp, w_gate, w_up, w_down)


def _router_kernel(h_ref, gain_ref, wrt_ref, xp_ref, idx_ref, wts_ref):
    u = _rms_norm_f32(h_ref[...], gain_ref[...])
    logits = lax.dot_general(wrt_ref[...], u, (((1,), (1,)), ((), ())),
                             precision=lax.Precision.HIGHEST, preferred_element_type=F32)
    n_exp = logits.shape[0]
    expert = lax.broadcasted_iota(jnp.int32, logits.shape, 0)
    v1 = jnp.max(logits, axis=0, keepdims=True)
    i1 = jnp.min(jnp.where(logits == v1, expert, n_exp), axis=0, keepdims=True)
    rest = jnp.where(expert == i1, -jnp.inf, logits)
    v2 = jnp.max(rest, axis=0, keepdims=True)
    i2 = jnp.min(jnp.where(rest == v2, expert, n_exp), axis=0, keepdims=True)
    e2 = jnp.exp(v2 - v1)
    idx_ref[...] = jnp.concatenate([i1, i2], axis=0)
    wts_ref[...] = jnp.concatenate([1.0 / (1.0 + e2), e2 / (1.0 + e2)], axis=0)

    bits = lax.bitcast_convert_type(u.astype(BF16).astype(F32), jnp.uint32)
    half = bits.shape[1] // 2
    xp_ref[...] = (bits[:, :half] >> 16) | bits[:, half:]


def _router(h, gain, w_router_t, *, tm=512):
    n, d = h.shape
    n_exp = w_router_t.shape[0]
    assert n % tm == 0
    limit = _vmem_limit(_nbytes((tm, d), F32) + _nbytes((tm, d // 2), jnp.uint32) + _nbytes((n_exp, d), F32),
                        0, 6 * _nbytes((tm, d), F32))
    return pl.pallas_call(
        _router_kernel,
        grid=(n // tm,),
        in_specs=[pl.BlockSpec((tm, d), lambda i: (i, 0)),
                  pl.BlockSpec((1, d), lambda i: (0, 0)),
                  pl.BlockSpec((n_exp, d), lambda i: (0, 0))],
        out_specs=[pl.BlockSpec((tm, d // 2), lambda i: (i, 0)),
                   pl.BlockSpec((TOP_K, tm), lambda i: (0, i)),
                   pl.BlockSpec((TOP_K, tm), lambda i: (0, i))],
        out_shape=[jax.ShapeDtypeStruct((n, d // 2), jnp.uint32),
                   jax.ShapeDtypeStruct((TOP_K, n), jnp.int32),
                   jax.ShapeDtypeStruct((TOP_K, n), F32)],
        compiler_params=pltpu.CompilerParams(dimension_semantics=("arbitrary",), vmem_limit_bytes=limit),
        name="moe_router",
    )(h, gain, w_router_t)


def _combine_kernel(pos_ref, y_hbm, wts_ref, h_ref, gain_ref, out_ref, bufs, sems, *, tm):
    i = pl.program_id(0)

    def start_tile(tile, slot):
        def start(r, carry):
            for k in range(TOP_K):
                pltpu.make_async_copy(y_hbm.at[pl.ds(pos_ref[TOP_K * (tile * tm + r) + k], 1), :],
                                      bufs.at[slot, k, pl.ds(r, 1), :], sems.at[slot, k]).start()
            return carry

        lax.fori_loop(0, tm, start, 0, unroll=DMA_ISSUE_UNROLL)

    @pl.when(i == 0)
    def _():
        start_tile(0, 0)

    @pl.when(i + 1 < pl.num_programs(0))
    def _():
        start_tile(i + 1, (i + 1) % 2)

    slot = i % 2
    for k in range(TOP_K):
        pltpu.make_async_copy(y_hbm.at[pl.ds(0, tm), :], bufs.at[slot, k], sems.at[slot, k]).wait()
    w = wts_ref[...]
    y = w[:, 0:1] * bufs[slot, 0] + w[:, 1:2] * bufs[slot, 1]
    out_ref[...] = h_ref[...] + _rms_norm_f32(y, gain_ref[...])


def _combine(y_rows, pos, wts, h, gain, *, tm=512):
    n, d = h.shape
    assert n % tm == 0
    limit = _vmem_limit(2 * _nbytes((tm, d), F32) + _nbytes((tm, 128), F32),
                        2 * TOP_K * _nbytes((tm, d), F32), 4 * _nbytes((tm, d), F32))
    return pl.pallas_call(
        functools.partial(_combine_kernel, tm=tm),
        grid_spec=pltpu.PrefetchScalarGridSpec(
            num_scalar_prefetch=1,
            grid=(n // tm,),
            in_specs=[pl.BlockSpec(memory_space=pl.ANY),
                      pl.BlockSpec((tm, TOP_K), lambda i, pos: (i, 0)),
                      pl.BlockSpec((tm, d), lambda i, pos: (i, 0)),
                      pl.BlockSpec((1, d), lambda i, pos: (0, 0))],
            out_specs=pl.BlockSpec((tm, d), lambda i, pos: (i, 0)),
            scratch_shapes=[pltpu.VMEM((2, TOP_K, tm, d), F32), pltpu.SemaphoreType.DMA((2, TOP_K))]),
        out_shape=jax.ShapeDtypeStruct((n, d), F32),
        compiler_params=pltpu.CompilerParams(dimension_semantics=("arbitrary",), vmem_limit_bytes=limit),
        name="moe_combine",
    )(pos, y_rows, wts, h, gain)


def _dispatch_plan(top_idx, n_exp, tm):
    n = top_idx.shape[1]
    n_assign = n * TOP_K
    n_tiles = n_assign // tm + n_exp
    expert = top_idx.T.reshape(n_assign)
    onehot = (expert[:, None] == jnp.arange(n_exp, dtype=jnp.int32)[None, :]).astype(jnp.int32)
    rank = jnp.sum((jnp.cumsum(onehot, axis=0) - onehot) * onehot, axis=1)
    counts = jnp.sum(onehot, axis=0)
    tiles_per_expert = (counts + tm - 1) // tm
    tile_ends = jnp.cumsum(tiles_per_expert)
    tile_starts = tile_ends - tiles_per_expert
    pos = tile_starts[expert] * tm + rank
    src = jnp.zeros((n_tiles * tm,), jnp.int32).at[pos].set(jnp.arange(n_assign, dtype=jnp.int32) // TOP_K)
    tile = jnp.arange(n_tiles, dtype=jnp.int32)
    tile_expert = jnp.sum((tile[:, None] >= tile_ends[None, :]).astype(jnp.int32), axis=1)
    used = tile < tile_ends[-1]
    tile_expert = jnp.where(used, tile_expert, tile_expert[tile_ends[-1] - 1])
    granule = EXPERT_ROW_GRANULE
    rows_left = (counts[tile_expert] + granule - 1) // granule * granule - (tile - tile_starts[tile_expert]) * tm
    tile_rows = jnp.where(used, jnp.clip(rows_left, 0, tm), 0)
    return pos.astype(jnp.int32), src, tile_expert.astype(jnp.int32), tile_rows.astype(jnp.int32)


def _moe(h, gain_in, gain_out, w_router, w_gate, w_up, w_down, *, tm=1280, tf=512):
    n_exp = w_router.shape[1]
    xp, top_idx, top_w = _router(h, gain_in, w_router.T)
    pos, src, tile_expert, tile_rows = _dispatch_plan(top_idx, n_exp, tm)
    y_rows = _expert_ffn(xp, src, tile_expert, tile_rows, w_gate, w_up, w_down, tm=tm, tf=tf)
    return _combine(y_rows, pos, top_w.T, h, gain_out)


def kernel(x, w_in, rel_bias, w_pool, pool_scale, w_attn_out, w_pool_out, w_out, norm_gains,
           dense_w_gate, dense_w_up, dense_w_down, moe_w_router, moe_w_gate, moe_w_up, moe_w_down):
    batch, seq, d = x.shape
    depth = w_in.shape[0]
    attn_width = N_ATTN_GROUPS * GROUP_WIDTH
    h = x.reshape(batch * seq, d)
    biases = [_band_bias(rel_bias[:, g * HEADS_PER_GROUP:(g + 1) * HEADS_PER_GROUP], window, dilation)
              for g, (window, dilation) in enumerate(DILATION_PATTERNS)]
    for layer in range(depth):
        gains = norm_gains[layer].reshape(4, 1, d)
        z = _in_proj(h, gains[0], w_in, layer)
        o_attn = _attention(z, biases, [dilation for _, dilation in DILATION_PATTERNS],
                            batch=batch, seq=seq, attn_width=attn_width)
        h = _merge(h, z, o_attn, w_pool[layer].astype(BF16), pool_scale[layer].reshape(1, -1),
                   w_attn_out[layer].astype(BF16), w_pool_out[layer].astype(BF16), w_out[layer].astype(BF16),
                   gains[1], seq=seq, attn_width=attn_width)
        j = layer // 2
        if layer % 2 == 0:
            h = _dense_ffn(h, gains[2], gains[3], dense_w_gate[j], dense_w_up[j], dense_w_down[j])
        else:
            h = _moe(h, gains[2], gains[3], moe_w_router[j], moe_w_gate[j], moe_w_up[j], moe_w_down[j])
    return h.reshape(batch, seq, d)
```

```python
import functools
import math

import jax
import jax.numpy as jnp
from jax import lax
from jax.experimental import pallas as pl
from jax.experimental.pallas import tpu as pltpu

F32 = jnp.float32
BF16 = jnp.bfloat16

RMS_EPS = 1e-6
HEAD_DIM = 128
Q_BLOCK = 128
DILATION_PATTERNS = ((128, 1), (512, 4), (2048, 16))
N_ATTN_GROUPS = len(DILATION_PATTERNS)
HEADS_PER_GROUP = 4
GROUP_WIDTH = HEADS_PER_GROUP * HEAD_DIM
POOL_SIZES = (2, 4, 8, 16)
POOL_GROUP_WIDTH = 128
POOL_HALO = 16
N_BUCKETS = 32
MAX_DISTANCE = 2048
TOP_K = 2

V7X_VMEM_BYTES = 64 * 1024 * 1024


def _vmem_limit(pipelined_bytes, resident_bytes, temp_bytes):
    return min(2 * pipelined_bytes + resident_bytes + temp_bytes, V7X_VMEM_BYTES)


def _nbytes(shape, dtype):
    return math.prod(shape) * jnp.dtype(dtype).itemsize


def _rms_norm_f32(x, gain):
    ms = jnp.mean(x * x, axis=-1, keepdims=True)
    return x * lax.rsqrt(ms + RMS_EPS) * gain


ROW_CHUNK = 256
EXPERT_ROW_GRANULE = 128
DMA_ISSUE_UNROLL = 16


def _for_row_chunks(n_rows, body):
    assert n_rows % ROW_CHUNK == 0

    def step(c, carry):
        body(pl.ds(pl.multiple_of(c * ROW_CHUNK, ROW_CHUNK), ROW_CHUNK))
        return carry

    lax.fori_loop(0, n_rows // ROW_CHUNK, step, 0)


def _static_row_chunks(n_rows):
    assert n_rows % ROW_CHUNK == 0
    return [slice(c * ROW_CHUNK, (c + 1) * ROW_CHUNK) for c in range(n_rows // ROW_CHUNK)]


def _inproj_kernel(h_ref, g_ref, w_ref, z_ref, u_scr):
    j = pl.program_id(1)

    @pl.when(j == 0)
    def _():
        for rows in _static_row_chunks(u_scr.shape[0]):
            u = _rms_norm_f32(h_ref[rows, :], g_ref[...]).astype(BF16)
            u_scr[rows, :] = u
            z_ref[rows, :] = jnp.dot(u, w_ref[...].astype(BF16), preferred_element_type=F32)

    @pl.when(j > 0)
    def _():
        z_ref[...] = jnp.dot(u_scr[...], w_ref[...].astype(BF16), preferred_element_type=F32)


def _in_proj(h, gain, w_all, layer, *, tm=2048, tn=512):
    n, d = h.shape
    width = w_all.shape[2]
    assert n % tm == 0 and width % tn == 0
    limit = _vmem_limit(
        _nbytes((d, tn), F32) + _nbytes((tm, tn), F32),
        _nbytes((tm, d), F32) + _nbytes((tm, d), BF16),
        _nbytes((tm, d), F32) + _nbytes((d, tn), BF16) + _nbytes((tm, tn), F32))
    return pl.pallas_call(
        _inproj_kernel,
        grid=(n // tm, width // tn),
        in_specs=[pl.BlockSpec((tm, d), lambda i, j: (i, 0), pipeline_mode=pl.Buffered(1)),
                  pl.BlockSpec((1, d), lambda i, j: (0, 0)),
                  pl.BlockSpec((None, d, tn), lambda i, j: (layer, 0, j))],
        out_specs=pl.BlockSpec((tm, tn), lambda i, j: (i, j)),
        out_shape=jax.ShapeDtypeStruct((n, width), F32),
        scratch_shapes=[pltpu.VMEM((tm, d), BF16)],
        compiler_params=pltpu.CompilerParams(dimension_semantics=("arbitrary", "arbitrary"),
                                             vmem_limit_bytes=limit),
        name="in_proj",
    )(h, gain, w_all)


def _group_attention_into(q_ref, k_ref, v_ref, kh_ref, vh_ref, bias_ref, o_scr, l_scr, *, dilation, scale):
    first_chunk = pl.program_id(1) == 0
    n_sub = q_ref.shape[1] // (dilation * Q_BLOCK)

    def stream_rows(start):
        if dilation == 1:
            return pl.ds(start, Q_BLOCK)
        return pl.ds(start, Q_BLOCK, stride=dilation)

    for r in range(dilation):
        k_prev = kh_ref[0, stream_rows(r), :].astype(BF16)
        v_prev = vh_ref[0, stream_rows(r), :].astype(BF16)
        for n in range(n_sub):
            rows = stream_rows(r + n * Q_BLOCK * dilation)
            q = q_ref[0, rows, :].astype(BF16)
            k_cur = k_ref[0, rows, :].astype(BF16)
            v_cur = v_ref[0, rows, :].astype(BF16)
            kc = jnp.concatenate([k_prev, k_cur], axis=0)
            vc = jnp.concatenate([v_prev, v_cur], axis=0)
            s = lax.dot_general(q, kc, (((1,), (1,)), ((), ())), preferred_element_type=F32)
            s = s * scale + bias_ref[0]
            if n == 0:
                key_col = lax.broadcasted_iota(jnp.int32, s.shape, 1)
                s = jnp.where(jnp.logical_and(first_chunk, key_col < Q_BLOCK), -jnp.inf, s)
            m = jnp.max(s, axis=-1, keepdims=True)
            p = jnp.exp(s - m)
            den = jnp.sum(p, axis=-1, keepdims=True)
            o_scr[rows, :] = jnp.dot(p.astype(BF16), vc, preferred_element_type=F32) / den
            l_scr[rows, :] = jnp.broadcast_to(m + jnp.log(den), (Q_BLOCK, HEAD_DIM))
            k_prev, v_prev = k_cur, v_cur


def _attn_kernel(*refs, dilations, scale):
    n_groups = len(dilations)
    group_refs = [refs[6 * g:6 * (g + 1)] for g in range(n_groups)]
    out_ref = refs[6 * n_groups]
    scratch = refs[6 * n_groups + 1:]
    o_scrs, l_scrs = scratch[:n_groups], scratch[n_groups:]
    for g, dilation in enumerate(dilations):
        _group_attention_into(*group_refs[g], o_scrs[g], l_scrs[g], dilation=dilation, scale=scale)

    def mix_rows(rows):
        lses = [l[rows, :] for l in l_scrs]
        m = functools.reduce(jnp.maximum, lses)
        es = [jnp.exp(l - m) for l in lses]
        num = sum(e * o[rows, :] for e, o in zip(es, o_scrs))
        out_ref[0, rows, :] = num / sum(es)

    _for_row_chunks(out_ref.shape[1], mix_rows)


def _t5_causal_bucket(dist):
    max_exact = N_BUCKETS // 2
    df = jnp.maximum(dist, 1).astype(F32)
    large = max_exact + (jnp.log(df / max_exact) / math.log(MAX_DISTANCE / max_exact)
                         * (N_BUCKETS - max_exact)).astype(jnp.int32)
    large = jnp.minimum(large, N_BUCKETS - 1)
    return jnp.where(dist < max_exact, dist, large)


def _band_bias(bias_table, window, dilation):
    span = window // dilation
    qi = jnp.arange(Q_BLOCK)[:, None]
    kj = jnp.arange(2 * Q_BLOCK)[None, :]
    step = qi + Q_BLOCK - kj
    valid = (step >= 0) & (step <= span)
    bucket = _t5_causal_bucket(jnp.clip(step, 0, span) * dilation)
    onehot = (bucket[:, :, None] == jnp.arange(N_BUCKETS)[None, None, :]).astype(F32)
    bias = jnp.einsum("qkb,bh->hqk", onehot, bias_table, precision=lax.Precision.HIGHEST)
    return jnp.where(valid[None], bias, -jnp.inf)


def _attention(z, biases, dilations, *, batch, seq, attn_width, chunk=2048):
    assert seq % chunk == 0 and attn_width % HEAD_DIM == 0
    zv = z.reshape(batch, seq, z.shape[1])
    blk = _nbytes((chunk, HEAD_DIM), F32)
    in_specs, operands, pipelined = [], [], blk
    for group, (dilation, bias) in enumerate(zip(dilations, biases)):
        hist = Q_BLOCK * dilation
        assert chunk % hist == 0
        q_col = group * HEADS_PER_GROUP
        k_col = q_col + attn_width // HEAD_DIM
        v_col = k_col + attn_width // HEAD_DIM

        def cur(col):
            return pl.BlockSpec((1, chunk, HEAD_DIM), lambda b, c, hh, col=col: (b, c, col + hh))

        def prev(col, hist=hist):
            return pl.BlockSpec(
                (1, hist, HEAD_DIM),
                lambda b, c, hh, col=col, hist=hist: (b, jnp.maximum(c * (chunk // hist) - 1, 0), col + hh))

        in_specs += [cur(q_col), cur(k_col), cur(v_col), prev(k_col), prev(v_col),
                     pl.BlockSpec((1,) + bias.shape[1:], lambda b, c, hh: (hh, 0, 0))]
        operands += [zv, zv, zv, zv, zv, bias]
        pipelined += 3 * blk + 2 * _nbytes((hist, HEAD_DIM), F32) + _nbytes(bias.shape[1:], F32)
    n_groups = len(dilations)
    o = pl.pallas_call(
        functools.partial(_attn_kernel, dilations=tuple(dilations), scale=HEAD_DIM ** -0.5),
        grid=(batch, seq // chunk, HEADS_PER_GROUP),
        in_specs=in_specs,
        out_specs=pl.BlockSpec((1, chunk, HEAD_DIM), lambda b, c, hh: (b, c, hh)),
        out_shape=jax.ShapeDtypeStruct((batch, seq, GROUP_WIDTH), F32),
        scratch_shapes=[pltpu.VMEM((chunk, HEAD_DIM), F32)] * (2 * n_groups),
        compiler_params=pltpu.CompilerParams(
            dimension_semantics=("arbitrary",) * 3,
            vmem_limit_bytes=_vmem_limit(pipelined, 2 * n_groups * blk, 4 * blk)),
        name="dilated_attention",
    )(*operands)
    return o.reshape(batch * seq, GROUP_WIDTH)


def _merge_kernel(oattn_ref, p_ref, pp_ref, ga0_ref, ga1_ref, gb0_ref, gb1_ref, h_ref,
                  wpool_ref, pscale_ref, wao_ref, wpo_ref, wout_ref, gain_ref, out_ref,
                  *, tm, tiles_per_seq):
    tile_in_seq = pl.program_id(0) % tiles_per_seq

    halo = jnp.where(tile_in_seq == 0, 0.0, pp_ref[...])
    xe = jnp.concatenate([halo, p_ref[...]], axis=0)
    t = tile_in_seq * tm + lax.broadcasted_iota(jnp.int32, (tm, 1), 0)
    pooled = []
    for g, size in enumerate(POOL_SIZES):
        a = xe[:, g * POOL_GROUP_WIDTH:(g + 1) * POOL_GROUP_WIDTH]
        s, shift = a, 1
        while shift < size:
            s = s + pltpu.roll(s, shift, axis=0)
            shift *= 2
        count = jnp.minimum(t + 1, size).astype(F32)
        y = (s[POOL_HALO:] / count - a[POOL_HALO:]).astype(BF16)
        pooled.append(jnp.dot(y, wpool_ref[g], preferred_element_type=F32))
    o_pool = jnp.concatenate(pooled, axis=1) * pscale_ref[...]

    attn_proj = jnp.dot(oattn_ref[...].astype(BF16), wao_ref[...], preferred_element_type=F32)
    pool_proj = jnp.dot(o_pool.astype(BF16), wpo_ref[...], preferred_element_type=F32)
    gate_a = jax.nn.sigmoid(jnp.concatenate([ga0_ref[...], ga1_ref[...]], axis=1))
    gate_b = jax.nn.sigmoid(jnp.concatenate([gb0_ref[...], gb1_ref[...]], axis=1))
    merged = gate_a * attn_proj + gate_b * pool_proj
    mix = jnp.dot(merged.astype(BF16), wout_ref[...], preferred_element_type=F32)
    out_ref[...] = h_ref[...] + _rms_norm_f32(mix, gain_ref[...])


def _merge(h, z, o_attn, w_pool, pool_scale, w_attn_out, w_pool_out, w_out, gain, *, seq, attn_width, tm=512):
    n, d = h.shape
    pool_width = w_pool.shape[0] * POOL_GROUP_WIDTH
    assert pool_width == GROUP_WIDTH
    assert n % tm == 0 and seq % tm == 0 and tm % POOL_HALO == 0
    p_col = 3 * attn_width // GROUP_WIDTH
    gate_w = d // 2
    gate_col = (3 * attn_width + pool_width) // gate_w
    assert (3 * attn_width + pool_width) % gate_w == 0

    row_blk = lambda c: pl.BlockSpec((tm, GROUP_WIDTH), lambda i: (i, c))
    gate_blk = lambda c: pl.BlockSpec((tm, gate_w), lambda i: (i, gate_col + c))
    halo_blk = pl.BlockSpec((POOL_HALO, GROUP_WIDTH),
                            lambda i: (jnp.maximum(i * (tm // POOL_HALO) - 1, 0), p_col))
    full = lambda a: pl.BlockSpec(a.shape, lambda i: (0,) * a.ndim)
    weights = (w_pool, pool_scale, w_attn_out, w_pool_out, w_out, gain)

    act = 2 * _nbytes((tm, GROUP_WIDTH), F32) + 4 * _nbytes((tm, gate_w), F32) + 2 * _nbytes((tm, d), F32)
    wbytes = sum(_nbytes(a.shape, a.dtype) for a in weights)
    limit = _vmem_limit(act + wbytes, 0, 8 * _nbytes((tm, d), F32))
    return pl.pallas_call(
        functools.partial(_merge_kernel, tm=tm, tiles_per_seq=seq // tm),
        grid=(n // tm,),
        in_specs=[row_blk(0), row_blk(p_col), halo_blk] + [gate_blk(c) for c in range(4)]
                 + [pl.BlockSpec((tm, d), lambda i: (i, 0))] + [full(a) for a in weights],
        out_specs=pl.BlockSpec((tm, d), lambda i: (i, 0)),
        out_shape=jax.ShapeDtypeStruct((n, d), F32),
        compiler_params=pltpu.CompilerParams(dimension_semantics=("arbitrary",), vmem_limit_bytes=limit),
        name="mixer_merge",
    )(o_attn, z, z, z, z, z, z, h, *weights)


def _unpack_bf16_pair(words):
    lo = lax.bitcast_convert_type(words << 16, F32)
    hi = lax.bitcast_convert_type(words & jnp.uint32(0xFFFF0000), F32)
    return jnp.concatenate([lo, hi], axis=1).astype(BF16)


def _swiglu(u, wg_ref, wu_ref, wd_ref):
    gate = jnp.dot(u, wg_ref[...].astype(BF16), preferred_element_type=F32)
    up = jnp.dot(u, wu_ref[...].astype(BF16), preferred_element_type=F32)
    hidden = (gate * jax.nn.sigmoid(gate) * up).astype(BF16)
    return jnp.dot(hidden, wd_ref[...].astype(BF16), preferred_element_type=F32)


def _swiglu_accumulate(u_scr, wg_ref, wu_ref, wd_ref, acc_ref, rows=slice(None)):
    acc_ref[rows, :] += _swiglu(u_scr[rows, :], wg_ref, wu_ref, wd_ref)


def _ffn_vmem_limit(tm, d, tf, x_bytes, scratch_bytes):
    return _vmem_limit(
        x_bytes + 3 * _nbytes((d, tf), F32) + _nbytes((tm, d), F32),
        _nbytes((tm, d), BF16) + scratch_bytes,
        3 * _nbytes((d, tf), BF16) + 4 * _nbytes((tm, tf), F32) + 2 * _nbytes((tm, d), F32))


def _dense_ffn_kernel(h_ref, gin_ref, wg_ref, wu_ref, wd_ref, gout_ref, out_ref, u_scr):
    j = pl.program_id(1)
    last = pl.num_programs(1) - 1
    chunks = _static_row_chunks(u_scr.shape[0])

    @pl.when(j == 0)
    def _():
        for rows in chunks:
            u = _rms_norm_f32(h_ref[rows, :], gin_ref[...]).astype(BF16)
            u_scr[rows, :] = u
            out_ref[rows, :] = _swiglu(u, wg_ref, wu_ref, wd_ref)

    @pl.when(jnp.logical_and(j > 0, j < last))
    def _():
        _swiglu_accumulate(u_scr, wg_ref, wu_ref, wd_ref, out_ref)

    @pl.when(j == last)
    def _():
        for rows in chunks:
            f = out_ref[rows, :] + _swiglu(u_scr[rows, :], wg_ref, wu_ref, wd_ref)
            out_ref[rows, :] = h_ref[rows, :] + _rms_norm_f32(f, gout_ref[...])


def _dense_ffn(h, gain_in, gain_out, w_gate, w_up, w_down, *, tm=1024, tf=256):
    n, d = h.shape
    ff = w_gate.shape[1]
    assert n % tm == 0 and ff % tf == 0 and ff // tf >= 2
    row_blk = pl.BlockSpec((tm, d), lambda i, j: (i, 0))
    gain_blk = pl.BlockSpec((1, d), lambda i, j: (0, 0))
    up_blk = pl.BlockSpec((d, tf), lambda i, j: (0, j))
    return pl.pallas_call(
        _dense_ffn_kernel,
        grid=(n // tm, ff // tf),
        in_specs=[row_blk, gain_blk, up_blk, up_blk, pl.BlockSpec((tf, d), lambda i, j: (j, 0)), gain_blk],
        out_specs=row_blk,
        out_shape=jax.ShapeDtypeStruct((n, d), F32),
        scratch_shapes=[pltpu.VMEM((tm, d), BF16)],
        compiler_params=pltpu.CompilerParams(
            dimension_semantics=("arbitrary", "arbitrary"),
            vmem_limit_bytes=_ffn_vmem_limit(tm, d, tf, _nbytes((tm, d), F32), 0)),
        name="swiglu_dense",
    )(h, gain_in, w_gate, w_up, w_down, gain_out)


def _expert_ffn_kernel(tile_expert_ref, tile_rows_ref, src_ref, x_hbm, wg_ref, wu_ref, wd_ref, out_ref,
                       u_scr, stage, sem):
    del tile_expert_ref
    i, j = pl.program_id(0), pl.program_id(1)
    tm = stage.shape[0]
    granule = EXPERT_ROW_GRANULE
    n_chunks = tile_rows_ref[i] // granule

    def start_gather(tile):
        def gather_chunk(c, carry):
            base = c * granule

            def start(r, inner):
                pltpu.make_async_copy(x_hbm.at[pl.ds(src_ref[tile * tm + base + r], 1), :],
                                      stage.at[pl.ds(base + r, 1), :], sem).start()
                return inner

            return lax.fori_loop(0, granule, start, carry, unroll=DMA_ISSUE_UNROLL)

        lax.fori_loop(0, tile_rows_ref[tile] // granule, gather_chunk, 0)

    @pl.when(jnp.logical_and(i == 0, j == 0))
    def _():
        start_gather(0)

    @pl.when(jnp.logical_and(j == 1, i + 1 < pl.num_programs(0)))
    def _():
        start_gather(i + 1)

    @pl.when(j == 0)
    def _():
        def zero_rows(rows):
            out_ref[rows, :] = jnp.zeros((ROW_CHUNK, out_ref.shape[1]), F32)

        _for_row_chunks(tm, zero_rows)

        def wait_chunk(c, carry):
            pltpu.make_async_copy(x_hbm.at[pl.ds(0, granule), :], stage.at[pl.ds(0, granule), :], sem).wait()
            return carry

        lax.fori_loop(0, n_chunks, wait_chunk, 0)

        def unpack_chunk(c, carry):
            rows = pl.ds(pl.multiple_of(c * granule, granule), granule)
            u_scr[rows, :] = _unpack_bf16_pair(stage[rows, :])
            return carry

        lax.fori_loop(0, n_chunks, unpack_chunk, 0)

    for k in range(1, tm // granule + 1):
        @pl.when(n_chunks == k)
        def _(k=k):
            _swiglu_accumulate(u_scr, wg_ref, wu_ref, wd_ref, out_ref, slice(0, k * granule))


def _expert_ffn(xp, src, tile_expert, tile_rows, w_gate, w_up, w_down, *, tm, tf):
    rows = src.shape[0]
    n_exp, d, ff = w_gate.shape
    assert rows % tm == 0 and tm % ROW_CHUNK == 0 and tm % EXPERT_ROW_GRANULE == 0
    assert ff % tf == 0 and xp.shape[1] == d // 2
    nj = ff // tf
    assert nj >= 2

    def ff_tile(i, j, tr):
        return jnp.where(tr[i] > 0, j, nj - 1)

    up_blk = pl.BlockSpec((None, d, tf), lambda i, j, te, tr, src: (te[i], 0, ff_tile(i, j, tr)))
    down_blk = pl.BlockSpec((None, tf, d), lambda i, j, te, tr, src: (te[i], ff_tile(i, j, tr), 0))
    return pl.pallas_call(
        _expert_ffn_kernel,
        grid_spec=pltpu.PrefetchScalarGridSpec(
            num_scalar_prefetch=3,
            grid=(rows // tm, nj),
            in_specs=[pl.BlockSpec(memory_space=pl.ANY), up_blk, up_blk, down_blk],
            out_specs=pl.BlockSpec((tm, d), lambda i, j, te, tr, src: (i, 0)),
            scratch_shapes=[pltpu.VMEM((tm, d), BF16), pltpu.VMEM((tm, d // 2), jnp.uint32),
                            pltpu.SemaphoreType.DMA]),
        out_shape=jax.ShapeDtypeStruct((rows, d), F32),
        compiler_params=pltpu.CompilerParams(
            dimension_semantics=("arbitrary", "arbitrary"),
            vmem_limit_bytes=_ffn_vmem_limit(tm, d, tf, 0, _nbytes((tm, d // 2), jnp.uint32))),
        name="swiglu_experts",
    )(tile_expert, tile_rows, src, xp, w_gate, w_up, w_down)


def _router_kernel(h_ref, gain_ref, wrt_ref, xp_ref, idx_ref, wts_ref):
    u = _rms_norm_f32(h_ref[...], gain_ref[...])
    logits = lax.dot_general(wrt_ref[...], u, (((1,), (1,)), ((), ())),
                             precision=lax.Precision.HIGHEST, preferred_element_type=F32)
    n_exp = logits.shape[0]
    expert = lax.broadcasted_iota(jnp.int32, logits.shape, 0)
    v1 = jnp.max(logits, axis=0, keepdims=True)
    i1 = jnp.min(jnp.where(logits == v1, expert, n_exp), axis=0, keepdims=True)
    rest = jnp.where(expert == i1, -jnp.inf, logits)
    v2 = jnp.max(rest, axis=0, keepdims=True)
    i2 = jnp.min(jnp.where(rest == v2, expert, n_exp), axis=0, keepdims=True)
    e2 = jnp.exp(v2 - v1)
    idx_ref[...] = jnp.concatenate([i1, i2], axis=0)
    wts_ref[...] = jnp.concatenate([1.0 / (1.0 + e2), e2 / (1.0 + e2)], axis=0)

    bits = lax.bitcast_convert_type(u.astype(BF16).astype(F32), jnp.uint32)
    half = bits.shape[1] // 2
    xp_ref[...] = (bits[:, :half] >> 16) | bits[:, half:]


def _router(h, gain, w_router_t, *, tm=512):
    n, d = h.shape
    n_exp = w_router_t.shape[0]
    assert n % tm == 0
    limit = _vmem_limit(_nbytes((tm, d), F32) + _nbytes((tm, d // 2), jnp.uint32) + _nbytes((n_exp, d), F32),
                        0, 6 * _nbytes((tm, d), F32))
    return pl.pallas_call(
        _router_kernel,
        grid=(n // tm,),
        in_specs=[pl.BlockSpec((tm, d), lambda i: (i, 0)),
                  pl.BlockSpec((1, d), lambda i: (0, 0)),
                  pl.BlockSpec((n_exp, d), lambda i: (0, 0))],
        out_specs=[pl.BlockSpec((tm, d // 2), lambda i: (i, 0)),
                   pl.BlockSpec((TOP_K, tm), lambda i: (0, i)),
                   pl.BlockSpec((TOP_K, tm), lambda i: (0, i))],
        out_shape=[jax.ShapeDtypeStruct((n, d // 2), jnp.uint32),
                   jax.ShapeDtypeStruct((TOP_K, n), jnp.int32),
                   jax.ShapeDtypeStruct((TOP_K, n), F32)],
        compiler_params=pltpu.CompilerParams(dimension_semantics=("arbitrary",), vmem_limit_bytes=limit),
        name="moe_router",
    )(h, gain, w_router_t)


def _combine_kernel(pos_ref, y_hbm, wts_ref, h_ref, gain_ref, out_ref, bufs, sems, *, tm):
    i = pl.program_id(0)

    def start_tile(tile, slot):
        def start(r, carry):
            for k in range(TOP_K):
                pltpu.make_async_copy(y_hbm.at[pl.ds(pos_ref[TOP_K * (tile * tm + r) + k], 1), :],
                                      bufs.at[slot, k, pl.ds(r, 1), :], sems.at[slot, k]).start()
            return carry

        lax.fori_loop(0, tm, start, 0, unroll=DMA_ISSUE_UNROLL)

    @pl.when(i == 0)
    def _():
        start_tile(0, 0)

    @pl.when(i + 1 < pl.num_programs(0))
    def _():
        start_tile(i + 1, (i + 1) % 2)

    slot = i % 2
    for k in range(TOP_K):
        pltpu.make_async_copy(y_hbm.at[pl.ds(0, tm), :], bufs.at[slot, k], sems.at[slot, k]).wait()
    w = wts_ref[...]
    y = w[:, 0:1] * bufs[slot, 0] + w[:, 1:2] * bufs[slot, 1]
    out_ref[...] = h_ref[...] + _rms_norm_f32(y, gain_ref[...])


def _combine(y_rows, pos, wts, h, gain, *, tm=512):
    n, d = h.shape
    assert n % tm == 0
    limit = _vmem_limit(2 * _nbytes((tm, d), F32) + _nbytes((tm, 128), F32),
                        2 * TOP_K * _nbytes((tm, d), F32), 4 * _nbytes((tm, d), F32))
    return pl.pallas_call(
        functools.partial(_combine_kernel, tm=tm),
        grid_spec=pltpu.PrefetchScalarGridSpec(
            num_scalar_prefetch=1,
            grid=(n // tm,),
            in_specs=[pl.BlockSpec(memory_space=pl.ANY),
                      pl.BlockSpec((tm, TOP_K), lambda i, pos: (i, 0)),
                      pl.BlockSpec((tm, d), lambda i, pos: (i, 0)),
                      pl.BlockSpec((1, d), lambda i, pos: (0, 0))],
            out_specs=pl.BlockSpec((tm, d), lambda i, pos: (i, 0)),
            scratch_shapes=[pltpu.VMEM((2, TOP_K, tm, d), F32), pltpu.SemaphoreType.DMA((2, TOP_K))]),
        out_shape=jax.ShapeDtypeStruct((n, d), F32),
        compiler_params=pltpu.CompilerParams(dimension_semantics=("arbitrary",), vmem_limit_bytes=limit),
        name="moe_combine",
    )(pos, y_rows, wts, h, gain)


def _dispatch_plan(top_idx, n_exp, tm):
    n = top_idx.shape[1]
    n_assign = n * TOP_K
    n_tiles = n_assign // tm + n_exp
    expert = top_idx.T.reshape(n_assign)
    onehot = (expert[:, None] == jnp.arange(n_exp, dtype=jnp.int32)[None, :]).astype(jnp.int32)
    rank = jnp.sum((jnp.cumsum(onehot, axis=0) - onehot) * onehot, axis=1)
    counts = jnp.sum(onehot, axis=0)
    tiles_per_expert = (counts + tm - 1) // tm
    tile_ends = jnp.cumsum(tiles_per_expert)
    tile_starts = tile_ends - tiles_per_expert
    pos = tile_starts[expert] * tm + rank
    src = jnp.zeros((n_tiles * tm,), jnp.int32).at[pos].set(jnp.arange(n_assign, dtype=jnp.int32) // TOP_K)
    tile = jnp.arange(n_tiles, dtype=jnp.int32)
    tile_expert = jnp.sum((tile[:, None] >= tile_ends[None, :]).astype(jnp.int32), axis=1)
    used = tile < tile_ends[-1]
    tile_expert = jnp.where(used, tile_expert, tile_expert[tile_ends[-1] - 1])
    granule = EXPERT_ROW_GRANULE
    rows_left = (counts[tile_expert] + granule - 1) // granule * granule - (tile - tile_starts[tile_expert]) * tm
    tile_rows = jnp.where(used, jnp.clip(rows_left, 0, tm), 0)
    return pos.astype(jnp.int32), src, tile_expert.astype(jnp.int32), tile_rows.astype(jnp.int32)


def _moe(h, gain_in, gain_out, w_router, w_gate, w_up, w_down, *, tm=1280, tf=512):
    n_exp = w_router.shape[1]
    xp, top_idx, top_w = _router(h, gain_in, w_router.T)
    pos, src, tile_expert, tile_rows = _dispatch_plan(top_idx, n_exp, tm)
    y_rows = _expert_ffn(xp, src, tile_expert, tile_rows, w_gate, w_up, w_down, tm=tm, tf=tf)
    return _combine(y_rows, pos, top_w.T, h, gain_out)


def kernel(x, w_in, rel_bias, w_pool, pool_scale, w_attn_out, w_pool_out, w_out, norm_gains,
           dense_w_gate, dense_w_up, dense_w_down, moe_w_router, moe_w_gate, moe_w_up, moe_w_down):
    batch, seq, d = x.shape
    depth = w_in.shape[0]
    attn_width = N_ATTN_GROUPS * GROUP_WIDTH
    h = x.reshape(batch * seq, d)
    biases = [_band_bias(rel_bias[:, g * HEADS_PER_GROUP:(g + 1) * HEADS_PER_GROUP], window, dilation)
              for g, (window, dilation) in enumerate(DILATION_PATTERNS)]
    for layer in range(depth):
        gains = norm_gains[layer].reshape(4, 1, d)
        z = _in_proj(h, gains[0], w_in, layer)
        o_attn = _attention(z, biases, [dilation for _, dilation in DILATION_PATTERNS],
                            batch=batch, seq=seq, attn_width=attn_width)
        h = _merge(h, z, o_attn, w_pool[layer].astype(BF16), pool_scale[layer].reshape(1, -1),
                   w_attn_out[layer].astype(BF16), w_pool_out[layer].astype(BF16), w_out[layer].astype(BF16),
                   gains[1], seq=seq, attn_width=attn_width)
        j = layer // 2
        if layer % 2 == 0:
            h = _dense_ffn(h, gains[2], gains[3], dense_w_gate[j], dense_w_up[j], dense_w_down[j])
        else:
            h = _moe(h, gains[2], gains[3], moe_w_router[j], moe_w_gate[j], moe_w_up[j], moe_w_down[j])
    return h.reshape(batch, seq, d)
```

```python
import functools
import math

import jax
import jax.numpy as jnp
from jax import lax
from jax.experimental import pallas as pl
from jax.experimental.pallas import tpu as pltpu

F32 = jnp.float32
BF16 = jnp.bfloat16

RMS_EPS = 1e-6
HEAD_DIM = 128
Q_BLOCK = 128
DILATION_PATTERNS = ((128, 1), (512, 4), (2048, 16))
N_ATTN_GROUPS = len(DILATION_PATTERNS)
HEADS_PER_GROUP = 4
GROUP_WIDTH = HEADS_PER_GROUP * HEAD_DIM
POOL_SIZES = (2, 4, 8, 16)
POOL_GROUP_WIDTH = 128
POOL_HALO = 16
N_BUCKETS = 32
MAX_DISTANCE = 2048
TOP_K = 2

V7X_VMEM_BYTES = 64 * 1024 * 1024


def _vmem_limit(pipelined_bytes, resident_bytes, temp_bytes):
    return min(2 * pipelined_bytes + resident_bytes + temp_bytes, V7X_VMEM_BYTES)


def _nbytes(shape, dtype):
    return math.prod(shape) * jnp.dtype(dtype).itemsize


def _rms_norm_f32(x, gain):
    ms = jnp.mean(x * x, axis=-1, keepdims=True)
    return x * lax.rsqrt(ms + RMS_EPS) * gain


ROW_CHUNK = 256
EXPERT_ROW_GRANULE = 128
DMA_ISSUE_UNROLL = 16


def _for_row_chunks(n_rows, body):
    assert n_rows % ROW_CHUNK == 0

    def step(c, carry):
        body(pl.ds(pl.multiple_of(c * ROW_CHUNK, ROW_CHUNK), ROW_CHUNK))
        return carry

    lax.fori_loop(0, n_rows // ROW_CHUNK, step, 0)


def _static_row_chunks(n_rows):
    assert n_rows % ROW_CHUNK == 0
    return [slice(c * ROW_CHUNK, (c + 1) * ROW_CHUNK) for c in range(n_rows // ROW_CHUNK)]


def _inproj_kernel(h_ref, g_ref, w_ref, z_ref, u_scr):
    j = pl.program_id(1)

    @pl.when(j == 0)
    def _():
        for rows in _static_row_chunks(u_scr.shape[0]):
            u = _rms_norm_f32(h_ref[rows, :], g_ref[...]).astype(BF16)
            u_scr[rows, :] = u
            z_ref[rows, :] = jnp.dot(u, w_ref[...].astype(BF16), preferred_element_type=F32)

    @pl.when(j > 0)
    def _():
        z_ref[...] = jnp.dot(u_scr[...], w_ref[...].astype(BF16), preferred_element_type=F32)


def _in_proj(h, gain, w_all, layer, *, tm=2048, tn=512):
    n, d = h.shape
    width = w_all.shape[2]
    assert n % tm == 0 and width % tn == 0
    limit = _vmem_limit(
        _nbytes((d, tn), F32) + _nbytes((tm, tn), F32),
        _nbytes((tm, d), F32) + _nbytes((tm, d), BF16),
        _nbytes((tm, d), F32) + _nbytes((d, tn), BF16) + _nbytes((tm, tn), F32))
    return pl.pallas_call(
        _inproj_kernel,
        grid=(n // tm, width // tn),
        in_specs=[pl.BlockSpec((tm, d), lambda i, j: (i, 0), pipeline_mode=pl.Buffered(1)),
                  pl.BlockSpec((1, d), lambda i, j: (0, 0)),
                  pl.BlockSpec((None, d, tn), lambda i, j: (layer, 0, j))],
        out_specs=pl.BlockSpec((tm, tn), lambda i, j: (i, j)),
        out_shape=jax.ShapeDtypeStruct((n, width), F32),
        scratch_shapes=[pltpu.VMEM((tm, d), BF16)],
        compiler_params=pltpu.CompilerParams(dimension_semantics=("arbitrary", "arbitrary"),
                                             vmem_limit_bytes=limit),
        name="in_proj",
    )(h, gain, w_all)


def _group_attention_into(q_ref, k_ref, v_ref, kh_ref, vh_ref, bias_ref, o_scr, l_scr, *, dilation, scale):
    first_chunk = pl.program_id(1) == 0
    n_sub = q_ref.shape[1] // (dilation * Q_BLOCK)

    def stream_rows(start):
        if dilation == 1:
            return pl.ds(start, Q_BLOCK)
        return pl.ds(start, Q_BLOCK, stride=dilation)

    for r in range(dilation):
        k_prev = kh_ref[0, stream_rows(r), :].astype(BF16)
        v_prev = vh_ref[0, stream_rows(r), :].astype(BF16)
        for n in range(n_sub):
            rows = stream_rows(r + n * Q_BLOCK * dilation)
            q = q_ref[0, rows, :].astype(BF16)
            k_cur = k_ref[0, rows, :].astype(BF16)
            v_cur = v_ref[0, rows, :].astype(BF16)
            kc = jnp.concatenate([k_prev, k_cur], axis=0)
            vc = jnp.concatenate([v_prev, v_cur], axis=0)
            s = lax.dot_general(q, kc, (((1,), (1,)), ((), ())), preferred_element_type=F32)
            s = s * scale + bias_ref[0]
            if n == 0:
                key_col = lax.broadcasted_iota(jnp.int32, s.shape, 1)
                s = jnp.where(jnp.logical_and(first_chunk, key_col < Q_BLOCK), -jnp.inf, s)
            m = jnp.max(s, axis=-1, keepdims=True)
            p = jnp.exp(s - m)
            den = jnp.sum(p, axis=-1, keepdims=True)
            o_scr[rows, :] = jnp.dot(p.astype(BF16), vc, preferred_element_type=F32) / den
            l_scr[rows, :] = jnp.broadcast_to(m + jnp.log(den), (Q_BLOCK, HEAD_DIM))
            k_prev, v_prev = k_cur, v_cur


def _attn_kernel(*refs, dilations, scale):
    n_groups = len(dilations)
    group_refs = [refs[6 * g:6 * (g + 1)] for g in range(n_groups)]
    out_ref = refs[6 * n_groups]
    scratch = refs[6 * n_groups + 1:]
    o_scrs, l_scrs = scratch[:n_groups], scratch[n_groups:]
    for g, dilation in enumerate(dilations):
        _group_attention_into(*group_refs[g], o_scrs[g], l_scrs[g], dilation=dilation, scale=scale)

    def mix_rows(rows):
        lses = [l[rows, :] for l in l_scrs]
        m = functools.reduce(jnp.maximum, lses)
        es = [jnp.exp(l - m) for l in lses]
        num = sum(e * o[rows, :] for e, o in zip(es, o_scrs))
        out_ref[0, rows, :] = num / sum(es)

    _for_row_chunks(out_ref.shape[1], mix_rows)


def _t5_causal_bucket(dist):
    max_exact = N_BUCKETS // 2
    df = jnp.maximum(dist, 1).astype(F32)
    large = max_exact + (jnp.log(df / max_exact) / math.log(MAX_DISTANCE / max_exact)
                         * (N_BUCKETS - max_exact)).astype(jnp.int32)
    large = jnp.minimum(large, N_BUCKETS - 1)
    return jnp.where(dist < max_exact, dist, large)


def _band_bias(bias_table, window, dilation):
    span = window // dilation
    qi = jnp.arange(Q_BLOCK)[:, None]
    kj = jnp.arange(2 * Q_BLOCK)[None, :]
    step = qi + Q_BLOCK - kj
    valid = (step >= 0) & (step <= span)
    bucket = _t5_causal_bucket(jnp.clip(step, 0, span) * dilation)
    onehot = (bucket[:, :, None] == jnp.arange(N_BUCKETS)[None, None, :]).astype(F32)
    bias = jnp.einsum("qkb,bh->hqk", onehot, bias_table, precision=lax.Precision.HIGHEST)
    return jnp.where(valid[None], bias, -jnp.inf)


def _attention(z, biases, dilations, *, batch, seq, attn_width, chunk=2048):
    assert seq % chunk == 0 and attn_width % HEAD_DIM == 0
    zv = z.reshape(batch, seq, z.shape[1])
    blk = _nbytes((chunk, HEAD_DIM), F32)
    in_specs, operands, pipelined = [], [], blk
    for group, (dilation, bias) in enumerate(zip(dilations, biases)):
        hist = Q_BLOCK * dilation
        assert chunk % hist == 0
        q_col = group * HEADS_PER_GROUP
        k_col = q_col + attn_width // HEAD_DIM
        v_col = k_col + attn_width // HEAD_DIM

        def cur(col):
            return pl.BlockSpec((1, chunk, HEAD_DIM), lambda b, c, hh, col=col: (b, c, col + hh))

        def prev(col, hist=hist):
            return pl.BlockSpec(
                (1, hist, HEAD_DIM),
                lambda b, c, hh, col=col, hist=hist: (b, jnp.maximum(c * (chunk // hist) - 1, 0), col + hh))

        in_specs += [cur(q_col), cur(k_col), cur(v_col), prev(k_col), prev(v_col),
                     pl.BlockSpec((1,) + bias.shape[1:], lambda b, c, hh: (hh, 0, 0))]
        operands += [zv, zv, zv, zv, zv, bias]
        pipelined += 3 * blk + 2 * _nbytes((hist, HEAD_DIM), F32) + _nbytes(bias.shape[1:], F32)
    n_groups = len(dilations)
    o = pl.pallas_call(
        functools.partial(_attn_kernel, dilations=tuple(dilations), scale=HEAD_DIM ** -0.5),
        grid=(batch, seq // chunk, HEADS_PER_GROUP),
        in_specs=in_specs,
        out_specs=pl.BlockSpec((1, chunk, HEAD_DIM), lambda b, c, hh: (b, c, hh)),
        out_shape=jax.ShapeDtypeStruct((batch, seq, GROUP_WIDTH), F32),
        scratch_shapes=[pltpu.VMEM((chunk, HEAD_DIM), F32)] * (2 * n_groups),
        compiler_params=pltpu.CompilerParams(
            dimension_semantics=("arbitrary",) * 3,
            vmem_limit_bytes=_vmem_limit(pipelined, 2 * n_groups * blk, 4 * blk)),
        name="dilated_attention",
    )(*operands)
    return o.reshape(batch * seq, GROUP_WIDTH)


def _merge_kernel(oattn_ref, p_ref, pp_ref, ga0_ref, ga1_ref, gb0_ref, gb1_ref, h_ref,
                  wpool_ref, pscale_ref, wao_ref, wpo_ref, wout_ref, gain_ref, out_ref,
                  *, tm, tiles_per_seq):
    tile_in_seq = pl.program_id(0) % tiles_per_seq

    halo = jnp.where(tile_in_seq == 0, 0.0, pp_ref[...])
    xe = jnp.concatenate([halo, p_ref[...]], axis=0)
    t = tile_in_seq * tm + lax.broadcasted_iota(jnp.int32, (tm, 1), 0)
    pooled = []
    for g, size in enumerate(POOL_SIZES):
        a = xe[:, g * POOL_GROUP_WIDTH:(g + 1) * POOL_GROUP_WIDTH]
        s, shift = a, 1
        while shift < size:
            s = s + pltpu.roll(s, shift, axis=0)
            shift *= 2
        count = jnp.minimum(t + 1, size).astype(F32)
        y = (s[POOL_HALO:] / count - a[POOL_HALO:]).astype(BF16)
        pooled.append(jnp.dot(y, wpool_ref[g], preferred_element_type=F32))
    o_pool = jnp.concatenate(pooled, axis=1) * pscale_ref[...]

    attn_proj = jnp.dot(oattn_ref[...].astype(BF16), wao_ref[...], preferred_element_type=F32)
    pool_proj = jnp.dot(o_pool.astype(BF16), wpo_ref[...], preferred_element_type=F32)
    gate_a = jax.nn.sigmoid(jnp.concatenate([ga0_ref[...], ga1_ref[...]], axis=1))
    gate_b = jax.nn.sigmoid(jnp.concatenate([gb0_ref[...], gb1_ref[...]], axis=1))
    merged = gate_a * attn_proj + gate_b * pool_proj
    mix = jnp.dot(merged.astype(BF16), wout_ref[...], preferred_element_type=F32)
    out_ref[...] = h_ref[...] + _rms_norm_f32(mix, gain_ref[...])


def _merge(h, z, o_attn, w_pool, pool_scale, w_attn_out, w_pool_out, w_out, gain, *, seq, attn_width, tm=512):
    n, d = h.shape
    pool_width = w_pool.shape[0] * POOL_GROUP_WIDTH
    assert pool_width == GROUP_WIDTH
    assert n % tm == 0 and seq % tm == 0 and tm % POOL_HALO == 0
    p_col = 3 * attn_width // GROUP_WIDTH
    gate_w = d // 2
    gate_col = (3 * attn_width + pool_width) // gate_w
    assert (3 * attn_width + pool_width) % gate_w == 0

    row_blk = lambda c: pl.BlockSpec((tm, GROUP_WIDTH), lambda i: (i, c))
    gate_blk = lambda c: pl.BlockSpec((tm, gate_w), lambda i: (i, gate_col + c))
    halo_blk = pl.BlockSpec((POOL_HALO, GROUP_WIDTH),
                            lambda i: (jnp.maximum(i * (tm // POOL_HALO) - 1, 0), p_col))
    full = lambda a: pl.BlockSpec(a.shape, lambda i: (0,) * a.ndim)
    weights = (w_pool, pool_scale, w_attn_out, w_pool_out, w_out, gain)

    act = 2 * _nbytes((tm, GROUP_WIDTH), F32) + 4 * _nbytes((tm, gate_w), F32) + 2 * _nbytes((tm, d), F32)
    wbytes = sum(_nbytes(a.shape, a.dtype) for a in weights)
    limit = _vmem_limit(act + wbytes, 0, 8 * _nbytes((tm, d), F32))
    return pl.pallas_call(
        functools.partial(_merge_kernel, tm=tm, tiles_per_seq=seq // tm),
        grid=(n // tm,),
        in_specs=[row_blk(0), row_blk(p_col), halo_blk] + [gate_blk(c) for c in range(4)]
                 + [pl.BlockSpec((tm, d), lambda i: (i, 0))] + [full(a) for a in weights],
        out_specs=pl.BlockSpec((tm, d), lambda i: (i, 0)),
        out_shape=jax.ShapeDtypeStruct((n, d), F32),
        compiler_params=pltpu.CompilerParams(dimension_semantics=("arbitrary",), vmem_limit_bytes=limit),
        name="mixer_merge",
    )(o_attn, z, z, z, z, z, z, h, *weights)


def _unpack_bf16_pair(words):
    lo = lax.bitcast_convert_type(words << 16, F32)
    hi = lax.bitcast_convert_type(words & jnp.uint32(0xFFFF0000), F32)
    return jnp.concatenate([lo, hi], axis=1).astype(BF16)


def _swiglu(u, wg_ref, wu_ref, wd_ref):
    gate = jnp.dot(u, wg_ref[...].astype(BF16), preferred_element_type=F32)
    up = jnp.dot(u, wu_ref[...].astype(BF16), preferred_element_type=F32)
    hidden = (gate * jax.nn.sigmoid(gate) * up).astype(BF16)
    return jnp.dot(hidden, wd_ref[...].astype(BF16), preferred_element_type=F32)


def _swiglu_accumulate(u_scr, wg_ref, wu_ref, wd_ref, acc_ref, rows=slice(None)):
    acc_ref[rows, :] += _swiglu(u_scr[rows, :], wg_ref, wu_ref, wd_ref)


def _ffn_vmem_limit(tm, d, tf, x_bytes, scratch_bytes):
    return _vmem_limit(
        x_bytes + 3 * _nbytes((d, tf), F32) + _nbytes((tm, d), F32),
        _nbytes((tm, d), BF16) + scratch_bytes,
        3 * _nbytes((d, tf), BF16) + 4 * _nbytes((tm, tf), F32) + 2 * _nbytes((tm, d), F32))


def _dense_ffn_kernel(h_ref, gin_ref, wg_ref, wu_ref, wd_ref, gout_ref, out_ref, u_scr):
    j = pl.program_id(1)
    last = pl.num_programs(1) - 1
    chunks = _static_row_chunks(u_scr.shape[0])

    @pl.when(j == 0)
    def _():
        for rows in chunks:
            u = _rms_norm_f32(h_ref[rows, :], gin_ref[...]).astype(BF16)
            u_scr[rows, :] = u
            out_ref[rows, :] = _swiglu(u, wg_ref, wu_ref, wd_ref)

    @pl.when(jnp.logical_and(j > 0, j < last))
    def _():
        _swiglu_accumulate(u_scr, wg_ref, wu_ref, wd_ref, out_ref)

    @pl.when(j == last)
    def _():
        for rows in chunks:
            f = out_ref[rows, :] + _swiglu(u_scr[rows, :], wg_ref, wu_ref, wd_ref)
            out_ref[rows, :] = h_ref[rows, :] + _rms_norm_f32(f, gout_ref[...])


def _dense_ffn(h, gain_in, gain_out, w_gate, w_up, w_down, *, tm=1024, tf=256):
    n, d = h.shape
    ff = w_gate.shape[1]
    assert n % tm == 0 and ff % tf == 0 and ff // tf >= 2
    row_blk = pl.BlockSpec((tm, d), lambda i, j: (i, 0))
    gain_blk = pl.BlockSpec((1, d), lambda i, j: (0, 0))
    up_blk = pl.BlockSpec((d, tf), lambda i, j: (0, j))
    return pl.pallas_call(
        _dense_ffn_kernel,
        grid=(n // tm, ff // tf),
        in_specs=[row_blk, gain_blk, up_blk, up_blk, pl.BlockSpec((tf, d), lambda i, j: (j, 0)), gain_blk],
        out_specs=row_blk,
        out_shape=jax.ShapeDtypeStruct((n, d), F32),
        scratch_shapes=[pltpu.VMEM((tm, d), BF16)],
        compiler_params=pltpu.CompilerParams(
            dimension_semantics=("arbitrary", "arbitrary"),
            vmem_limit_bytes=_ffn_vmem_limit(tm, d, tf, _nbytes((tm, d), F32), 0)),
        name="swiglu_dense",
    )(h, gain_in, w_gate, w_up, w_down, gain_out)


def _expert_ffn_kernel(tile_expert_ref, tile_rows_ref, src_ref, x_hbm, wg_ref, wu_ref, wd_ref, out_ref,
                       u_scr, stage, sem):
    del tile_expert_ref
    i, j = pl.program_id(0), pl.program_id(1)
    tm = stage.shape[0]
    granule = EXPERT_ROW_GRANULE
    n_chunks = tile_rows_ref[i] // granule

    @pl.when(j == 0)
    def _():
        def zero_rows(rows):
            out_ref[rows, :] = jnp.zeros((ROW_CHUNK, out_ref.shape[1]), F32)

        _for_row_chunks(tm, zero_rows)

        def gather_chunk(c, carry):
            base = c * granule

            def start(r, inner):
                pltpu.make_async_copy(x_hbm.at[pl.ds(src_ref[i * tm + base + r], 1), :],
                                      stage.at[pl.ds(base + r, 1), :], sem).start(priority=1)
                return inner

            return lax.fori_loop(0, granule, start, carry, unroll=DMA_ISSUE_UNROLL)

        lax.fori_loop(0, n_chunks, gather_chunk, 0)

        def wait_chunk(c, carry):
            pltpu.make_async_copy(x_hbm.at[pl.ds(0, granule), :], stage.at[pl.ds(0, granule), :], sem).wait()
            return carry

        lax.fori_loop(0, n_chunks, wait_chunk, 0)

        def unpack_chunk(c, carry):
            rows = pl.ds(pl.multiple_of(c * granule, granule), granule)
            u_scr[rows, :] = _unpack_bf16_pair(stage[rows, :])
            return carry

        lax.fori_loop(0, n_chunks, unpack_chunk, 0)

    for k in range(1, tm // granule + 1):
        @pl.when(n_chunks == k)
        def _(k=k):
            _swiglu_accumulate(u_scr, wg_ref, wu_ref, wd_ref, out_ref, slice(0, k * granule))


def _expert_ffn(xp, src, tile_expert, tile_rows, w_gate, w_up, w_down, *, tm, tf):
    rows = src.shape[0]
    n_exp, d, ff = w_gate.shape
    assert rows % tm == 0 and tm % ROW_CHUNK == 0 and tm % EXPERT_ROW_GRANULE == 0
    assert ff % tf == 0 and xp.shape[1] == d // 2
    nj = ff // tf

    def ff_tile(i, j, tr):
        return jnp.where(tr[i] > 0, j, nj - 1)

    up_blk = pl.BlockSpec((None, d, tf), lambda i, j, te, tr, src: (te[i], 0, ff_tile(i, j, tr)))
    down_blk = pl.BlockSpec((None, tf, d), lambda i, j, te, tr, src: (te[i], ff_tile(i, j, tr), 0))
    return pl.pallas_call(
        _expert_ffn_kernel,
        grid_spec=pltpu.PrefetchScalarGridSpec(
            num_scalar_prefetch=3,
            grid=(rows // tm, nj),
            in_specs=[pl.BlockSpec(memory_space=pl.ANY), up_blk, up_blk, down_blk],
            out_specs=pl.BlockSpec((tm, d), lambda i, j, te, tr, src: (i, 0)),
            scratch_shapes=[pltpu.VMEM((tm, d), BF16), pltpu.VMEM((tm, d // 2), jnp.uint32),
                            pltpu.SemaphoreType.DMA]),
        out_shape=jax.ShapeDtypeStruct((rows, d), F32),
        compiler_params=pltpu.CompilerParams(
            dimension_semantics=("arbitrary", "arbitrary"),
            vmem_limit_bytes=_ffn_vmem_limit(tm, d, tf, 0, _nbytes((tm, d // 2), jnp.uint32))),
        name="swiglu_experts",
    )(tile_expert, tile_rows, src, xp, w_gate, w_up, w_down)


def _router_kernel(h_ref, gain_ref, wrt_ref, xp_ref, idx_ref, wts_ref):
    u = _rms_norm_f32(h_ref[...], gain_ref[...])
    logits = lax.dot_general(wrt_ref[...], u, (((1,), (1,)), ((), ())),
                             precision=lax.Precision.HIGHEST, preferred_element_type=F32)
    n_exp = logits.shape[0]
    expert = lax.broadcasted_iota(jnp.int32, logits.shape, 0)
    v1 = jnp.max(logits, axis=0, keepdims=True)
    i1 = jnp.min(jnp.where(logits == v1, expert, n_exp), axis=0, keepdims=True)
    rest = jnp.where(expert == i1, -jnp.inf, logits)
    v2 = jnp.max(rest, axis=0, keepdims=True)
    i2 = jnp.min(jnp.where(rest == v2, expert, n_exp), axis=0, keepdims=True)
    e2 = jnp.exp(v2 - v1)
    idx_ref[...] = jnp.concatenate([i1, i2], axis=0)
    wts_ref[...] = jnp.concatenate([1.0 / (1.0 + e2), e2 / (1.0 + e2)], axis=0)

    bits = lax.bitcast_convert_type(u.astype(BF16).astype(F32), jnp.uint32)
    half = bits.shape[1] // 2
    xp_ref[...] = (bits[:, :half] >> 16) | bits[:, half:]


def _router(h, gain, w_router_t, *, tm=512):
    n, d = h.shape
    n_exp = w_router_t.shape[0]
    assert n % tm == 0
    limit = _vmem_limit(_nbytes((tm, d), F32) + _nbytes((tm, d // 2), jnp.uint32) + _nbytes((n_exp, d), F32),
                        0, 6 * _nbytes((tm, d), F32))
    return pl.pallas_call(
        _router_kernel,
        grid=(n // tm,),
        in_specs=[pl.BlockSpec((tm, d), lambda i: (i, 0)),
                  pl.BlockSpec((1, d), lambda i: (0, 0)),
                  pl.BlockSpec((n_exp, d), lambda i: (0, 0))],
        out_specs=[pl.BlockSpec((tm, d // 2), lambda i: (i, 0)),
                   pl.BlockSpec((TOP_K, tm), lambda i: (0, i)),
                   pl.BlockSpec((TOP_K, tm), lambda i: (0, i))],
        out_shape=[jax.ShapeDtypeStruct((n, d // 2), jnp.uint32),
                   jax.ShapeDtypeStruct((TOP_K, n), jnp.int32),
                   jax.ShapeDtypeStruct((TOP_K, n), F32)],
        compiler_params=pltpu.CompilerParams(dimension_semantics=("arbitrary",), vmem_limit_bytes=limit),
        name="moe_router",
    )(h, gain, w_router_t)


def _combine_kernel(pos_ref, y_hbm, wts_ref, h_ref, gain_ref, out_ref, bufs, sems, *, tm):
    i = pl.program_id(0)

    def start_tile(tile, slot):
        def start(r, carry):
            for k in range(TOP_K):
                pltpu.make_async_copy(y_hbm.at[pl.ds(pos_ref[TOP_K * (tile * tm + r) + k], 1), :],
                                      bufs.at[slot, k, pl.ds(r, 1), :], sems.at[slot, k]).start(priority=k)
            return carry

        lax.fori_loop(0, tm, start, 0, unroll=DMA_ISSUE_UNROLL)

    @pl.when(i == 0)
    def _():
        start_tile(0, 0)

    @pl.when(i + 1 < pl.num_programs(0))
    def _():
        start_tile(i + 1, (i + 1) % 2)

    slot = i % 2
    for k in range(TOP_K):
        pltpu.make_async_copy(y_hbm.at[pl.ds(0, tm), :], bufs.at[slot, k], sems.at[slot, k]).wait()
    w = wts_ref[...]
    y = w[:, 0:1] * bufs[slot, 0] + w[:, 1:2] * bufs[slot, 1]
    out_ref[...] = h_ref[...] + _rms_norm_f32(y, gain_ref[...])


def _combine(y_rows, pos, wts, h, gain, *, tm=512):
    n, d = h.shape
    assert n % tm == 0
    limit = _vmem_limit(2 * _nbytes((tm, d), F32) + _nbytes((tm, 128), F32),
                        2 * TOP_K * _nbytes((tm, d), F32), 4 * _nbytes((tm, d), F32))
    return pl.pallas_call(
        functools.partial(_combine_kernel, tm=tm),
        grid_spec=pltpu.PrefetchScalarGridSpec(
            num_scalar_prefetch=1,
            grid=(n // tm,),
            in_specs=[pl.BlockSpec(memory_space=pl.ANY),
                      pl.BlockSpec((tm, TOP_K), lambda i, pos: (i, 0)),
                      pl.BlockSpec((tm, d), lambda i, pos: (i, 0)),
                      pl.BlockSpec((1, d), lambda i, pos: (0, 0))],
            out_specs=pl.BlockSpec((tm, d), lambda i, pos: (i, 0)),
            scratch_shapes=[pltpu.VMEM((2, TOP_K, tm, d), F32), pltpu.SemaphoreType.DMA((2, TOP_K))]),
        out_shape=jax.ShapeDtypeStruct((n, d), F32),
        compiler_params=pltpu.CompilerParams(dimension_semantics=("arbitrary",), vmem_limit_bytes=limit),
        name="moe_combine",
    )(pos, y_rows, wts, h, gain)


def _dispatch_plan(top_idx, n_exp, tm):
    n = top_idx.shape[1]
    n_assign = n * TOP_K
    n_tiles = n_assign // tm + n_exp
    expert = top_idx.T.reshape(n_assign)
    onehot = (expert[:, None] == jnp.arange(n_exp, dtype=jnp.int32)[None, :]).astype(jnp.int32)
    rank = jnp.sum((jnp.cumsum(onehot, axis=0) - onehot) * onehot, axis=1)
    counts = jnp.sum(onehot, axis=0)
    tiles_per_expert = (counts + tm - 1) // tm
    tile_ends = jnp.cumsum(tiles_per_expert)
    tile_starts = tile_ends - tiles_per_expert
    pos = tile_starts[expert] * tm + rank
    src = jnp.zeros((n_tiles * tm,), jnp.int32).at[pos].set(jnp.arange(n_assign, dtype=jnp.int32) // TOP_K)
    tile = jnp.arange(n_tiles, dtype=jnp.int32)
    tile_expert = jnp.sum((tile[:, None] >= tile_ends[None, :]).astype(jnp.int32), axis=1)
    used = tile < tile_ends[-1]
    tile_expert = jnp.where(used, tile_expert, tile_expert[tile_ends[-1] - 1])
    granule = EXPERT_ROW_GRANULE
    rows_left = (counts[tile_expert] + granule - 1) // granule * granule - (tile - tile_starts[tile_expert]) * tm
    tile_rows = jnp.where(used, jnp.clip(rows_left, 0, tm), 0)
    return pos.astype(jnp.int32), src, tile_expert.astype(jnp.int32), tile_rows.astype(jnp.int32)


def _moe(h, gain_in, gain_out, w_router, w_gate, w_up, w_down, *, tm=1280, tf=512):
    n_exp = w_router.shape[1]
    xp, top_idx, top_w = _router(h, gain_in, w_router.T)
    pos, src, tile_expert, tile_rows = _dispatch_plan(top_idx, n_exp, tm)
    y_rows = _expert_ffn(xp, src, tile_expert, tile_rows, w_gate, w_up, w_down, tm=tm, tf=tf)
    return _combine(y_rows, pos, top_w.T, h, gain_out)


def kernel(x, w_in, rel_bias, w_pool, pool_scale, w_attn_out, w_pool_out, w_out, norm_gains,
           dense_w_gate, dense_w_up, dense_w_down, moe_w_router, moe_w_gate, moe_w_up, moe_w_down):
    batch, seq, d = x.shape
    depth = w_in.shape[0]
    attn_width = N_ATTN_GROUPS * GROUP_WIDTH
    h = x.reshape(batch * seq, d)
    biases = [_band_bias(rel_bias[:, g * HEADS_PER_GROUP:(g + 1) * HEADS_PER_GROUP], window, dilation)
              for g, (window, dilation) in enumerate(DILATION_PATTERNS)]
    for layer in range(depth):
        gains = norm_gains[layer].reshape(4, 1, d)
        z = _in_proj(h, gains[0], w_in, layer)
        o_attn = _attention(z, biases, [dilation for _, dilation in DILATION_PATTERNS],
                            batch=batch, seq=seq, attn_width=attn_width)
        h = _merge(h, z, o_attn, w_pool[layer].astype(BF16), pool_scale[layer].reshape(1, -1),
                   w_attn_out[layer].astype(BF16), w_pool_out[layer].astype(BF16), w_out[layer].astype(BF16),
                   gains[1], seq=seq, attn_width=attn_width)
        j = layer // 2
        if layer % 2 == 0:
            h = _dense_ffn(h, gains[2], gains[3], dense_w_gate[j], dense_w_up[j], dense_w_down[j])
        else:
            h = _moe(h, gains[2], gains[3], moe_w_router[j], moe_w_gate[j], moe_w_up[j], moe_w_down[j])
    return h.reshape(batch, seq, d)
```

```python
import functools
import math

import jax
import jax.numpy as jnp
from jax import lax
from jax.experimental import pallas as pl
from jax.experimental.pallas import tpu as pltpu

F32 = jnp.float32
BF16 = jnp.bfloat16

RMS_EPS = 1e-6
HEAD_DIM = 128
Q_BLOCK = 128
DILATION_PATTERNS = ((128, 1), (512, 4), (2048, 16))
N_ATTN_GROUPS = len(DILATION_PATTERNS)
HEADS_PER_GROUP = 4
GROUP_WIDTH = HEADS_PER_GROUP * HEAD_DIM
POOL_SIZES = (2, 4, 8, 16)
POOL_GROUP_WIDTH = 128
POOL_HALO = 16
N_BUCKETS = 32
MAX_DISTANCE = 2048
TOP_K = 2

V7X_VMEM_BYTES = 64 * 1024 * 1024


def _vmem_limit(pipelined_bytes, resident_bytes, temp_bytes):
    return min(2 * pipelined_bytes + resident_bytes + temp_bytes, V7X_VMEM_BYTES)


def _nbytes(shape, dtype):
    return math.prod(shape) * jnp.dtype(dtype).itemsize


def _rms_norm_f32(x, gain):
    ms = jnp.mean(x * x, axis=-1, keepdims=True)
    return x * lax.rsqrt(ms + RMS_EPS) * gain


ROW_CHUNK = 256
EXPERT_ROW_GRANULE = 128
DMA_ISSUE_UNROLL = 16


def _for_row_chunks(n_rows, body):
    assert n_rows % ROW_CHUNK == 0

    def step(c, carry):
        body(pl.ds(pl.multiple_of(c * ROW_CHUNK, ROW_CHUNK), ROW_CHUNK))
        return carry

    lax.fori_loop(0, n_rows // ROW_CHUNK, step, 0)


def _static_row_chunks(n_rows):
    assert n_rows % ROW_CHUNK == 0
    return [slice(c * ROW_CHUNK, (c + 1) * ROW_CHUNK) for c in range(n_rows // ROW_CHUNK)]


def _inproj_kernel(h_ref, g_ref, w_ref, z_ref, u_scr):
    j = pl.program_id(1)

    @pl.when(j == 0)
    def _():
        for rows in _static_row_chunks(u_scr.shape[0]):
            u = _rms_norm_f32(h_ref[rows, :], g_ref[...]).astype(BF16)
            u_scr[rows, :] = u
            z_ref[rows, :] = jnp.dot(u, w_ref[...].astype(BF16), preferred_element_type=F32)

    @pl.when(j > 0)
    def _():
        z_ref[...] = jnp.dot(u_scr[...], w_ref[...].astype(BF16), preferred_element_type=F32)


def _in_proj(h, gain, w_all, layer, *, tm=2048, tn=512):
    n, d = h.shape
    width = w_all.shape[2]
    assert n % tm == 0 and width % tn == 0
    limit = _vmem_limit(
        _nbytes((d, tn), F32) + _nbytes((tm, tn), F32),
        _nbytes((tm, d), F32) + _nbytes((tm, d), BF16),
        _nbytes((tm, d), F32) + _nbytes((d, tn), BF16) + _nbytes((tm, tn), F32))
    return pl.pallas_call(
        _inproj_kernel,
        grid=(n // tm, width // tn),
        in_specs=[pl.BlockSpec((tm, d), lambda i, j: (i, 0), pipeline_mode=pl.Buffered(1)),
                  pl.BlockSpec((1, d), lambda i, j: (0, 0)),
                  pl.BlockSpec((None, d, tn), lambda i, j: (layer, 0, j))],
        out_specs=pl.BlockSpec((tm, tn), lambda i, j: (i, j)),
        out_shape=jax.ShapeDtypeStruct((n, width), F32),
        scratch_shapes=[pltpu.VMEM((tm, d), BF16)],
        compiler_params=pltpu.CompilerParams(dimension_semantics=("arbitrary", "arbitrary"),
                                             vmem_limit_bytes=limit),
        name="in_proj",
    )(h, gain, w_all)


def _group_attention_into(q_ref, k_ref, v_ref, kh_ref, vh_ref, bias_ref, o_scr, l_scr, *, dilation, scale):
    first_chunk = pl.program_id(1) == 0
    n_sub = q_ref.shape[1] // (dilation * Q_BLOCK)

    def stream_rows(start):
        if dilation == 1:
            return pl.ds(start, Q_BLOCK)
        return pl.ds(start, Q_BLOCK, stride=dilation)

    for r in range(dilation):
        k_prev = kh_ref[0, stream_rows(r), :].astype(BF16)
        v_prev = vh_ref[0, stream_rows(r), :].astype(BF16)
        for n in range(n_sub):
            rows = stream_rows(r + n * Q_BLOCK * dilation)
            q = q_ref[0, rows, :].astype(BF16)
            k_cur = k_ref[0, rows, :].astype(BF16)
            v_cur = v_ref[0, rows, :].astype(BF16)
            kc = jnp.concatenate([k_prev, k_cur], axis=0)
            vc = jnp.concatenate([v_prev, v_cur], axis=0)
            s = lax.dot_general(q, kc, (((1,), (1,)), ((), ())), preferred_element_type=F32)
            s = s * scale + bias_ref[0]
            if n == 0:
                key_col = lax.broadcasted_iota(jnp.int32, s.shape, 1)
                s = jnp.where(jnp.logical_and(first_chunk, key_col < Q_BLOCK), -jnp.inf, s)
            m = jnp.max(s, axis=-1, keepdims=True)
            p = jnp.exp(s - m)
            den = jnp.sum(p, axis=-1, keepdims=True)
            o_scr[rows, :] = jnp.dot(p.astype(BF16), vc, preferred_element_type=F32) / den
            l_scr[rows, :] = jnp.broadcast_to(m + jnp.log(den), (Q_BLOCK, HEAD_DIM))
            k_prev, v_prev = k_cur, v_cur


def _attn_kernel(*refs, dilations, scale):
    n_groups = len(dilations)
    group_refs = [refs[6 * g:6 * (g + 1)] for g in range(n_groups)]
    out_ref = refs[6 * n_groups]
    scratch = refs[6 * n_groups + 1:]
    o_scrs, l_scrs = scratch[:n_groups], scratch[n_groups:]
    for g, dilation in enumerate(dilations):
        _group_attention_into(*group_refs[g], o_scrs[g], l_scrs[g], dilation=dilation, scale=scale)

    def mix_rows(rows):
        lses = [l[rows, :] for l in l_scrs]
        m = functools.reduce(jnp.maximum, lses)
        es = [jnp.exp(l - m) for l in lses]
        num = sum(e * o[rows, :] for e, o in zip(es, o_scrs))
        out_ref[0, rows, :] = num / sum(es)

    _for_row_chunks(out_ref.shape[1], mix_rows)


def _t5_causal_bucket(dist):
    max_exact = N_BUCKETS // 2
    df = jnp.maximum(dist, 1).astype(F32)
    large = max_exact + (jnp.log(df / max_exact) / math.log(MAX_DISTANCE / max_exact)
                         * (N_BUCKETS - max_exact)).astype(jnp.int32)
    large = jnp.minimum(large, N_BUCKETS - 1)
    return jnp.where(dist < max_exact, dist, large)


def _band_bias(bias_table, window, dilation):
    span = window // dilation
    qi = jnp.arange(Q_BLOCK)[:, None]
    kj = jnp.arange(2 * Q_BLOCK)[None, :]
    step = qi + Q_BLOCK - kj
    valid = (step >= 0) & (step <= span)
    bucket = _t5_causal_bucket(jnp.clip(step, 0, span) * dilation)
    onehot = (bucket[:, :, None] == jnp.arange(N_BUCKETS)[None, None, :]).astype(F32)
    bias = jnp.einsum("qkb,bh->hqk", onehot, bias_table, precision=lax.Precision.HIGHEST)
    return jnp.where(valid[None], bias, -jnp.inf)


def _attention(z, biases, dilations, *, batch, seq, attn_width, chunk=2048):
    assert seq % chunk == 0 and attn_width % HEAD_DIM == 0
    zv = z.reshape(batch, seq, z.shape[1])
    blk = _nbytes((chunk, HEAD_DIM), F32)
    in_specs, operands, pipelined = [], [], blk
    for group, (dilation, bias) in enumerate(zip(dilations, biases)):
        hist = Q_BLOCK * dilation
        assert chunk % hist == 0
        q_col = group * HEADS_PER_GROUP
        k_col = q_col + attn_width // HEAD_DIM
        v_col = k_col + attn_width // HEAD_DIM

        def cur(col):
            return pl.BlockSpec((1, chunk, HEAD_DIM), lambda b, c, hh, col=col: (b, c, col + hh))

        def prev(col, hist=hist):
            return pl.BlockSpec(
                (1, hist, HEAD_DIM),
                lambda b, c, hh, col=col, hist=hist: (b, jnp.maximum(c * (chunk // hist) - 1, 0), col + hh))

        in_specs += [cur(q_col), cur(k_col), cur(v_col), prev(k_col), prev(v_col),
                     pl.BlockSpec((1,) + bias.shape[1:], lambda b, c, hh: (hh, 0, 0))]
        operands += [zv, zv, zv, zv, zv, bias]
        pipelined += 3 * blk + 2 * _nbytes((hist, HEAD_DIM), F32) + _nbytes(bias.shape[1:], F32)
    n_groups = len(dilations)
    o = pl.pallas_call(
        functools.partial(_attn_kernel, dilations=tuple(dilations), scale=HEAD_DIM ** -0.5),
        grid=(batch, seq // chunk, HEADS_PER_GROUP),
        in_specs=in_specs,
        out_specs=pl.BlockSpec((1, chunk, HEAD_DIM), lambda b, c, hh: (b, c, hh)),
        out_shape=jax.ShapeDtypeStruct((batch, seq, GROUP_WIDTH), F32),
        scratch_shapes=[pltpu.VMEM((chunk, HEAD_DIM), F32)] * (2 * n_groups),
        compiler_params=pltpu.CompilerParams(
            dimension_semantics=("arbitrary",) * 3,
            vmem_limit_bytes=_vmem_limit(pipelined, 2 * n_groups * blk, 4 * blk)),
        name="dilated_attention",
    )(*operands)
    return o.reshape(batch * seq, GROUP_WIDTH)


def _merge_kernel(oattn_ref, p_ref, pp_ref, ga0_ref, ga1_ref, gb0_ref, gb1_ref, h_ref,
                  wpool_ref, pscale_ref, wao_ref, wpo_ref, wout_ref, gain_ref, out_ref,
                  *, tm, tiles_per_seq):
    tile_in_seq = pl.program_id(0) % tiles_per_seq

    halo = jnp.where(tile_in_seq == 0, 0.0, pp_ref[...])
    xe = jnp.concatenate([halo, p_ref[...]], axis=0)
    t = tile_in_seq * tm + lax.broadcasted_iota(jnp.int32, (tm, 1), 0)
    pooled = []
    for g, size in enumerate(POOL_SIZES):
        a = xe[:, g * POOL_GROUP_WIDTH:(g + 1) * POOL_GROUP_WIDTH]
        s, shift = a, 1
        while shift < size:
            s = s + pltpu.roll(s, shift, axis=0)
            shift *= 2
        count = jnp.minimum(t + 1, size).astype(F32)
        y = (s[POOL_HALO:] / count - a[POOL_HALO:]).astype(BF16)
        pooled.append(jnp.dot(y, wpool_ref[g], preferred_element_type=F32))
    o_pool = jnp.concatenate(pooled, axis=1) * pscale_ref[...]

    attn_proj = jnp.dot(oattn_ref[...].astype(BF16), wao_ref[...], preferred_element_type=F32)
    pool_proj = jnp.dot(o_pool.astype(BF16), wpo_ref[...], preferred_element_type=F32)
    gate_a = jax.nn.sigmoid(jnp.concatenate([ga0_ref[...], ga1_ref[...]], axis=1))
    gate_b = jax.nn.sigmoid(jnp.concatenate([gb0_ref[...], gb1_ref[...]], axis=1))
    merged = gate_a * attn_proj + gate_b * pool_proj
    mix = jnp.dot(merged.astype(BF16), wout_ref[...], preferred_element_type=F32)
    out_ref[...] = h_ref[...] + _rms_norm_f32(mix, gain_ref[...])


def _merge(h, z, o_attn, w_pool, pool_scale, w_attn_out, w_pool_out, w_out, gain, *, seq, attn_width, tm=512):
    n, d = h.shape
    pool_width = w_pool.shape[0] * POOL_GROUP_WIDTH
    assert pool_width == GROUP_WIDTH
    assert n % tm == 0 and seq % tm == 0 and tm % POOL_HALO == 0
    p_col = 3 * attn_width // GROUP_WIDTH
    gate_w = d // 2
    gate_col = (3 * attn_width + pool_width) // gate_w
    assert (3 * attn_width + pool_width) % gate_w == 0

    row_blk = lambda c: pl.BlockSpec((tm, GROUP_WIDTH), lambda i: (i, c))
    gate_blk = lambda c: pl.BlockSpec((tm, gate_w), lambda i: (i, gate_col + c))
    halo_blk = pl.BlockSpec((POOL_HALO, GROUP_WIDTH),
                            lambda i: (jnp.maximum(i * (tm // POOL_HALO) - 1, 0), p_col))
    full = lambda a: pl.BlockSpec(a.shape, lambda i: (0,) * a.ndim)
    weights = (w_pool, pool_scale, w_attn_out, w_pool_out, w_out, gain)

    act = 2 * _nbytes((tm, GROUP_WIDTH), F32) + 4 * _nbytes((tm, gate_w), F32) + 2 * _nbytes((tm, d), F32)
    wbytes = sum(_nbytes(a.shape, a.dtype) for a in weights)
    limit = _vmem_limit(act + wbytes, 0, 8 * _nbytes((tm, d), F32))
    return pl.pallas_call(
        functools.partial(_merge_kernel, tm=tm, tiles_per_seq=seq // tm),
        grid=(n // tm,),
        in_specs=[row_blk(0), row_blk(p_col), halo_blk] + [gate_blk(c) for c in range(4)]
                 + [pl.BlockSpec((tm, d), lambda i: (i, 0))] + [full(a) for a in weights],
        out_specs=pl.BlockSpec((tm, d), lambda i: (i, 0)),
        out_shape=jax.ShapeDtypeStruct((n, d), F32),
        compiler_params=pltpu.CompilerParams(dimension_semantics=("arbitrary",), vmem_limit_bytes=limit),
        name="mixer_merge",
    )(o_attn, z, z, z, z, z, z, h, *weights)


def _unpack_bf16_pair(words):
    lo = lax.bitcast_convert_type(words << 16, F32)
    hi = lax.bitcast_convert_type(words & jnp.uint32(0xFFFF0000), F32)
    return jnp.concatenate([lo, hi], axis=1).astype(BF16)


def _swiglu(u, wg_ref, wu_ref, wd_ref):
    gate = jnp.dot(u, wg_ref[...].astype(BF16), preferred_element_type=F32)
    up = jnp.dot(u, wu_ref[...].astype(BF16), preferred_element_type=F32)
    hidden = (gate * jax.nn.sigmoid(gate) * up).astype(BF16)
    return jnp.dot(hidden, wd_ref[...].astype(BF16), preferred_element_type=F32)


def _swiglu_accumulate(u_scr, wg_ref, wu_ref, wd_ref, acc_ref, rows=slice(None)):
    acc_ref[rows, :] += _swiglu(u_scr[rows, :], wg_ref, wu_ref, wd_ref)


def _ffn_vmem_limit(tm, d, tf, x_bytes, scratch_bytes):
    return _vmem_limit(
        x_bytes + 3 * _nbytes((d, tf), F32) + _nbytes((tm, d), F32),
        _nbytes((tm, d), BF16) + scratch_bytes,
        3 * _nbytes((d, tf), BF16) + 4 * _nbytes((tm, tf), F32) + 2 * _nbytes((tm, d), F32))


def _dense_ffn_kernel(h_ref, gin_ref, wg_ref, wu_ref, wd_ref, gout_ref, out_ref, u_scr):
    j = pl.program_id(1)
    last = pl.num_programs(1) - 1
    chunks = _static_row_chunks(u_scr.shape[0])

    @pl.when(j == 0)
    def _():
        for rows in chunks:
            u = _rms_norm_f32(h_ref[rows, :], gin_ref[...]).astype(BF16)
            u_scr[rows, :] = u
            out_ref[rows, :] = _swiglu(u, wg_ref, wu_ref, wd_ref)

    @pl.when(jnp.logical_and(j > 0, j < last))
    def _():
        _swiglu_accumulate(u_scr, wg_ref, wu_ref, wd_ref, out_ref)

    @pl.when(j == last)
    def _():
        for rows in chunks:
            f = out_ref[rows, :] + _swiglu(u_scr[rows, :], wg_ref, wu_ref, wd_ref)
            out_ref[rows, :] = h_ref[rows, :] + _rms_norm_f32(f, gout_ref[...])


def _dense_ffn(h, gain_in, gain_out, w_gate, w_up, w_down, *, tm=1024, tf=256):
    n, d = h.shape
    ff = w_gate.shape[1]
    assert n % tm == 0 and ff % tf == 0 and ff // tf >= 2
    row_blk = pl.BlockSpec((tm, d), lambda i, j: (i, 0))
    gain_blk = pl.BlockSpec((1, d), lambda i, j: (0, 0))
    up_blk = pl.BlockSpec((d, tf), lambda i, j: (0, j))
    return pl.pallas_call(
        _dense_ffn_kernel,
        grid=(n // tm, ff // tf),
        in_specs=[row_blk, gain_blk, up_blk, up_blk, pl.BlockSpec((tf, d), lambda i, j: (j, 0)), gain_blk],
        out_specs=row_blk,
        out_shape=jax.ShapeDtypeStruct((n, d), F32),
        scratch_shapes=[pltpu.VMEM((tm, d), BF16)],
        compiler_params=pltpu.CompilerParams(
            dimension_semantics=("arbitrary", "arbitrary"),
            vmem_limit_bytes=_ffn_vmem_limit(tm, d, tf, _nbytes((tm, d), F32), 0)),
        name="swiglu_dense",
    )(h, gain_in, w_gate, w_up, w_down, gain_out)


def _expert_ffn_kernel(tile_expert_ref, tile_rows_ref, src_ref, x_hbm, wg_ref, wu_ref, wd_ref, out_ref,
                       u_scr, stage, sem):
    del tile_expert_ref
    i, j = pl.program_id(0), pl.program_id(1)
    tm = stage.shape[0]
    granule = EXPERT_ROW_GRANULE
    n_chunks = tile_rows_ref[i] // granule

    @pl.when(j == 0)
    def _():
        def zero_rows(rows):
            out_ref[rows, :] = jnp.zeros((ROW_CHUNK, out_ref.shape[1]), F32)

        _for_row_chunks(tm, zero_rows)

        def gather_chunk(c, carry):
            base = c * granule

            def start(r, inner):
                pltpu.make_async_copy(x_hbm.at[pl.ds(src_ref[i * tm + base + r], 1), :],
                                      stage.at[pl.ds(base + r, 1), :], sem).start()
                return inner

            return lax.fori_loop(0, granule, start, carry, unroll=DMA_ISSUE_UNROLL)

        lax.fori_loop(0, n_chunks, gather_chunk, 0)

        def wait_chunk(c, carry):
            pltpu.make_async_copy(x_hbm.at[pl.ds(0, granule), :], stage.at[pl.ds(0, granule), :], sem).wait()
            return carry

        lax.fori_loop(0, n_chunks, wait_chunk, 0)

        def unpack_chunk(c, carry):
            rows = pl.ds(pl.multiple_of(c * granule, granule), granule)
            u_scr[rows, :] = _unpack_bf16_pair(stage[rows, :])
            return carry

        lax.fori_loop(0, n_chunks, unpack_chunk, 0)

    for k in range(1, tm // granule + 1):
        @pl.when(n_chunks == k)
        def _(k=k):
            _swiglu_accumulate(u_scr, wg_ref, wu_ref, wd_ref, out_ref, slice(0, k * granule))


def _expert_ffn(xp, src, tile_expert, tile_rows, w_gate, w_up, w_down, *, tm, tf):
    rows = src.shape[0]
    n_exp, d, ff = w_gate.shape
    assert rows % tm == 0 and tm % ROW_CHUNK == 0 and tm % EXPERT_ROW_GRANULE == 0
    assert ff % tf == 0 and xp.shape[1] == d // 2
    nj = ff // tf

    def ff_tile(i, j, tr):
        return jnp.where(tr[i] > 0, j, nj - 1)

    up_blk = pl.BlockSpec((None, d, tf), lambda i, j, te, tr, src: (te[i], 0, ff_tile(i, j, tr)))
    down_blk = pl.BlockSpec((None, tf, d), lambda i, j, te, tr, src: (te[i], ff_tile(i, j, tr), 0))
    return pl.pallas_call(
        _expert_ffn_kernel,
        grid_spec=pltpu.PrefetchScalarGridSpec(
            num_scalar_prefetch=3,
            grid=(rows // tm, nj),
            in_specs=[pl.BlockSpec(memory_space=pl.ANY), up_blk, up_blk, down_blk],
            out_specs=pl.BlockSpec((tm, d), lambda i, j, te, tr, src: (i, 0)),
            scratch_shapes=[pltpu.VMEM((tm, d), BF16), pltpu.VMEM((tm, d // 2), jnp.uint32),
                            pltpu.SemaphoreType.DMA]),
        out_shape=jax.ShapeDtypeStruct((rows, d), F32),
        compiler_params=pltpu.CompilerParams(
            dimension_semantics=("arbitrary", "arbitrary"),
            vmem_limit_bytes=_ffn_vmem_limit(tm, d, tf, 0, _nbytes((tm, d // 2), jnp.uint32))),
        name="swiglu_experts",
    )(tile_expert, tile_rows, src, xp, w_gate, w_up, w_down)


def _router_kernel(h_ref, gain_ref, wrt_ref, xp_ref, idx_ref, wts_ref):
    u = _rms_norm_f32(h_ref[...], gain_ref[...])
    logits = lax.dot_general(wrt_ref[...], u, (((1,), (1,)), ((), ())),
                             precision=lax.Precision.HIGHEST, preferred_element_type=F32)
    n_exp = logits.shape[0]
    expert = lax.broadcasted_iota(jnp.int32, logits.shape, 0)
    v1 = jnp.max(logits, axis=0, keepdims=True)
    i1 = jnp.min(jnp.where(logits == v1, expert, n_exp), axis=0, keepdims=True)
    rest = jnp.where(expert == i1, -jnp.inf, logits)
    v2 = jnp.max(rest, axis=0, keepdims=True)
    i2 = jnp.min(jnp.where(rest == v2, expert, n_exp), axis=0, keepdims=True)
    e2 = jnp.exp(v2 - v1)
    idx_ref[...] = jnp.concatenate([i1, i2], axis=0)
    wts_ref[...] = jnp.concatenate([1.0 / (1.0 + e2), e2 / (1.0 + e2)], axis=0)

    bits = lax.bitcast_convert_type(u.astype(BF16).astype(F32), jnp.uint32)
    half = bits.shape[1] // 2
    xp_ref[...] = (bits[:, :half] >> 16) | bits[:, half:]


def _router(h, gain, w_router_t, *, tm=512):
    n, d = h.shape
    n_exp = w_router_t.shape[0]
    assert n % tm == 0
    limit = _vmem_limit(_nbytes((tm, d), F32) + _nbytes((tm, d // 2), jnp.uint32) + _nbytes((n_exp, d), F32),
                        0, 6 * _nbytes((tm, d), F32))
    return pl.pallas_call(
        _router_kernel,
        grid=(n // tm,),
        in_specs=[pl.BlockSpec((tm, d), lambda i: (i, 0)),
                  pl.BlockSpec((1, d), lambda i: (0, 0)),
                  pl.BlockSpec((n_exp, d), lambda i: (0, 0))],
        out_specs=[pl.BlockSpec((tm, d // 2), lambda i: (i, 0)),
                   pl.BlockSpec((TOP_K, tm), lambda i: (0, i)),
                   pl.BlockSpec((TOP_K, tm), lambda i: (0, i))],
        out_shape=[jax.ShapeDtypeStruct((n, d // 2), jnp.uint32),
                   jax.ShapeDtypeStruct((TOP_K, n), jnp.int32),
                   jax.ShapeDtypeStruct((TOP_K, n), F32)],
        compiler_params=pltpu.CompilerParams(dimension_semantics=("arbitrary",), vmem_limit_bytes=limit),
        name="moe_router",
    )(h, gain, w_router_t)


def _combine_kernel(pos_ref, y_hbm, wts_ref, h_ref, gain_ref, out_ref, bufs, sems, *, tm):
    i = pl.program_id(0)

    def start_tile(tile, slot):
        def start(r, carry):
            for k in range(TOP_K):
                pltpu.make_async_copy(y_hbm.at[pl.ds(pos_ref[TOP_K * (tile * tm + r) + k], 1), :],
                                      bufs.at[slot, k, pl.ds(r, 1), :], sems.at[slot, k]).start()
            return carry

        lax.fori_loop(0, tm, start, 0, unroll=DMA_ISSUE_UNROLL)

    @pl.when(i == 0)
    def _():
        start_tile(0, 0)

    @pl.when(i + 1 < pl.num_programs(0))
    def _():
        start_tile(i + 1, (i + 1) % 2)

    slot = i % 2
    for k in range(TOP_K):
        pltpu.make_async_copy(y_hbm.at[pl.ds(0, tm), :], bufs.at[slot, k], sems.at[slot, k]).wait()
    w = wts_ref[...]
    y = w[:, 0:1] * bufs[slot, 0] + w[:, 1:2] * bufs[slot, 1]
    out_ref[...] = h_ref[...] + _rms_norm_f32(y, gain_ref[...])


def _combine(y_rows, pos, wts, h, gain, *, tm=512):
    n, d = h.shape
    assert n % tm == 0
    limit = _vmem_limit(2 * _nbytes((tm, d), F32) + _nbytes((tm, 128), F32),
                        2 * TOP_K * _nbytes((tm, d), F32), 4 * _nbytes((tm, d), F32))
    return pl.pallas_call(
        functools.partial(_combine_kernel, tm=tm),
        grid_spec=pltpu.PrefetchScalarGridSpec(
            num_scalar_prefetch=1,
            grid=(n // tm,),
            in_specs=[pl.BlockSpec(memory_space=pl.ANY),
                      pl.BlockSpec((tm, TOP_K), lambda i, pos: (i, 0)),
                      pl.BlockSpec((tm, d), lambda i, pos: (i, 0)),
                      pl.BlockSpec((1, d), lambda i, pos: (0, 0))],
            out_specs=pl.BlockSpec((tm, d), lambda i, pos: (i, 0)),
            scratch_shapes=[pltpu.VMEM((2, TOP_K, tm, d), F32), pltpu.SemaphoreType.DMA((2, TOP_K))]),
        out_shape=jax.ShapeDtypeStruct((n, d), F32),
        compiler_params=pltpu.CompilerParams(dimension_semantics=("arbitrary",), vmem_limit_bytes=limit),
        name="moe_combine",
    )(pos, y_rows, wts, h, gain)


def _dispatch_plan(top_idx, n_exp, tm):
    n = top_idx.shape[1]
    n_assign = n * TOP_K
    n_tiles = n_assign // tm + n_exp
    expert = top_idx.T.reshape(n_assign)
    onehot = (expert[:, None] == jnp.arange(n_exp, dtype=jnp.int32)[None, :]).astype(jnp.int32)
    rank = jnp.sum((jnp.cumsum(onehot, axis=0) - onehot) * onehot, axis=1)
    counts = jnp.sum(onehot, axis=0)
    tiles_per_expert = (counts + tm - 1) // tm
    tile_ends = jnp.cumsum(tiles_per_expert)
    tile_starts = tile_ends - tiles_per_expert
    pos = tile_starts[expert] * tm + rank
    tile = jnp.arange(n_tiles, dtype=jnp.int32)
    tile_expert = jnp.sum((tile[:, None] >= tile_ends[None, :]).astype(jnp.int32), axis=1)
    used = tile < tile_ends[-1]
    tile_expert = jnp.where(used, tile_expert, tile_expert[tile_ends[-1] - 1])
    order = jnp.argsort(expert, stable=True).astype(jnp.int32) // TOP_K
    order = jnp.concatenate([order, jnp.zeros((tm,), jnp.int32)])
    group_start = jnp.cumsum(counts) - counts
    row_in_group = (tile - tile_starts[tile_expert]) * tm
    slice_start = jnp.clip(group_start[tile_expert] + row_in_group, 0, n_assign)
    tile_src = jax.vmap(lambda s: lax.dynamic_slice(order, (s,), (tm,)))(slice_start)
    local = row_in_group[:, None] + jnp.arange(tm, dtype=jnp.int32)[None, :]
    valid = jnp.logical_and(used[:, None], local < counts[tile_expert][:, None])
    src = jnp.where(valid, tile_src, 0).reshape(n_tiles * tm)
    granule = EXPERT_ROW_GRANULE
    rows_left = (counts[tile_expert] + granule - 1) // granule * granule - (tile - tile_starts[tile_expert]) * tm
    tile_rows = jnp.where(used, jnp.clip(rows_left, 0, tm), 0)
    return pos.astype(jnp.int32), src, tile_expert.astype(jnp.int32), tile_rows.astype(jnp.int32)


def _moe(h, gain_in, gain_out, w_router, w_gate, w_up, w_down, *, tm=1280, tf=512):
    n_exp = w_router.shape[1]
    xp, top_idx, top_w = _router(h, gain_in, w_router.T)
    pos, src, tile_expert, tile_rows = _dispatch_plan(top_idx, n_exp, tm)
    y_rows = _expert_ffn(xp, src, tile_expert, tile_rows, w_gate, w_up, w_down, tm=tm, tf=tf)
    return _combine(y_rows, pos, top_w.T, h, gain_out)


def kernel(x, w_in, rel_bias, w_pool, pool_scale, w_attn_out, w_pool_out, w_out, norm_gains,
           dense_w_gate, dense_w_up, dense_w_down, moe_w_router, moe_w_gate, moe_w_up, moe_w_down):
    batch, seq, d = x.shape
    depth = w_in.shape[0]
    attn_width = N_ATTN_GROUPS * GROUP_WIDTH
    h = x.reshape(batch * seq, d)
    biases = [_band_bias(rel_bias[:, g * HEADS_PER_GROUP:(g + 1) * HEADS_PER_GROUP], window, dilation)
              for g, (window, dilation) in enumerate(DILATION_PATTERNS)]
    for layer in range(depth):
        gains = norm_gains[layer].reshape(4, 1, d)
        z = _in_proj(h, gains[0], w_in, layer)
        o_attn = _attention(z, biases, [dilation for _, dilation in DILATION_PATTERNS],
                            batch=batch, seq=seq, attn_width=attn_width)
        h = _merge(h, z, o_attn, w_pool[layer].astype(BF16), pool_scale[layer].reshape(1, -1),
                   w_attn_out[layer].astype(BF16), w_pool_out[layer].astype(BF16), w_out[layer].astype(BF16),
                   gains[1], seq=seq, attn_width=attn_width)
        j = layer // 2
        if layer % 2 == 0:
            h = _dense_ffn(h, gains[2], gains[3], dense_w_gate[j], dense_w_up[j], dense_w_down[j])
        else:
            h = _moe(h, gains[2], gains[3], moe_w_router[j], moe_w_gate[j], moe_w_up[j], moe_w_down[j])
    return h.reshape(batch, seq, d)
```
